```python
import math
import jax, jax.numpy as jnp
from jax import lax
import numpy as np

D_MODEL = 1024
BATCH = 16
SEQ = 4096
DEPTH = 4

CHUNK = 64
N_MIXERS = 4
Q_BLOCK = 128
NORM_EPS = 1e-6
ROPE_BASE = 10000.0
PLE_DIM = 256

SSD_D_INNER = 2 * D_MODEL
SSD_HEAD_DIM = 64
SSD_HEADS = SSD_D_INNER // SSD_HEAD_DIM
SSD_GROUPS = 4
SSD_HEADS_PER_GROUP = SSD_HEADS // SSD_GROUPS
SSD_STATE = 128
SSD_CONV = 4
SSD_CONV_DIM = SSD_D_INNER + 2 * SSD_GROUPS * SSD_STATE
SSD_IN_DIM = SSD_D_INNER + SSD_CONV_DIM + SSD_HEADS

RET_HEADS = 4
RET_QK_DIM = D_MODEL // RET_HEADS
RET_V_DIM = 2 * RET_QK_DIM
RET_V_WIDTH = RET_HEADS * RET_V_DIM
RET_IN_DIM = 2 * D_MODEL + 2 * RET_V_WIDTH

DIFF_HEADS = 8
DIFF_HEAD_DIM = D_MODEL // DIFF_HEADS // 2
DIFF_IN_DIM = 3 * D_MODEL

FOX_HEADS = 16
FOX_HEAD_DIM = D_MODEL // FOX_HEADS
FOX_IN_DIM = 3 * D_MODEL + FOX_HEADS

REL_BUCKETS = 32
REL_MAX_DIST = 128

PEER_KEYS = 128
PEER_EXPERTS = PEER_KEYS * PEER_KEYS
PEER_HEADS = 8
PEER_TOPK = 16
PEER_QUERY_DIM = 256
PEER_HALF = PEER_QUERY_DIM // 2
PEER_BLOCK = 128

kernel_name = 'hybrid_chunk_causal_ssd_ret_diff_fox_peer'


def rms_norm(x, g):
    xf = x.astype(jnp.float32)
    y = xf * lax.rsqrt(jnp.mean(xf * xf, axis=-1, keepdims=True) + NORM_EPS)
    return (y * g.astype(jnp.float32)).astype(x.dtype)


def to_chunks(t, L):
    B, S = t.shape[:2]
    return jnp.moveaxis(t.reshape(B, S // L, L, *t.shape[2:]), 1, 0)


def from_chunks(t):
    n, B, L = t.shape[:3]
    return jnp.moveaxis(t, 0, 1).reshape(B, n * L, *t.shape[3:])


def rotary(x):
    S, d = x.shape[1], x.shape[-1]
    inv = 1.0 / (ROPE_BASE ** (jnp.arange(0, d, 2, dtype=jnp.float32) / d))
    ang = jnp.arange(S, dtype=jnp.float32)[:, None] * inv[None, :]
    cos = jnp.cos(ang)[None, :, None, :]
    sin = jnp.sin(ang)[None, :, None, :]
    xf = x.astype(jnp.float32)
    x1, x2 = xf[..., : d // 2], xf[..., d // 2:]
    return jnp.concatenate([x1 * cos - x2 * sin, x1 * sin + x2 * cos], axis=-1).astype(x.dtype)


def t5_bucket(rel):
    nb = REL_BUCKETS // 2
    max_exact = nb // 2
    ret = (rel > 0).astype(jnp.int32) * nb
    n = jnp.abs(rel)
    nf = jnp.maximum(n, 1).astype(jnp.float32)
    large = max_exact + (jnp.log(nf / max_exact) / math.log(REL_MAX_DIST / max_exact)
                         * (nb - max_exact)).astype(jnp.int32)
    large = jnp.minimum(large, nb - 1)
    return ret + jnp.where(n < max_exact, n, large)


def causal_depthwise_conv(x, w, b):
    K = w.shape[0]
    y = lax.conv_general_dilated(x, w, window_strides=(1,), padding=[(K - 1, 0)],
                                 dimension_numbers=('NWC', 'WIO', 'NWC'),
                                 feature_group_count=x.shape[-1])
    return y + b


def ssd_chunk_step(state, inp):
    xc, ac, dtc, bc, cc = inp
    L = xc.shape[1]
    cum = jnp.cumsum(ac, axis=1)
    causal = jnp.tril(jnp.ones((L, L), dtype=bool))[None, :, :, None, None]
    seg = jnp.where(causal, cum[:, :, None] - cum[:, None, :], -jnp.inf)
    cb = jnp.einsum('btgn,bsgn->btsg', cc, bc)
    w = jnp.exp(seg) * cb[..., None] * dtc[:, None]
    y = jnp.einsum('btsgr,bsgrp->btgrp', w, xc)
    y = y + jnp.einsum('btgn,bgrpn->btgrp', cc, state) * jnp.exp(cum)[..., None]
    last = cum[:, -1]
    wst = jnp.exp(last[:, None] - cum) * dtc
    state = state * jnp.exp(last)[..., None, None] + jnp.einsum('bsgr,bsgrp,bsgn->bgrpn', wst, xc, bc)
    return state, y


def ssd_mixer(h, w_in, conv_w, conv_b, dt_bias, a_log, d_skip, norm_g, w_out):
    B, S, _ = h.shape
    G, R, P, N = SSD_GROUPS, SSD_HEADS_PER_GROUP, SSD_HEAD_DIM, SSD_STATE
    DI = SSD_D_INNER
    proj = h @ w_in
    z = proj[..., :DI]
    xbc = jax.nn.silu(causal_depthwise_conv(proj[..., DI:DI + SSD_CONV_DIM], conv_w, conv_b))
    dt = jax.nn.softplus((proj[..., DI + SSD_CONV_DIM:] + dt_bias).astype(jnp.float32))
    xs = xbc[..., :DI].reshape(B, S, G, R, P)
    bm = xbc[..., DI:DI + G * N].reshape(B, S, G, N)
    cm = xbc[..., DI + G * N:].reshape(B, S, G, N)
    a = (dt * -jnp.exp(a_log.astype(jnp.float32))).reshape(B, S, G, R)
    dt = dt.reshape(B, S, G, R)
    state0 = jnp.zeros((B, G, R, P, N), jnp.float32)
    _, y = lax.scan(ssd_chunk_step, state0,
                    (to_chunks(xs, CHUNK), to_chunks(a, CHUNK), to_chunks(dt, CHUNK),
                     to_chunks(bm, CHUNK), to_chunks(cm, CHUNK)))
    y = from_chunks(y) + d_skip.astype(jnp.float32).reshape(G, R, 1) * xs
    y = y.reshape(B, S, DI).astype(h.dtype) * jax.nn.silu(z)
    y = rms_norm(y.reshape(B, S, G, DI // G), norm_g.reshape(G, DI // G))
    return y.reshape(B, S, DI) @ w_out


def retention(h, w_in, norm_g, w_out):
    B, S, _ = h.shape
    H, dk, dv, L = RET_HEADS, RET_QK_DIM, RET_V_DIM, CHUNK
    proj = h @ w_in
    q = rotary(proj[..., :D_MODEL].reshape(B, S, H, dk))
    k = rotary(proj[..., D_MODEL:2 * D_MODEL].reshape(B, S, H, dk)) * (dk ** -0.5)
    v = proj[..., 2 * D_MODEL:2 * D_MODEL + RET_V_WIDTH].reshape(B, S, H, dv)
    gate = proj[..., 2 * D_MODEL + RET_V_WIDTH:]
    log_gamma = jnp.log1p(-jnp.exp2(-5.0 - jnp.arange(H, dtype=jnp.float32)))
    idx = jnp.arange(L, dtype=jnp.float32)
    intra = jnp.exp(log_gamma[:, None, None] * jnp.abs(idx[:, None] - idx[None, :]))
    q_decay = jnp.exp(log_gamma[None, :] * (idx[:, None] + 1.0))
    k_decay = jnp.exp(log_gamma[None, :] * (L - 1.0 - idx[:, None]))
    chunk_decay = jnp.exp(log_gamma * L)

    def step(state, inp):
        qc, kc, vc = inp
        s = jnp.einsum('blhd,bmhd->bhlm', qc, kc) * intra[None]
        o = jnp.einsum('bhlm,bmhe->blhe', s, vc)
        o = o + jnp.einsum('blhd,bhde->blhe', qc, state) * q_decay[None, :, :, None]
        state = state * chunk_decay[None, :, None, None] + jnp.einsum(
            'blhd,blhe->bhde', kc * k_decay[None, :, :, None], vc)
        return state, o

    state0 = jnp.zeros((B, H, dk, dv), jnp.float32)
    _, o = lax.scan(step, state0, (to_chunks(q, L), to_chunks(k, L), to_chunks(v, L)))
    o = rms_norm(from_chunks(o).astype(h.dtype), norm_g.reshape(H, dv))
    o = o.reshape(B, S, RET_V_WIDTH) * jax.nn.silu(gate)
    return o @ w_out


def diff_attention(h, w_in, lam_vecs, norm_g, w_out, rel_bias, lam_init):
    B, S, _ = h.shape
    H, dh = DIFF_HEADS, DIFF_HEAD_DIM
    proj = h @ w_in
    q = proj[..., :D_MODEL].reshape(B, S, H, 2, dh)
    k = proj[..., D_MODEL:2 * D_MODEL].reshape(B, S, H, 2, dh)
    v = proj[..., 2 * D_MODEL:].reshape(B, S, H, 2 * dh)
    lv = lam_vecs.astype(jnp.float32)
    lam = jnp.exp(jnp.sum(lv[0] * lv[1])) - jnp.exp(jnp.sum(lv[2] * lv[3])) + lam_init
    scale = dh ** -0.5
    pos = jnp.arange(S)
    nb = S // Q_BLOCK

    def block(args):
        qb, t0 = args
        tq = t0 + jnp.arange(Q_BLOCK)
        s = jnp.einsum('bqhmd,bkhmd->bhmqk', qb, k).astype(jnp.float32) * scale
        bias = rel_bias[t5_bucket(pos[None, :] - tq[:, None])].astype(jnp.float32)
        s = s + jnp.transpose(bias, (2, 0, 1))[None, :, None]
        mask = (pos[None, :] // CHUNK) <= (tq[:, None] // CHUNK)
        a = jax.nn.softmax(jnp.where(mask, s, -jnp.inf), axis=-1)
        a = a[:, :, 0] - lam * a[:, :, 1]
        return jnp.einsum('bhqk,bkhe->bqhe', a.astype(v.dtype), v)

    o = lax.map(block, (to_chunks(q, Q_BLOCK), jnp.arange(nb) * Q_BLOCK))
    o = rms_norm(from_chunks(o), norm_g) * (1.0 - lam_init)
    return o.reshape(B, S, D_MODEL) @ w_out


def forgetting_attention(h, w_in, b_f, w_out):
    B, S, _ = h.shape
    H, dh = FOX_HEADS, FOX_HEAD_DIM
    proj = h @ w_in
    q = proj[..., :D_MODEL].reshape(B, S, H, dh)
    k = proj[..., D_MODEL:2 * D_MODEL].reshape(B, S, H, dh)
    v = proj[..., 2 * D_MODEL:3 * D_MODEL].reshape(B, S, H, dh)
    log_f = jax.nn.log_sigmoid((proj[..., 3 * D_MODEL:] + b_f).astype(jnp.float32))
    c = jnp.cumsum(log_f, axis=1)
    c_key = jnp.transpose(c, (0, 2, 1))
    scale = dh ** -0.5
    pos = jnp.arange(S)
    nb = S // Q_BLOCK

    def block(args):
        qb, cq, t0 = args
        tq = t0 + jnp.arange(Q_BLOCK)
        s = jnp.einsum('bqhd,bkhd->bhqk', qb, k).astype(jnp.float32) * scale
        s = s + jnp.transpose(cq, (0, 2, 1))[..., :, None] - c_key[..., None, :]
        mask = pos[None, :] <= tq[:, None]
        a = jax.nn.softmax(jnp.where(mask, s, -jnp.inf), axis=-1)
        return jnp.einsum('bhqk,bkhd->bqhd', a.astype(v.dtype), v)

    o = lax.map(block, (to_chunks(q, Q_BLOCK), to_chunks(c, Q_BLOCK), jnp.arange(nb) * Q_BLOCK))
    return from_chunks(o).reshape(B, S, D_MODEL) @ w_out


def peer_ffn(h, w_q, sub_keys, u, v):
    B, S, D = h.shape
    T = B * S
    K, NK = PEER_TOPK, PEER_KEYS
    xt = h.reshape(T, D)
    q = (xt @ w_q).reshape(T, PEER_HEADS, 2, PEER_HALF)
    sc = jnp.einsum('thcd,hckd->thck', q, sub_keys).astype(jnp.float32)
    s1, i1 = lax.top_k(sc[:, :, 0], K)
    s2, i2 = lax.top_k(sc[:, :, 1], K)
    cand = (s1[..., :, None] + s2[..., None, :]).reshape(T, PEER_HEADS, K * K)
    cand_idx = (i1[..., :, None] * NK + i2[..., None, :]).reshape(T, PEER_HEADS, K * K)
    top_s, top_pos = lax.top_k(cand, K)
    idx = jnp.take_along_axis(cand_idx, top_pos, axis=-1)
    gate = jax.nn.softmax(top_s, axis=-1)
    nb = T // PEER_BLOCK

    def block(args):
        xb, ib, gb = args
        hid = jax.nn.gelu(jnp.einsum('tkd,td->tk', u[ib], xb), approximate=False)
        return jnp.einsum('tk,tkd->td', (gb * hid).astype(xb.dtype), v[ib])

    out = lax.map(block, (xt.reshape(nb, PEER_BLOCK, D),
                          idx.reshape(nb, PEER_BLOCK, PEER_HEADS * K),
                          gate.reshape(nb, PEER_BLOCK, PEER_HEADS * K)))
    return out.reshape(B, S, D)


def setup_inputs(seed: int = 0) -> dict:
    key = jax.random.key(seed)
    ks = iter(jax.random.split(key, 64))
    f32 = jnp.float32
    D = D_MODEL

    def nrm(shape, scale):
        return jax.random.normal(next(ks), shape, f32) * scale

    def gain(shape):
        return 1.0 + 0.05 * jax.random.normal(next(ks), shape, f32)

    def unif(shape, lo, hi):
        return jax.random.uniform(next(ks), shape, f32, lo, hi)

    nA, nB, nC, nD = [len(range(m, DEPTH, N_MIXERS)) for m in range(N_MIXERS)]
    dt0 = jnp.exp(unif((nA, SSD_HEADS), math.log(1e-3), math.log(1e-1)))
    return {
        'x': nrm((BATCH, SEQ, D), 1.0),
        'p': nrm((DEPTH, BATCH, SEQ, PLE_DIM), 1.0),
        'mix_norm': gain((DEPTH, D)),
        'ffn_norm': gain((DEPTH, D)),
        'ple_norm': gain((DEPTH, D)),
        'final_norm': gain((D,)),
        'ssd_w_in': nrm((nA, D, SSD_IN_DIM), D ** -0.5),
        'ssd_conv_w': nrm((nA, SSD_CONV, 1, SSD_CONV_DIM), SSD_CONV ** -0.5),
        'ssd_conv_b': nrm((nA, SSD_CONV_DIM), 0.02),
        'ssd_dt_bias': dt0 + jnp.log(-jnp.expm1(-dt0)),
        'ssd_a_log': jnp.log(unif((nA, SSD_HEADS), 1.0, 16.0)),
        'ssd_d': gain((nA, SSD_HEADS)),
        'ssd_norm': gain((nA, SSD_D_INNER)),
        'ssd_w_out': nrm((nA, SSD_D_INNER, D), SSD_D_INNER ** -0.5),
        'ret_w_in': nrm((nB, D, RET_IN_DIM), D ** -0.5),
        'ret_norm': gain((nB, RET_V_WIDTH)),
        'ret_w_out': nrm((nB, RET_V_WIDTH, D), RET_V_WIDTH ** -0.5),
        'diff_w_in': nrm((nC, D, DIFF_IN_DIM), D ** -0.5),
        'diff_lambda': nrm((nC, 4, DIFF_HEAD_DIM), 0.1),
        'diff_norm': gain((nC, 2 * DIFF_HEAD_DIM)),
        'diff_w_out': nrm((nC, D, D), D ** -0.5),
        'fox_w_in': nrm((nD, D, FOX_IN_DIM), D ** -0.5),
        'fox_b_f': unif((nD, FOX_HEADS), 1.0, 4.0),
        'fox_w_out': nrm((nD, D, D), D ** -0.5),
        'rel_bias': nrm((REL_BUCKETS, DIFF_HEADS), 0.3),
        'peer_w_q': nrm((DEPTH, D, PEER_HEADS * PEER_QUERY_DIM), D ** -0.5),
        'peer_keys': nrm((DEPTH, PEER_HEADS, 2, PEER_KEYS, PEER_HALF), PEER_HALF ** -0.5),
        'peer_u': nrm((DEPTH, PEER_EXPERTS, D), D ** -0.5),
        'peer_v': nrm((DEPTH, PEER_EXPERTS, D), (PEER_HEADS * PEER_TOPK) ** -0.5),
        'ple_proj': nrm((DEPTH, PLE_DIM, D), PLE_DIM ** -0.5),
        'ple_gate': nrm((DEPTH, D, D), D ** -0.5),
    }


def reference(x, p, mix_norm, ffn_norm, ple_norm, final_norm,
              ssd_w_in, ssd_conv_w, ssd_conv_b, ssd_dt_bias, ssd_a_log, ssd_d, ssd_norm, ssd_w_out,
              ret_w_in, ret_norm, ret_w_out,
              diff_w_in, diff_lambda, diff_norm, diff_w_out,
              fox_w_in, fox_b_f, fox_w_out,
              rel_bias,
              peer_w_q, peer_keys, peer_u, peer_v,
              ple_proj, ple_gate):
    h = x
    for i in range(DEPTH):
        m, j = i % N_MIXERS, i // N_MIXERS
        hn = rms_norm(h, mix_norm[i])
        if m == 0:
            y = ssd_mixer(hn, ssd_w_in[j], ssd_conv_w[j], ssd_conv_b[j], ssd_dt_bias[j],
                          ssd_a_log[j], ssd_d[j], ssd_norm[j], ssd_w_out[j])
        elif m == 1:
            y = retention(hn, ret_w_in[j], ret_norm[j], ret_w_out[j])
        elif m == 2:
            lam_init = 0.8 - 0.6 * math.exp(-0.3 * i)
            y = diff_attention(hn, diff_w_in[j], diff_lambda[j], diff_norm[j], diff_w_out[j],
                               rel_bias, lam_init)
        else:
            y = forgetting_attention(hn, fox_w_in[j], fox_b_f[j], fox_w_out[j])
        h = h + y
        h = h + peer_ffn(rms_norm(h, ffn_norm[i]), peer_w_q[i], peer_keys[i], peer_u[i], peer_v[i])
        g = jax.nn.sigmoid(rms_norm(h, ple_norm[i]) @ ple_gate[i])
        h = h + g * (p[i] @ ple_proj[i])
    return rms_norm(h, final_norm)
```

```python
import functools
import math

import jax
import jax.numpy as jnp
import numpy as np
from jax import lax
from jax.experimental import pallas as pl
from jax.experimental.pallas import tpu as pltpu

F32 = jnp.float32
BF16 = jnp.bfloat16

D_MODEL = 1024
CHUNK = 64
NORM_EPS = 1e-6
ROPE_BASE = 10000.0
PLE_DIM = 256

SSD_D_INNER = 2 * D_MODEL
SSD_HEAD_DIM = 64
SSD_HEADS = SSD_D_INNER // SSD_HEAD_DIM
SSD_GROUPS = 4
SSD_HEADS_PER_GROUP = SSD_HEADS // SSD_GROUPS
SSD_STATE = 128
SSD_CONV = 4
SSD_CONV_DIM = SSD_D_INNER + 2 * SSD_GROUPS * SSD_STATE
SSD_MAIN = SSD_D_INNER + SSD_CONV_DIM
SSD_BLOCK = 128

RET_HEADS = 4
RET_QK_DIM = D_MODEL // RET_HEADS
RET_V_DIM = 2 * RET_QK_DIM
RET_V_WIDTH = RET_HEADS * RET_V_DIM
RET_BLOCK = 256

DIFF_HEADS = 8
DIFF_HEAD_DIM = D_MODEL // DIFF_HEADS // 2
FOX_HEADS = 16
FOX_HEAD_DIM = D_MODEL // FOX_HEADS
ATTN_TILE = 256
HEAD_LANES = 128

REL_BUCKETS = 32
REL_MAX_DIST = 128

PEER_KEYS = 128
PEER_EXPERTS = PEER_KEYS * PEER_KEYS
PEER_HEADS = 8
PEER_TOPK = 16
PEER_QUERY_DIM = 256
PEER_HALF = PEER_QUERY_DIM // 2
PEER_TOKENS = 512
PEER_EXPERT_TILE = 512
PEER_SELECT_TOKENS = 256

LANE = 128
VMEM_LIMIT = 48 * 1024 * 1024

_NT = (((1,), (1,)), ((), ()))


def _params(*sem):
    return pltpu.CompilerParams(dimension_semantics=sem, vmem_limit_bytes=VMEM_LIMIT)


def _rms(x, g):
    return x * lax.rsqrt(jnp.mean(x * x, axis=-1, keepdims=True) + NORM_EPS) * g


def _split3(x):
    hi = x.astype(BF16)
    r1 = x - hi.astype(F32)
    mid = r1.astype(BF16)
    lo = (r1 - mid.astype(F32)).astype(BF16)
    return hi, mid, lo


def _dot_exact_lhs01(m01, x):
    hi, mid, lo = _split3(x)
    d = functools.partial(jnp.dot, preferred_element_type=F32)
    return (d(m01, hi) + d(m01, mid)) + d(m01, lo)


def _norm_matmul_kernel(h_ref, g_ref, w_ref, o_ref, *rest, emit_t):
    if emit_t:
        xt_ref, xn_ref = rest
    else:
        (xn_ref,) = rest

    @pl.when(pl.program_id(1) == 0)
    def _():
        y = _rms(h_ref[...], g_ref[...])
        xn_ref[...] = y.astype(BF16)
        if emit_t:
            xt_ref[...] = y.T.astype(BF16)

    o_ref[...] = jnp.dot(xn_ref[...], w_ref[...], preferred_element_type=F32).astype(o_ref.dtype)


def norm_matmul(h, g, w, out_dtype, emit_t=False, tm=512, tn=512):
    T, D = h.shape
    N = w.shape[1]
    tn = min(tn, N)
    assert T % tm == 0 and N % tn == 0
    out_shape = [jax.ShapeDtypeStruct((T, N), out_dtype)]
    out_specs = [pl.BlockSpec((tm, tn), lambda i, j: (i, j))]
    if emit_t:
        out_shape.append(jax.ShapeDtypeStruct((D, T), BF16))
        out_specs.append(pl.BlockSpec((D, tm), lambda i, j: (0, i)))
    res = pl.pallas_call(
        functools.partial(_norm_matmul_kernel, emit_t=emit_t),
        grid=(T // tm, N // tn),
        in_specs=[pl.BlockSpec((tm, D), lambda i, j: (i, 0)),
                  pl.BlockSpec((1, D), lambda i, j: (0, 0)),
                  pl.BlockSpec((D, tn), lambda i, j: (0, j))],
        out_specs=out_specs,
        out_shape=out_shape,
        scratch_shapes=[pltpu.VMEM((tm, D), BF16)],
        compiler_params=_params("parallel", "arbitrary"),
        name="norm_matmul",
    )(h, g.reshape(1, D), w)
    return res if emit_t else res[0]


def _matmul_residual_kernel(a_ref, w_ref, h_ref, o_ref):
    o_ref[...] = h_ref[...] + jnp.dot(a_ref[...], w_ref[...], preferred_element_type=F32)


def matmul_residual(a, w, h, tm=512):
    T, K = a.shape
    N = w.shape[1]
    return pl.pallas_call(
        _matmul_residual_kernel,
        grid=(T // tm,),
        in_specs=[pl.BlockSpec((tm, K), lambda i: (i, 0)),
                  pl.BlockSpec((K, N), lambda i: (0, 0)),
                  pl.BlockSpec((tm, N), lambda i: (i, 0))],
        out_specs=pl.BlockSpec((tm, N), lambda i: (i, 0)),
        out_shape=jax.ShapeDtypeStruct((T, N), F32),
        input_output_aliases={2: 0},
        compiler_params=_params("parallel"),
        name="matmul_residual",
    )(a, w, h)


def _ple_kernel(h_ref, g_ref, wg_ref, p_ref, wp_ref, fg_ref, o_ref, *, final):
    x = h_ref[...]
    xn = _rms(x, g_ref[...]).astype(BF16)
    gate = jax.nn.sigmoid(jnp.dot(xn, wg_ref[...], preferred_element_type=F32))
    proj = jnp.dot(p_ref[...].astype(BF16), wp_ref[...], preferred_element_type=F32)
    y = x + gate * proj
    if final:
        y = _rms(y, fg_ref[...])
    o_ref[...] = y


def ple_layer(h, g, w_gate, p_all, layer, w_proj, final_g, final, tm=512):
    T, D = h.shape
    return pl.pallas_call(
        functools.partial(_ple_kernel, final=final),
        grid=(T // tm,),
        in_specs=[pl.BlockSpec((tm, D), lambda i: (i, 0)),
                  pl.BlockSpec((1, D), lambda i: (0, 0)),
                  pl.BlockSpec((D, D), lambda i: (0, 0)),
                  pl.BlockSpec((None, tm, PLE_DIM), lambda i: (layer, i, 0)),
                  pl.BlockSpec((PLE_DIM, D), lambda i: (0, 0)),
                  pl.BlockSpec((1, D), lambda i: (0, 0))],
        out_specs=pl.BlockSpec((tm, D), lambda i: (i, 0)),
        out_shape=jax.ShapeDtypeStruct((T, D), F32),
        input_output_aliases={0: 0},
        compiler_params=_params("parallel"),
        name="ple_layer",
    )(h, g.reshape(1, D), w_gate, p_all, w_proj, final_g.reshape(1, D))


def _ssd_kernel(proj_ref, dt_ref, cw_ref, cb_ref, dtb_ref, alog_ref, dsk_ref, ng_ref, o_ref,
                xbuf, state, ybuf, *, L):
    DI, P, N, R = SSD_D_INNER, SSD_HEAD_DIM, SSD_STATE, SSD_HEADS_PER_GROUP
    GW = DI // SSD_GROUPS

    @pl.when(pl.program_id(1) == 0)
    def _():
        xbuf[0:8, :] = jnp.zeros((8, SSD_CONV_DIM), F32)
        state[...] = jnp.zeros(state.shape, F32)

    xbc = proj_ref[0, :, DI:].astype(F32)
    xbuf[8:8 + L, :] = xbc
    conv = cb_ref[...] + cw_ref[3:4, :] * xbc
    for j in range(1, SSD_CONV):
        conv = conv + cw_ref[SSD_CONV - 1 - j:SSD_CONV - j, :] * xbuf[8 - j:8 - j + L, :]
    xbuf[0:8, :] = xbuf[L:L + 8, :]
    act = conv * jax.nn.sigmoid(conv)
    xs = act[:, :DI]
    bm = act[:, DI:DI + SSD_GROUPS * N]
    cm = act[:, DI + SSD_GROUPS * N:]

    dt = jax.nn.softplus(dt_ref[0] + dtb_ref[...])
    a = dt * (-jnp.exp(alog_ref[...]))
    row = lax.broadcasted_iota(jnp.int32, (L, L), 0)
    col = lax.broadcasted_iota(jnp.int32, (L, L), 1)
    tril = row >= col
    cum = _dot_exact_lhs01(tril.astype(BF16), a)
    cum_t = cum.T
    dt_t = dt.T
    last = cum[L - 1:L, :]
    wst = jnp.exp(last - cum) * dt
    ecum = jnp.exp(cum)
    elast = jnp.exp(last)

    for g in range(SSD_GROUPS):
        bg = bm[:, g * N:(g + 1) * N]
        cg = cm[:, g * N:(g + 1) * N].astype(BF16)
        cb = lax.dot_general(cg, bg.astype(BF16), _NT, preferred_element_type=F32)
        bg_t = bg.T.astype(BF16)
        for r in range(R):
            hd = g * R + r
            xh = xs[:, hd * P:(hd + 1) * P]
            seg = cum[:, hd:hd + 1] - cum_t[hd:hd + 1, :]
            w = jnp.exp(jnp.where(tril, seg, -jnp.inf)) * cb * dt_t[hd:hd + 1, :]
            st = state[g, :, r * P:(r + 1) * P]
            y = jnp.dot(w.astype(BF16), xh.astype(BF16), preferred_element_type=F32)
            y = y + jnp.dot(cg, st.astype(BF16), preferred_element_type=F32) * ecum[:, hd:hd + 1]
            xw = (xh * wst[:, hd:hd + 1]).astype(BF16)
            state[g, :, r * P:(r + 1) * P] = (st * elast[:, hd:hd + 1]
                                              + jnp.dot(bg_t, xw, preferred_element_type=F32))
            ybuf[:, hd * P:(hd + 1) * P] = y

    y = ybuf[...] + dsk_ref[...] * xs
    z = proj_ref[0, :, :DI].astype(F32)
    y = y * (z * jax.nn.sigmoid(z))
    for g in range(SSD_GROUPS):
        yg = y[:, g * GW:(g + 1) * GW]
        o_ref[0, :, g * GW:(g + 1) * GW] = _rms(yg, ng_ref[:, g * GW:(g + 1) * GW]).astype(o_ref.dtype)


def ssd_core(proj, dt_raw, conv_w, conv_b, dt_bias, a_log, d_skip, norm_g):
    B, S, _ = proj.shape
    L = SSD_BLOCK
    pad = LANE - SSD_HEADS
    vec = lambda n: pl.BlockSpec((1, n), lambda b, c: (0, 0))
    return pl.pallas_call(
        functools.partial(_ssd_kernel, L=L),
        grid=(B, S // L),
        in_specs=[pl.BlockSpec((1, L, SSD_MAIN), lambda b, c: (b, c, 0)),
                  pl.BlockSpec((1, L, LANE), lambda b, c: (b, c, 0)),
                  pl.BlockSpec((SSD_CONV, SSD_CONV_DIM), lambda b, c: (0, 0)),
                  vec(SSD_CONV_DIM), vec(LANE), vec(LANE), vec(SSD_D_INNER), vec(SSD_D_INNER)],
        out_specs=pl.BlockSpec((1, L, SSD_D_INNER), lambda b, c: (b, c, 0)),
        out_shape=jax.ShapeDtypeStruct((B, S, SSD_D_INNER), BF16),
        scratch_shapes=[pltpu.VMEM((L + 8, SSD_CONV_DIM), F32),
                        pltpu.VMEM((SSD_GROUPS, SSD_STATE, SSD_D_INNER // SSD_GROUPS), F32),
                        pltpu.VMEM((L, SSD_D_INNER), F32)],
        compiler_params=_params("parallel", "arbitrary"),
        name="ssd_core",
    )(proj, dt_raw,
      conv_w.reshape(SSD_CONV, SSD_CONV_DIM), conv_b.reshape(1, SSD_CONV_DIM),
      jnp.pad(dt_bias, (0, pad)).reshape(1, LANE), jnp.pad(a_log, (0, pad)).reshape(1, LANE),
      jnp.repeat(d_skip, SSD_HEAD_DIM).reshape(1, SSD_D_INNER), norm_g.reshape(1, SSD_D_INNER))


def _retention_kernel(q_ref, k_ref, v_ref, gate_ref, cos_ref, sin_ref, dec_ref, qd_ref, kd_ref,
                      cd_ref, ng_ref, o_ref, state):
    half = RET_QK_DIM // 2

    @pl.when(pl.program_id(2) == 0)
    def _():
        state[...] = jnp.zeros(state.shape, F32)

    cos = cos_ref[...]
    sin = sin_ref[...]

    def rot(x):
        x1, x2 = x[:, :half], x[:, half:]
        return jnp.concatenate([x1 * cos - x2 * sin, x1 * sin + x2 * cos], axis=-1)

    q = rot(q_ref[0].astype(F32))
    k = rot(k_ref[0].astype(F32)) * (RET_QK_DIM ** -0.5)
    v = v_ref[0]
    qb = q.astype(BF16)
    s = lax.dot_general(qb, k.astype(BF16), _NT, preferred_element_type=F32) * dec_ref[0]
    o = jnp.dot(s.astype(BF16), v, preferred_element_type=F32)
    st = state[...]
    o = o + jnp.dot(qb, st.astype(BF16), preferred_element_type=F32) * qd_ref[0]
    kt = (k * kd_ref[0]).T.astype(BF16)
    state[...] = st * cd_ref[0] + jnp.dot(kt, v, preferred_element_type=F32)
    gate = gate_ref[0].astype(F32)
    o_ref[0] = (_rms(o, ng_ref[0]) * (gate * jax.nn.sigmoid(gate))).astype(o_ref.dtype)


def retention_core(proj, norm_g):
    B, S, _ = proj.shape
    H, dk, dv, L = RET_HEADS, RET_QK_DIM, RET_V_DIM, RET_BLOCK
    inv = 1.0 / (ROPE_BASE ** (jnp.arange(0, dk, 2, dtype=F32) / dk))
    ang = jnp.arange(S, dtype=F32)[:, None] * inv[None, :]
    log_gamma = jnp.log1p(-jnp.exp2(-5.0 - jnp.arange(H, dtype=F32)))
    idx = jnp.arange(L, dtype=F32)
    visible = (jnp.arange(L)[None, :] // CHUNK) <= (jnp.arange(L)[:, None] // CHUNK)
    decay = jnp.where(visible[None],
                      jnp.exp(log_gamma[:, None, None] * jnp.abs(idx[:, None] - idx[None, :])), 0.0)
    q_decay = jnp.exp(log_gamma[:, None] * (idx[None, :] + 1.0))[..., None]
    k_decay = jnp.exp(log_gamma[:, None] * (L - 1.0 - idx[None, :]))[..., None]
    block_decay = jnp.exp(log_gamma * L).reshape(H, 1, 1)
    return pl.pallas_call(
        _retention_kernel,
        grid=(B, H, S // L),
        in_specs=[pl.BlockSpec((1, L, dk), lambda b, h, c: (b, c, h)),
                  pl.BlockSpec((1, L, dk), lambda b, h, c: (b, c, H + h)),
                  pl.BlockSpec((1, L, dv), lambda b, h, c: (b, c, H + h)),
                  pl.BlockSpec((1, L, dv), lambda b, h, c: (b, c, 2 * H + h)),
                  pl.BlockSpec((L, dk // 2), lambda b, h, c: (c, 0)),
                  pl.BlockSpec((L, dk // 2), lambda b, h, c: (c, 0)),
                  pl.BlockSpec((1, L, L), lambda b, h, c: (h, 0, 0)),
                  pl.BlockSpec((1, L, 1), lambda b, h, c: (h, 0, 0)),
                  pl.BlockSpec((1, L, 1), lambda b, h, c: (h, 0, 0)),
                  pl.BlockSpec((1, 1, 1), lambda b, h, c: (h, 0, 0)),
                  pl.BlockSpec((1, 1, dv), lambda b, h, c: (h, 0, 0))],
        out_specs=pl.BlockSpec((1, L, dv), lambda b, h, c: (b, c, h)),
        out_shape=jax.ShapeDtypeStruct((B, S, RET_V_WIDTH), BF16),
        scratch_shapes=[pltpu.VMEM((dk, dv), F32)],
        compiler_params=_params("parallel", "parallel", "arbitrary"),
        name="retention_core",
    )(proj, proj, proj, proj, jnp.cos(ang), jnp.sin(ang), decay, q_decay, k_decay, block_decay,
      norm_g.reshape(H, 1, dv))


def _attn_kernel(*refs, mode, T):
    if mode == "diff":
        q_ref, k_ref, v_ref, bias_ref, lam_ref, ng_ref, o_ref, m_s, l_s, acc_s = refs
    else:
        q_ref, k_ref, v_ref, ck_ref, cq_ref, o_ref, m_s, l_s, acc_s = refs
    i = pl.program_id(2)
    half = HEAD_LANES // 2
    lane = lax.broadcasted_iota(jnp.int32, (1, HEAD_LANES), 1)
    first = lane < half
    q = q_ref[0] * jnp.asarray(half ** -0.5, BF16)
    zero = jnp.zeros_like(q)
    qs = (jnp.where(first, q, zero), jnp.where(first, zero, q))

    m_s[...] = jnp.full(m_s.shape, -jnp.inf, F32)
    l_s[...] = jnp.zeros(l_s.shape, F32)
    acc_s[...] = jnp.zeros(acc_s.shape, F32)

    def step(j, diagonal):
        start = pl.multiple_of(j * T, T)
        k = k_ref[0, pl.ds(start, T), :]
        v = v_ref[0, pl.ds(start, T), :]
        if diagonal:
            row = lax.broadcasted_iota(jnp.int32, (T, T), 0)
            col = lax.broadcasted_iota(jnp.int32, (T, T), 1)
            if mode == "diff":
                visible = (col // CHUNK) <= (row // CHUNK)
            else:
                visible = col <= row
        for a in range(2):
            s = lax.dot_general(qs[a], k, _NT, preferred_element_type=F32)
            if mode == "diff":
                s = s + bias_ref[0, jnp.minimum(i - j, 2)]
            else:
                s = s + (cq_ref[0, 0, :, a:a + 1] - ck_ref[0, 0, j, a:a + 1, :])
            if diagonal:
                s = jnp.where(visible, s, -jnp.inf)
            m_prev = m_s[a]
            m_new = jnp.maximum(m_prev, jnp.max(s, axis=-1, keepdims=True))
            alpha = jnp.exp(m_prev - m_new)
            p = jnp.exp(s - m_new)
            l_s[a] = alpha * l_s[a] + jnp.sum(p, axis=-1, keepdims=True)
            acc_s[a] = alpha * acc_s[a] + jnp.dot(p.astype(BF16), v, preferred_element_type=F32)
            m_s[a] = m_new

    lax.fori_loop(0, i, lambda j, c: (step(j, False), c)[1], 0)
    step(i, True)

    o0 = acc_s[0] / l_s[0]
    o1 = acc_s[1] / l_s[1]
    if mode == "diff":
        o = o0 - lam_ref[0] * o1
        o = _rms(o, ng_ref[...])
    else:
        o = jnp.where(first, o0, o1)
    o_ref[0] = o.astype(o_ref.dtype)


def _attn_call(mode, proj, n_blocks, extra_inputs, extra_specs):
    B, S, _ = proj.shape
    T = ATTN_TILE
    return pl.pallas_call(
        functools.partial(_attn_kernel, mode=mode, T=T),
        grid=(B, n_blocks, S // T),
        in_specs=[pl.BlockSpec((1, T, HEAD_LANES), lambda b, h, i: (b, i, h)),
                  pl.BlockSpec((1, S, HEAD_LANES), lambda b, h, i: (b, 0, n_blocks + h)),
                  pl.BlockSpec((1, S, HEAD_LANES), lambda b, h, i: (b, 0, 2 * n_blocks + h))] + extra_specs,
        out_specs=pl.BlockSpec((1, T, HEAD_LANES), lambda b, h, i: (b, i, h)),
        out_shape=jax.ShapeDtypeStruct((B, S, D_MODEL), BF16),
        scratch_shapes=[pltpu.VMEM((2, T, 1), F32), pltpu.VMEM((2, T, 1), F32),
                        pltpu.VMEM((2, T, HEAD_LANES), F32)],
        compiler_params=_params("parallel", "parallel", "arbitrary"),
        name=mode + "_attention",
    )(proj, proj, proj, *extra_inputs)


def _t5_bucket(rel):
    nb = REL_BUCKETS // 2
    max_exact = nb // 2
    ret = (rel > 0).astype(jnp.int32) * nb
    n = jnp.abs(rel)
    nf = jnp.maximum(n, 1).astype(F32)
    large = max_exact + (jnp.log(nf / max_exact) / math.log(REL_MAX_DIST / max_exact)
                         * (nb - max_exact)).astype(jnp.int32)
    large = jnp.minimum(large, nb - 1)
    return ret + jnp.where(n < max_exact, n, large)


def diff_core(proj, lam_vecs, norm_g, rel_bias, lam_init):
    T = ATTN_TILE
    assert T >= REL_MAX_DIST
    lv = lam_vecs.astype(F32)
    lam = jnp.exp(jnp.sum(lv[0] * lv[1])) - jnp.exp(jnp.sum(lv[2] * lv[3])) + lam_init
    off = jnp.arange(T)
    rel = (off[None, None, :] - off[None, :, None]) - (jnp.arange(3) * T)[:, None, None]
    bias = jnp.transpose(rel_bias[_t5_bucket(rel)].astype(F32), (3, 0, 1, 2))
    g = (norm_g * (1.0 - lam_init)).reshape(1, HEAD_LANES)
    extra_specs = [pl.BlockSpec((1, 3, T, T), lambda b, h, i: (h, 0, 0, 0)),
                   pl.BlockSpec(memory_space=pltpu.SMEM),
                   pl.BlockSpec((1, HEAD_LANES), lambda b, h, i: (0, 0))]
    return _attn_call("diff", proj, DIFF_HEADS, [bias, lam.reshape(1), g], extra_specs)


def _fox_gate_kernel(h_ref, g_ref, w_ref, b_ref, ct_ref, c_ref, carry, *, L):
    @pl.when(pl.program_id(1) == 0)
    def _():
        carry[...] = jnp.zeros(carry.shape, F32)

    xn = _rms(h_ref[0], g_ref[...])
    logits = lax.dot_general(w_ref[...], xn.astype(BF16), _NT, preferred_element_type=F32)
    log_f = jax.nn.log_sigmoid(logits + b_ref[...])
    row = lax.broadcasted_iota(jnp.int32, (L, L), 0)
    col = lax.broadcasted_iota(jnp.int32, (L, L), 1)
    triu = (row <= col).astype(BF16)
    hi, mid, lo = _split3(log_f)
    d = functools.partial(jnp.dot, preferred_element_type=F32)
    c = carry[...] + ((d(hi, triu) + d(mid, triu)) + d(lo, triu))
    carry[...] = c[:, L - 1:L]
    ct_ref[0] = c
    c_ref[0] = c.T


def fox_gate(h3, g, w_f, b_f):
    B, S, D = h3.shape
    L = 512
    pad = LANE - FOX_HEADS
    return pl.pallas_call(
        functools.partial(_fox_gate_kernel, L=L),
        grid=(B, S // L),
        in_specs=[pl.BlockSpec((1, L, D), lambda b, c: (b, c, 0)),
                  pl.BlockSpec((1, D), lambda b, c: (0, 0)),
                  pl.BlockSpec((LANE, D), lambda b, c: (0, 0)),
                  pl.BlockSpec((LANE, 1), lambda b, c: (0, 0))],
        out_specs=[pl.BlockSpec((1, LANE, L), lambda b, c: (b, 0, c)),
                   pl.BlockSpec((1, L, LANE), lambda b, c: (b, c, 0))],
        out_shape=[jax.ShapeDtypeStruct((B, LANE, S), F32), jax.ShapeDtypeStruct((B, S, LANE), F32)],
        scratch_shapes=[pltpu.VMEM((LANE, 1), F32)],
        compiler_params=_params("parallel", "arbitrary"),
        name="fox_gate",
    )(h3, g.reshape(1, D), jnp.pad(w_f.T, ((0, pad), (0, 0))).astype(BF16),
      jnp.pad(b_f, (0, pad)).reshape(LANE, 1))


def fox_core(proj, c_t, c_rows):
    B, S, _ = proj.shape
    T = ATTN_TILE
    nb = FOX_HEADS // 2
    ck = jnp.transpose(c_t[:, :FOX_HEADS].reshape(B, nb, 2, S // T, T), (0, 1, 3, 2, 4))
    cq = jnp.transpose(c_rows[:, :, :FOX_HEADS].reshape(B, S, nb, 2), (0, 2, 1, 3))
    extra_specs = [pl.BlockSpec((1, 1, S // T, 2, T), lambda b, h, i: (b, h, 0, 0, 0)),
                   pl.BlockSpec((1, 1, T, 2), lambda b, h, i: (b, h, i, 0))]
    return _attn_call("fox", proj, nb, [ck, cq], extra_specs)


def _candidate_tables():
    pairs = [(a, b) for a in range(PEER_TOPK) for b in range(PEER_TOPK) if (a + 1) * (b + 1) <= PEER_TOPK]
    rows = 64
    p1 = np.zeros((rows, LANE), np.float32)
    p2 = np.zeros((rows, LANE), np.float32)
    for r, (a, b) in enumerate(pairs):
        p1[r, a] = 1.0
        p2[r, b] = 1.0
    return len(pairs), p1, p2


N_CAND, _P1, _P2 = _candidate_tables()


def _top16_ranks(s):
    n, t = s.shape
    idx = lax.broadcasted_iota(jnp.int32, (n, t), 0)
    ridx = lax.broadcasted_iota(jnp.int32, (PEER_TOPK, t), 0)
    rank = jnp.full((n, t), PEER_TOPK, jnp.int32)
    vals = jnp.zeros((PEER_TOPK, t), F32)
    work = s
    for r in range(PEER_TOPK):
        mx = jnp.max(work, axis=0, keepdims=True)
        pos = jnp.min(jnp.where(work == mx, idx, n), axis=0, keepdims=True)
        hit = idx == pos
        rank = jnp.where(hit, r, rank)
        work = jnp.where(hit, -jnp.inf, work)
        vals = jnp.where(ridx == r, mx, vals)
    return rank, vals


def _peer_select_kernel(q_ref, keys_ref, p1_ref, p2_ref, p1t_ref, rank2_ref, e2_ref, nrow_ref, e1_ref):
    def scores(c):
        kk = keys_ref[0, c]
        qq = q_ref[:, c * PEER_HALF:(c + 1) * PEER_HALF]
        kh, km, _ = _split3(kk)
        qh, qm, _ = _split3(qq)
        d = lambda a, b: lax.dot_general(a, b, _NT, preferred_element_type=F32)
        return d(kh, qh) + (d(kh, qm) + d(km, qh))

    s1 = scores(0)
    s2 = scores(1)
    rank1, v1 = _top16_ranks(s1)
    rank2, v2 = _top16_ranks(s2)

    tokens = s1.shape[1]
    pad = jnp.zeros((LANE - PEER_TOPK, tokens), F32)
    cand = (_dot_exact_lhs01(p1_ref[...], jnp.concatenate([v1, pad], axis=0))
            + _dot_exact_lhs01(p2_ref[...], jnp.concatenate([v2, pad], axis=0)))
    cidx = lax.broadcasted_iota(jnp.int32, cand.shape, 0)
    cand = jnp.where(cidx < N_CAND, cand, -jnp.inf)
    work = cand
    sel = jnp.zeros(cand.shape, F32)
    for _ in range(PEER_TOPK):
        mx = jnp.max(work, axis=0, keepdims=True)
        pos = jnp.min(jnp.where(work == mx, cidx, cand.shape[0]), axis=0, keepdims=True)
        hit = cidx == pos
        sel = jnp.where(hit, 1.0, sel)
        work = jnp.where(hit, -jnp.inf, work)
    top = v1[0:1, :] + v2[0:1, :]
    z = jnp.sum(sel * jnp.exp(jnp.where(sel > 0.0, cand - top, 0.0)), axis=0, keepdims=True)
    sel_pad = jnp.concatenate([sel, jnp.zeros((LANE - sel.shape[0], tokens), F32)], axis=0).astype(BF16)
    n_by_rank = jnp.dot(p1t_ref[...], sel_pad, preferred_element_type=F32)
    nrow = jnp.zeros(s1.shape, F32)
    for a in range(PEER_TOPK):
        nrow = jnp.where(rank1 == a, n_by_rank[a:a + 1, :], nrow)

    rank2_ref[0] = rank2.astype(F32).astype(BF16)
    e2_ref[0] = jnp.exp(s2 - v2[0:1, :]).astype(BF16)
    nrow_ref[0] = nrow
    e1_ref[0] = jnp.exp(s1 - v1[0:1, :]) / z


def peer_select(q, keys):
    T = q.shape[0]
    tm = PEER_SELECT_TOKENS
    hk = pl.BlockSpec((1, PEER_KEYS, tm), lambda i, h: (h, 0, i))
    shp = lambda dt: jax.ShapeDtypeStruct((PEER_HEADS, PEER_KEYS, T), dt)
    cst = lambda a: pl.BlockSpec(a.shape, lambda i, h: (0, 0))
    p1, p2 = jnp.asarray(_P1, BF16), jnp.asarray(_P2, BF16)
    p1t = jnp.asarray(np.pad(_P1.T, ((0, 0), (0, LANE - _P1.shape[0]))), BF16)
    return pl.pallas_call(
        _peer_select_kernel,
        grid=(T // tm, PEER_HEADS),
        in_specs=[pl.BlockSpec((tm, PEER_QUERY_DIM), lambda i, h: (i, h)),
                  pl.BlockSpec((1, 2, PEER_KEYS, PEER_HALF), lambda i, h: (h, 0, 0, 0)),
                  cst(p1), cst(p2), cst(p1t)],
        out_specs=[hk, hk, hk, hk],
        out_shape=[shp(BF16), shp(BF16), shp(F32), shp(F32)],
        compiler_params=_params("parallel", "parallel"),
        name="peer_select",
    )(q, keys, p1, p2, p1t)


def _peer_dense_kernel(xt_ref, u_ref, vt_ref, rank2_ref, e2_ref, nrow_ref, e1_ref, h_ref, o_ref,
                       acc, gbuf):
    j = pl.program_id(1)
    rows = PEER_EXPERT_TILE // PEER_KEYS

    @pl.when(j == 0)
    def _():
        acc[...] = jnp.zeros(acc.shape, F32)

    hid = jnp.dot(u_ref[...], xt_ref[...], preferred_element_type=F32)
    act = 0.5 * hid * (1.0 + lax.erf(hid * (2.0 ** -0.5)))
    for ii in range(rows):
        key1 = j * rows + ii
        w = jnp.zeros((PEER_KEYS, xt_ref.shape[1]), BF16)
        for h in range(PEER_HEADS):
            n = nrow_ref[h, pl.ds(key1, 1), :].astype(BF16)
            e1 = e1_ref[h, pl.ds(key1, 1), :].astype(BF16)
            w = w + jnp.where(rank2_ref[h] < n, e2_ref[h] * e1, jnp.zeros_like(w))
        gbuf[ii * PEER_KEYS:(ii + 1) * PEER_KEYS, :] = (
            w.astype(F32) * act[ii * PEER_KEYS:(ii + 1) * PEER_KEYS, :]).astype(BF16)
    acc[...] += jnp.dot(vt_ref[...], gbuf[...], preferred_element_type=F32)

    @pl.when(j == pl.num_programs(1) - 1)
    def _():
        o_ref[...] = h_ref[...] + acc[...].T


def peer_dense(xt, u, vt, rank2, e2, nrow, e1, h):
    D, T = xt.shape
    tm, te = PEER_TOKENS, PEER_EXPERT_TILE
    sel = pl.BlockSpec((PEER_HEADS, PEER_KEYS, tm), lambda i, j: (0, 0, i))
    return pl.pallas_call(
        _peer_dense_kernel,
        grid=(T // tm, PEER_EXPERTS // te),
        in_specs=[pl.BlockSpec((D, tm), lambda i, j: (0, i)),
                  pl.BlockSpec((te, D), lambda i, j: (j, 0)),
                  pl.BlockSpec((D, te), lambda i, j: (0, j)),
                  sel, sel, sel, sel,
                  pl.BlockSpec((tm, D), lambda i, j: (i, 0))],
        out_specs=pl.BlockSpec((tm, D), lambda i, j: (i, 0)),
        out_shape=jax.ShapeDtypeStruct((T, D), F32),
        scratch_shapes=[pltpu.VMEM((D, tm), F32), pltpu.VMEM((te, tm), BF16)],
        input_output_aliases={7: 0},
        compiler_params=_params("parallel", "arbitrary"),
        name="peer_dense",
    )(xt, u, vt, rank2, e2, nrow, e1, h)


def peer_layer(h, g, w_q, keys, u, v):
    q, xt = norm_matmul(h, g, w_q.astype(BF16), F32, emit_t=True)
    rank2, e2, nrow, e1 = peer_select(q, keys)
    return peer_dense(xt, u.astype(BF16), v.T.astype(BF16), rank2, e2, nrow, e1, h)


def _pad_cols(w, n):
    return jnp.pad(w, ((0, 0), (0, n - w.shape[1])))


def kernel(x, p, mix_norm, ffn_norm, ple_norm, final_norm, ssd_w_in, ssd_conv_w, ssd_conv_b, ssd_dt_bias, ssd_a_log, ssd_d, ssd_norm, ssd_w_out, ret_w_in, ret_norm, ret_w_out, diff_w_in, diff_lambda, diff_norm, diff_w_out, fox_w_in, fox_b_f, fox_w_out, rel_bias, peer_w_q, peer_keys, peer_u, peer_v, ple_proj, ple_gate):
    B, S, D = x.shape
    T = B * S
    depth = mix_norm.shape[0]
    n_mixers = 4
    h = x.reshape(T, D)
    p_all = p.reshape(depth, T, PLE_DIM)
    for i in range(depth):
        m, j = i % n_mixers, i // n_mixers
        g = mix_norm[i]
        if m == 0:
            w = ssd_w_in[j]
            proj = norm_matmul(h, g, w[:, :SSD_MAIN].astype(BF16), BF16)
            dt_raw = norm_matmul(h, g, _pad_cols(w[:, SSD_MAIN:], LANE).astype(BF16), F32)
            y = ssd_core(proj.reshape(B, S, SSD_MAIN), dt_raw.reshape(B, S, LANE), ssd_conv_w[j],
                         ssd_conv_b[j], ssd_dt_bias[j], ssd_a_log[j], ssd_d[j], ssd_norm[j])
            w_out = ssd_w_out[j]
        elif m == 1:
            proj = norm_matmul(h, g, ret_w_in[j].astype(BF16), BF16)
            y = retention_core(proj.reshape(B, S, -1), ret_norm[j])
            w_out = ret_w_out[j]
        elif m == 2:
            lam_init = 0.8 - 0.6 * math.exp(-0.3 * i)
            proj = norm_matmul(h, g, diff_w_in[j].astype(BF16), BF16)
            y = diff_core(proj.reshape(B, S, -1), diff_lambda[j], diff_norm[j], rel_bias, lam_init)
            w_out = diff_w_out[j]
        else:
            w = fox_w_in[j]
            proj = norm_matmul(h, g, w[:, :3 * D].astype(BF16), BF16)
            c_t, c_rows = fox_gate(h.reshape(B, S, D), g, w[:, 3 * D:], fox_b_f[j])
            y = fox_core(proj.reshape(B, S, -1), c_t, c_rows)
            w_out = fox_w_out[j]
        h = matmul_residual(y.reshape(T, -1), w_out.astype(BF16), h)
        h = peer_layer(h, ffn_norm[i], peer_w_q[i], peer_keys[i], peer_u[i], peer_v[i])
        h = ple_layer(h, ple_norm[i], ple_gate[i].astype(BF16), p_all, i, ple_proj[i].astype(BF16),
                      final_norm, final=(i == depth - 1))
    return h.reshape(B, S, D)
```

```python
import functools
import math

import jax
import jax.numpy as jnp
import numpy as np
from jax import lax
from jax.experimental import pallas as pl
from jax.experimental.pallas import tpu as pltpu

F32 = jnp.float32
BF16 = jnp.bfloat16

D_MODEL = 1024
CHUNK = 64
NORM_EPS = 1e-6
ROPE_BASE = 10000.0
PLE_DIM = 256

SSD_D_INNER = 2 * D_MODEL
SSD_HEAD_DIM = 64
SSD_HEADS = SSD_D_INNER // SSD_HEAD_DIM
SSD_GROUPS = 4
SSD_HEADS_PER_GROUP = SSD_HEADS // SSD_GROUPS
SSD_STATE = 128
SSD_CONV = 4
SSD_CONV_DIM = SSD_D_INNER + 2 * SSD_GROUPS * SSD_STATE
SSD_MAIN = SSD_D_INNER + SSD_CONV_DIM
SSD_BLOCK = 128

RET_HEADS = 4
RET_QK_DIM = D_MODEL // RET_HEADS
RET_V_DIM = 2 * RET_QK_DIM
RET_V_WIDTH = RET_HEADS * RET_V_DIM
RET_BLOCK = 256

DIFF_HEADS = 8
DIFF_HEAD_DIM = D_MODEL // DIFF_HEADS // 2
FOX_HEADS = 16
FOX_HEAD_DIM = D_MODEL // FOX_HEADS
ATTN_TILE = 512
HEAD_LANES = 128

REL_BUCKETS = 32
REL_MAX_DIST = 128

PEER_KEYS = 128
PEER_EXPERTS = PEER_KEYS * PEER_KEYS
PEER_HEADS = 8
PEER_TOPK = 16
PEER_QUERY_DIM = 256
PEER_HALF = PEER_QUERY_DIM // 2
PEER_TOKENS = 1024
PEER_TOKEN_CHUNK = 256
PEER_EXPERT_TILE = 1024
PEER_SELECT_TOKENS = 256
PEER_DENSE_VMEM = 56 * 1024 * 1024

LANE = 128
SUBLANE = 8
VMEM_LIMIT = 48 * 1024 * 1024

_NT = (((1,), (1,)), ((), ()))


def _params(*sem):
    return pltpu.CompilerParams(dimension_semantics=sem, vmem_limit_bytes=VMEM_LIMIT)


def _rms(x, g):
    return x * lax.rsqrt(jnp.mean(x * x, axis=-1, keepdims=True) + NORM_EPS) * g


def _split3(x):
    hi = x.astype(BF16)
    r1 = x - hi.astype(F32)
    mid = r1.astype(BF16)
    lo = (r1 - mid.astype(F32)).astype(BF16)
    return hi, mid, lo


def _dot_exact_lhs01(m01, x):
    hi, mid, lo = _split3(x)
    d = functools.partial(jnp.dot, preferred_element_type=F32)
    return (d(m01, hi) + d(m01, mid)) + d(m01, lo)


def _norm_matmul_kernel(h_ref, g_ref, w_ref, o_ref, *rest, emit_t):
    if emit_t:
        xt_ref, xn_ref = rest
    else:
        (xn_ref,) = rest

    @pl.when(pl.program_id(1) == 0)
    def _():
        y = _rms(h_ref[...], g_ref[...])
        xn_ref[...] = y.astype(BF16)
        if emit_t:
            tc = xt_ref.shape[2]
            for c in range(xt_ref.shape[0]):
                xt_ref[c] = y[c * tc:(c + 1) * tc, :].T.astype(BF16)

    o_ref[...] = jnp.dot(xn_ref[...], w_ref[...], preferred_element_type=F32).astype(o_ref.dtype)


def norm_matmul(h, g, w, out_dtype, emit_t=False, tm=512, tn=512):
    T, D = h.shape
    N = w.shape[1]
    tn = min(tn, N)
    assert T % tm == 0 and N % tn == 0
    out_shape = [jax.ShapeDtypeStruct((T, N), out_dtype)]
    out_specs = [pl.BlockSpec((tm, tn), lambda i, j: (i, j))]
    if emit_t:
        tc = PEER_TOKEN_CHUNK
        out_shape.append(jax.ShapeDtypeStruct((T // tc, D, tc), BF16))
        out_specs.append(pl.BlockSpec((tm // tc, D, tc), lambda i, j: (i, 0, 0)))
    res = pl.pallas_call(
        functools.partial(_norm_matmul_kernel, emit_t=emit_t),
        grid=(T // tm, N // tn),
        in_specs=[pl.BlockSpec((tm, D), lambda i, j: (i, 0)),
                  pl.BlockSpec((1, D), lambda i, j: (0, 0)),
                  pl.BlockSpec((D, tn), lambda i, j: (0, j))],
        out_specs=out_specs,
        out_shape=out_shape,
        scratch_shapes=[pltpu.VMEM((tm, D), BF16)],
        compiler_params=_params("parallel", "arbitrary"),
        name="norm_matmul",
    )(h, g.reshape(1, D), w)
    return res if emit_t else res[0]


def _matmul_residual_kernel(a_ref, w_ref, h_ref, o_ref):
    o_ref[...] = h_ref[...] + jnp.dot(a_ref[...], w_ref[...], preferred_element_type=F32)


def matmul_residual(a, w, h, tm=512):
    T, K = a.shape
    N = w.shape[1]
    return pl.pallas_call(
        _matmul_residual_kernel,
        grid=(T // tm,),
        in_specs=[pl.BlockSpec((tm, K), lambda i: (i, 0)),
                  pl.BlockSpec((K, N), lambda i: (0, 0)),
                  pl.BlockSpec((tm, N), lambda i: (i, 0))],
        out_specs=pl.BlockSpec((tm, N), lambda i: (i, 0)),
        out_shape=jax.ShapeDtypeStruct((T, N), F32),
        input_output_aliases={2: 0},
        compiler_params=_params("parallel"),
        name="matmul_residual",
    )(a, w, h)


def _ple_kernel(h_ref, y_ref, g_ref, wg_ref, p_ref, wp_ref, fg_ref, o_ref, *, final):
    x = h_ref[...] + y_ref[...]
    xn = _rms(x, g_ref[...]).astype(BF16)
    gate = jax.nn.sigmoid(jnp.dot(xn, wg_ref[...], preferred_element_type=F32))
    proj = jnp.dot(p_ref[...].astype(BF16), wp_ref[...], preferred_element_type=F32)
    y = x + gate * proj
    if final:
        y = _rms(y, fg_ref[...])
    o_ref[...] = y


def ple_layer(h, y, g, w_gate, p_all, layer, w_proj, final_g, final, tm=512):
    T, D = h.shape
    return pl.pallas_call(
        functools.partial(_ple_kernel, final=final),
        grid=(T // tm,),
        in_specs=[pl.BlockSpec((tm, D), lambda i: (i, 0)),
                  pl.BlockSpec((tm, D), lambda i: (i, 0)),
                  pl.BlockSpec((1, D), lambda i: (0, 0)),
                  pl.BlockSpec((D, D), lambda i: (0, 0)),
                  pl.BlockSpec((None, tm, PLE_DIM), lambda i: (layer, i, 0)),
                  pl.BlockSpec((PLE_DIM, D), lambda i: (0, 0)),
                  pl.BlockSpec((1, D), lambda i: (0, 0))],
        out_specs=pl.BlockSpec((tm, D), lambda i: (i, 0)),
        out_shape=jax.ShapeDtypeStruct((T, D), F32),
        input_output_aliases={0: 0},
        compiler_params=_params("parallel"),
        name="ple_layer",
    )(h, y, g.reshape(1, D), w_gate, p_all, w_proj, final_g.reshape(1, D))


def _ssd_kernel(proj_ref, dt_ref, cw_ref, cb_ref, dtb_ref, alog_ref, dsk_ref, ng_ref, o_ref,
                xbuf, state, ybuf, *, L):
    DI, P, N, R = SSD_D_INNER, SSD_HEAD_DIM, SSD_STATE, SSD_HEADS_PER_GROUP
    GW = DI // SSD_GROUPS

    @pl.when(pl.program_id(1) == 0)
    def _():
        xbuf[0:8, :] = jnp.zeros((8, SSD_CONV_DIM), F32)
        state[...] = jnp.zeros(state.shape, F32)

    xbc = proj_ref[0, :, DI:].astype(F32)
    xbuf[8:8 + L, :] = xbc
    conv = cb_ref[...] + cw_ref[3:4, :] * xbc
    for j in range(1, SSD_CONV):
        conv = conv + cw_ref[SSD_CONV - 1 - j:SSD_CONV - j, :] * xbuf[8 - j:8 - j + L, :]
    xbuf[0:8, :] = xbuf[L:L + 8, :]
    act = conv * jax.nn.sigmoid(conv)
    xs = act[:, :DI]
    bm = act[:, DI:DI + SSD_GROUPS * N]
    cm = act[:, DI + SSD_GROUPS * N:]

    dt = jax.nn.softplus(dt_ref[0] + dtb_ref[...])
    a = dt * (-jnp.exp(alog_ref[...]))
    row = lax.broadcasted_iota(jnp.int32, (L, L), 0)
    col = lax.broadcasted_iota(jnp.int32, (L, L), 1)
    tril = row >= col
    cum = _dot_exact_lhs01(tril.astype(BF16), a)
    cum_t = cum.T
    dt_t = dt.T
    last = cum[L - 1:L, :]
    wst = jnp.exp(last - cum) * dt
    ecum = jnp.exp(cum)
    elast = jnp.exp(last)

    for g in range(SSD_GROUPS):
        bg = bm[:, g * N:(g + 1) * N]
        cg = cm[:, g * N:(g + 1) * N].astype(BF16)
        cb = lax.dot_general(cg, bg.astype(BF16), _NT, preferred_element_type=F32)
        bg_t = bg.T.astype(BF16)
        for r in range(R):
            hd = g * R + r
            xh = xs[:, hd * P:(hd + 1) * P]
            seg = cum[:, hd:hd + 1] - cum_t[hd:hd + 1, :]
            w = jnp.exp(jnp.where(tril, seg, -jnp.inf)) * cb * dt_t[hd:hd + 1, :]
            st = state[g, :, r * P:(r + 1) * P]
            y = jnp.dot(w.astype(BF16), xh.astype(BF16), preferred_element_type=F32)
            y = y + jnp.dot(cg, st.astype(BF16), preferred_element_type=F32) * ecum[:, hd:hd + 1]
            xw = (xh * wst[:, hd:hd + 1]).astype(BF16)
            state[g, :, r * P:(r + 1) * P] = (st * elast[:, hd:hd + 1]
                                              + jnp.dot(bg_t, xw, preferred_element_type=F32))
            ybuf[:, hd * P:(hd + 1) * P] = y

    y = ybuf[...] + dsk_ref[...] * xs
    z = proj_ref[0, :, :DI].astype(F32)
    y = y * (z * jax.nn.sigmoid(z))
    for g in range(SSD_GROUPS):
        yg = y[:, g * GW:(g + 1) * GW]
        o_ref[0, :, g * GW:(g + 1) * GW] = _rms(yg, ng_ref[:, g * GW:(g + 1) * GW]).astype(o_ref.dtype)


def ssd_core(proj, dt_raw, conv_w, conv_b, dt_bias, a_log, d_skip, norm_g):
    B, S, _ = proj.shape
    L = SSD_BLOCK
    pad = LANE - SSD_HEADS
    vec = lambda n: pl.BlockSpec((1, n), lambda b, c: (0, 0))
    return pl.pallas_call(
        functools.partial(_ssd_kernel, L=L),
        grid=(B, S // L),
        in_specs=[pl.BlockSpec((1, L, SSD_MAIN), lambda b, c: (b, c, 0)),
                  pl.BlockSpec((1, L, LANE), lambda b, c: (b, c, 0)),
                  pl.BlockSpec((SSD_CONV, SSD_CONV_DIM), lambda b, c: (0, 0)),
                  vec(SSD_CONV_DIM), vec(LANE), vec(LANE), vec(SSD_D_INNER), vec(SSD_D_INNER)],
        out_specs=pl.BlockSpec((1, L, SSD_D_INNER), lambda b, c: (b, c, 0)),
        out_shape=jax.ShapeDtypeStruct((B, S, SSD_D_INNER), BF16),
        scratch_shapes=[pltpu.VMEM((L + 8, SSD_CONV_DIM), F32),
                        pltpu.VMEM((SSD_GROUPS, SSD_STATE, SSD_D_INNER // SSD_GROUPS), F32),
                        pltpu.VMEM((L, SSD_D_INNER), F32)],
        compiler_params=_params("parallel", "arbitrary"),
        name="ssd_core",
    )(proj, dt_raw,
      conv_w.reshape(SSD_CONV, SSD_CONV_DIM), conv_b.reshape(1, SSD_CONV_DIM),
      jnp.pad(dt_bias, (0, pad)).reshape(1, LANE), jnp.pad(a_log, (0, pad)).reshape(1, LANE),
      jnp.repeat(d_skip, SSD_HEAD_DIM).reshape(1, SSD_D_INNER), norm_g.reshape(1, SSD_D_INNER))


def _retention_kernel(q_ref, k_ref, v_ref, gate_ref, cos_ref, sin_ref, dec_ref, qd_ref, kd_ref,
                      cd_ref, ng_ref, o_ref, state):
    half = RET_QK_DIM // 2

    @pl.when(pl.program_id(2) == 0)
    def _():
        state[...] = jnp.zeros(state.shape, F32)

    cos = cos_ref[...]
    sin = sin_ref[...]

    def rot(x):
        x1, x2 = x[:, :half], x[:, half:]
        return jnp.concatenate([x1 * cos - x2 * sin, x1 * sin + x2 * cos], axis=-1)

    q = rot(q_ref[0].astype(F32))
    k = rot(k_ref[0].astype(F32)) * (RET_QK_DIM ** -0.5)
    v = v_ref[0]
    qb = q.astype(BF16)
    s = lax.dot_general(qb, k.astype(BF16), _NT, preferred_element_type=F32) * dec_ref[0]
    o = jnp.dot(s.astype(BF16), v, preferred_element_type=F32)
    st = state[...]
    o = o + jnp.dot(qb, st.astype(BF16), preferred_element_type=F32) * qd_ref[0]
    kt = (k * kd_ref[0]).T.astype(BF16)
    state[...] = st * cd_ref[0] + jnp.dot(kt, v, preferred_element_type=F32)
    gate = gate_ref[0].astype(F32)
    o_ref[0] = (_rms(o, ng_ref[0]) * (gate * jax.nn.sigmoid(gate))).astype(o_ref.dtype)


def retention_core(proj, norm_g):
    B, S, _ = proj.shape
    H, dk, dv, L = RET_HEADS, RET_QK_DIM, RET_V_DIM, RET_BLOCK
    inv = 1.0 / (ROPE_BASE ** (jnp.arange(0, dk, 2, dtype=F32) / dk))
    ang = jnp.arange(S, dtype=F32)[:, None] * inv[None, :]
    log_gamma = jnp.log1p(-jnp.exp2(-5.0 - jnp.arange(H, dtype=F32)))
    idx = jnp.arange(L, dtype=F32)
    visible = (jnp.arange(L)[None, :] // CHUNK) <= (jnp.arange(L)[:, None] // CHUNK)
    decay = jnp.where(visible[None],
                      jnp.exp(log_gamma[:, None, None] * jnp.abs(idx[:, None] - idx[None, :])), 0.0)
    q_decay = jnp.exp(log_gamma[:, None] * (idx[None, :] + 1.0))[..., None]
    k_decay = jnp.exp(log_gamma[:, None] * (L - 1.0 - idx[None, :]))[..., None]
    block_decay = jnp.exp(log_gamma * L).reshape(H, 1, 1)
    return pl.pallas_call(
        _retention_kernel,
        grid=(B, H, S // L),
        in_specs=[pl.BlockSpec((1, L, dk), lambda b, h, c: (b, c, h)),
                  pl.BlockSpec((1, L, dk), lambda b, h, c: (b, c, H + h)),
                  pl.BlockSpec((1, L, dv), lambda b, h, c: (b, c, H + h)),
                  pl.BlockSpec((1, L, dv), lambda b, h, c: (b, c, 2 * H + h)),
                  pl.BlockSpec((L, dk // 2), lambda b, h, c: (c, 0)),
                  pl.BlockSpec((L, dk // 2), lambda b, h, c: (c, 0)),
                  pl.BlockSpec((1, L, L), lambda b, h, c: (h, 0, 0)),
                  pl.BlockSpec((1, L, 1), lambda b, h, c: (h, 0, 0)),
                  pl.BlockSpec((1, L, 1), lambda b, h, c: (h, 0, 0)),
                  pl.BlockSpec((1, 1, 1), lambda b, h, c: (h, 0, 0)),
                  pl.BlockSpec((1, 1, dv), lambda b, h, c: (h, 0, 0))],
        out_specs=pl.BlockSpec((1, L, dv), lambda b, h, c: (b, c, h)),
        out_shape=jax.ShapeDtypeStruct((B, S, RET_V_WIDTH), BF16),
        scratch_shapes=[pltpu.VMEM((dk, dv), F32)],
        compiler_params=_params("parallel", "parallel", "arbitrary"),
        name="retention_core",
    )(proj, proj, proj, proj, jnp.cos(ang), jnp.sin(ang), decay, q_decay, k_decay, block_decay,
      norm_g.reshape(H, 1, dv))


def _attn_kernel(*refs, mode, T):
    if mode == "diff":
        q_ref, k_ref, v_ref, bias_ref, lam_ref, ng_ref, o_ref, m_s, acc_s, l_s = refs
    else:
        q_ref, k_ref, v_ref, ck_ref, cq_ref, o_ref, m_s, acc_s, cq_s = refs
    i = pl.program_id(2)
    half = HEAD_LANES // 2
    reps = T // HEAD_LANES
    lane = lax.broadcasted_iota(jnp.int32, (1, HEAD_LANES), 1)
    first = lane < half
    q = q_ref[0] * jnp.asarray(half ** -0.5, BF16)
    zero = jnp.zeros_like(q)
    qs = (jnp.where(first, q, zero), jnp.where(first, zero, q))

    m_s[...] = jnp.full(m_s.shape, -jnp.inf, F32)
    acc_s[...] = jnp.zeros(acc_s.shape, F32)
    if mode == "diff":
        l_s[...] = jnp.zeros(l_s.shape, F32)
    else:
        for a in range(2):
            cq_s[a] = jnp.broadcast_to(cq_ref[0, 0, :, a:a + 1], (T, T))

    def step(j, diagonal):
        start = pl.multiple_of(j * T, T)
        k = k_ref[0, pl.ds(start, T), :]
        v = v_ref[0, pl.ds(start, T), :]
        if diagonal:
            row = lax.broadcasted_iota(jnp.int32, (T, T), 0)
            col = lax.broadcasted_iota(jnp.int32, (T, T), 1)
            if mode == "diff":
                visible = (col // CHUNK) <= (row // CHUNK)
            else:
                visible = col <= row
        for a in range(2):
            s = lax.dot_general(qs[a], k, _NT, preferred_element_type=F32)
            if mode == "diff":
                s = s + bias_ref[0, jnp.minimum(i - j, 2)]
                va = v
            else:
                s = (s - ck_ref[0, 0, j, a:a + 1, :]) + cq_s[a]
                va = jnp.where(first if a == 0 else jnp.logical_not(first), v, jnp.ones_like(v))
            if diagonal:
                s = jnp.where(visible, s, -jnp.inf)
            m_prev = m_s[a]
            m_new = jnp.maximum(m_prev, jnp.max(s, axis=-1, keepdims=True))
            alpha = jnp.exp(m_prev - m_new)
            p = jnp.exp(s - jnp.concatenate([m_new] * reps, axis=1))
            if mode == "diff":
                l_s[a] = alpha * l_s[a] + jnp.sum(p, axis=-1, keepdims=True)
            acc_s[a] = alpha * acc_s[a] + jnp.dot(p.astype(BF16), va, preferred_element_type=F32)
            m_s[a] = m_new

    lax.fori_loop(0, i, lambda j, c: (step(j, False), c)[1], 0)
    step(i, True)

    if mode == "diff":
        o = acc_s[0] / l_s[0] - lam_ref[0] * (acc_s[1] / l_s[1])
        o = _rms(o, ng_ref[...])
    else:
        o0, o1 = acc_s[0], acc_s[1]
        o = jnp.where(first, o0 / pltpu.roll(o0, half, 1), o1 / pltpu.roll(o1, half, 1))
    o_ref[0] = o.astype(o_ref.dtype)


def _attn_call(mode, proj, n_blocks, extra_inputs, extra_specs):
    B, S, _ = proj.shape
    T = ATTN_TILE
    return pl.pallas_call(
        functools.partial(_attn_kernel, mode=mode, T=T),
        grid=(B, n_blocks, S // T),
        in_specs=[pl.BlockSpec((1, T, HEAD_LANES), lambda b, h, i: (b, i, h)),
                  pl.BlockSpec((1, S, HEAD_LANES), lambda b, h, i: (b, 0, n_blocks + h)),
                  pl.BlockSpec((1, S, HEAD_LANES), lambda b, h, i: (b, 0, 2 * n_blocks + h))] + extra_specs,
        out_specs=pl.BlockSpec((1, T, HEAD_LANES), lambda b, h, i: (b, i, h)),
        out_shape=jax.ShapeDtypeStruct((B, S, D_MODEL), BF16),
        scratch_shapes=[pltpu.VMEM((2, T, HEAD_LANES), F32), pltpu.VMEM((2, T, HEAD_LANES), F32),
                        pltpu.VMEM((2, T, HEAD_LANES if mode == "diff" else T), F32)],
        compiler_params=_params("parallel", "parallel", "arbitrary"),
        name=mode + "_attention",
    )(proj, proj, proj, *extra_inputs)


def _t5_bucket(rel):
    nb = REL_BUCKETS // 2
    max_exact = nb // 2
    ret = (rel > 0).astype(jnp.int32) * nb
    n = jnp.abs(rel)
    nf = jnp.maximum(n, 1).astype(F32)
    large = max_exact + (jnp.log(nf / max_exact) / math.log(REL_MAX_DIST / max_exact)
                         * (nb - max_exact)).astype(jnp.int32)
    large = jnp.minimum(large, nb - 1)
    return ret + jnp.where(n < max_exact, n, large)


def diff_core(proj, lam_vecs, norm_g, rel_bias, lam_init):
    T = ATTN_TILE
    assert T >= REL_MAX_DIST
    lv = lam_vecs.astype(F32)
    lam = jnp.exp(jnp.sum(lv[0] * lv[1])) - jnp.exp(jnp.sum(lv[2] * lv[3])) + lam_init
    off = jnp.arange(T)
    rel = (off[None, None, :] - off[None, :, None]) - (jnp.arange(3) * T)[:, None, None]
    bias = jnp.transpose(rel_bias[_t5_bucket(rel)].astype(F32), (3, 0, 1, 2))
    g = (norm_g * (1.0 - lam_init)).reshape(1, HEAD_LANES)
    extra_specs = [pl.BlockSpec((1, 3, T, T), lambda b, h, i: (h, 0, 0, 0)),
                   pl.BlockSpec(memory_space=pltpu.SMEM),
                   pl.BlockSpec((1, HEAD_LANES), lambda b, h, i: (0, 0))]
    return _attn_call("diff", proj, DIFF_HEADS, [bias, lam.reshape(1), g], extra_specs)


def _fox_gate_kernel(h_ref, g_ref, w_ref, b_ref, ct_ref, c_ref, carry, *, L):
    @pl.when(pl.program_id(1) == 0)
    def _():
        carry[...] = jnp.zeros(carry.shape, F32)

    xn = _rms(h_ref[0], g_ref[...])
    logits = lax.dot_general(w_ref[...], xn.astype(BF16), _NT, preferred_element_type=F32)
    log_f = jax.nn.log_sigmoid(logits + b_ref[...])
    row = lax.broadcasted_iota(jnp.int32, (L, L), 0)
    col = lax.broadcasted_iota(jnp.int32, (L, L), 1)
    triu = (row <= col).astype(BF16)
    hi, mid, lo = _split3(log_f)
    d = functools.partial(jnp.dot, preferred_element_type=F32)
    c = carry[...] + ((d(hi, triu) + d(mid, triu)) + d(lo, triu))
    carry[...] = c[:, L - 1:L]
    ct_ref[0] = c
    c_ref[0] = c.T


def fox_gate(h3, g, w_f, b_f):
    B, S, D = h3.shape
    L = 512
    pad = LANE - FOX_HEADS
    return pl.pallas_call(
        functools.partial(_fox_gate_kernel, L=L),
        grid=(B, S // L),
        in_specs=[pl.BlockSpec((1, L, D), lambda b, c: (b, c, 0)),
                  pl.BlockSpec((1, D), lambda b, c: (0, 0)),
                  pl.BlockSpec((LANE, D), lambda b, c: (0, 0)),
                  pl.BlockSpec((LANE, 1), lambda b, c: (0, 0))],
        out_specs=[pl.BlockSpec((1, LANE, L), lambda b, c: (b, 0, c)),
                   pl.BlockSpec((1, L, LANE), lambda b, c: (b, c, 0))],
        out_shape=[jax.ShapeDtypeStruct((B, LANE, S), F32), jax.ShapeDtypeStruct((B, S, LANE), F32)],
        scratch_shapes=[pltpu.VMEM((LANE, 1), F32)],
        compiler_params=_params("parallel", "arbitrary"),
        name="fox_gate",
    )(h3, g.reshape(1, D), jnp.pad(w_f.T, ((0, pad), (0, 0))).astype(BF16),
      jnp.pad(b_f, (0, pad)).reshape(LANE, 1))


def fox_core(proj, c_t, c_rows):
    B, S, _ = proj.shape
    T = ATTN_TILE
    nb = FOX_HEADS // 2
    ck = jnp.transpose(c_t[:, :FOX_HEADS].reshape(B, nb, 2, S // T, T), (0, 1, 3, 2, 4))
    cq = jnp.transpose(c_rows[:, :, :FOX_HEADS].reshape(B, S, nb, 2), (0, 2, 1, 3))
    extra_specs = [pl.BlockSpec((1, 1, S // T, 2, T), lambda b, h, i: (b, h, 0, 0, 0)),
                   pl.BlockSpec((1, 1, T, 2), lambda b, h, i: (b, h, i, 0))]
    return _attn_call("fox", proj, nb, [ck, cq], extra_specs)


def _candidate_tables():
    pairs = [(a, b) for a in range(PEER_TOPK) for b in range(PEER_TOPK) if (a + 1) * (b + 1) <= PEER_TOPK]
    rows = 64
    p1 = np.zeros((rows, LANE), np.float32)
    p2 = np.zeros((rows, LANE), np.float32)
    for r, (a, b) in enumerate(pairs):
        p1[r, a] = 1.0
        p2[r, b] = 1.0
    return len(pairs), p1, p2


N_CAND, _P1, _P2 = _candidate_tables()


def _pop_max(work, idx, break_ties):
    mx = jnp.max(work, axis=0, keepdims=True)
    hit = work == mx
    if break_ties:
        pos = jnp.min(jnp.where(hit, idx, work.shape[0]), axis=0, keepdims=True)
        hit = idx == pos
    return mx, hit


def _top16_ranks(s, break_ties):
    n, t = s.shape
    idx = lax.broadcasted_iota(jnp.int32, (n, t), 0)
    ridx = lax.broadcasted_iota(jnp.int32, (PEER_TOPK, t), 0)
    rank = jnp.full((n, t), PEER_TOPK, jnp.int32)
    vals = jnp.zeros((PEER_TOPK, t), F32)
    work = s
    for r in range(PEER_TOPK):
        mx, hit = _pop_max(work, idx, break_ties)
        rank = jnp.where(hit, r, rank)
        work = jnp.where(hit, -jnp.inf, work)
        vals = jnp.where(ridx == r, mx, vals)
    return rank, vals


def _peer_select_kernel(q_ref, keys_ref, p1_ref, p2_ref, p1t_ref, rank2_ref, e2_ref, nrow_ref, e1_ref):
    def scores(c):
        kk = keys_ref[0, c]
        qq = q_ref[:, c * PEER_HALF:(c + 1) * PEER_HALF]
        kh, km, _ = _split3(kk)
        qh, qm, _ = _split3(qq)
        d = lambda a, b: lax.dot_general(a, b, _NT, preferred_element_type=F32)
        return d(kh, qh) + (d(kh, qm) + d(km, qh))

    s1 = scores(0)
    s2 = scores(1)
    out_refs = (rank2_ref, e2_ref, nrow_ref, e1_ref)
    clean = _peer_select_pass(s1, s2, p1_ref, p2_ref, p1t_ref, out_refs, break_ties=False)

    @pl.when(jnp.logical_not(clean))
    def _():
        _peer_select_pass(s1, s2, p1_ref, p2_ref, p1t_ref, out_refs, break_ties=True)


def _peer_select_pass(s1, s2, p1_ref, p2_ref, p1t_ref, out_refs, break_ties):
    rank2_ref, e2_ref, nrow_ref, e1_ref = out_refs
    rank1, v1 = _top16_ranks(s1, break_ties)
    rank2, v2 = _top16_ranks(s2, break_ties)

    tokens = s1.shape[1]
    pad = jnp.zeros((LANE - PEER_TOPK, tokens), F32)
    cand = (_dot_exact_lhs01(p1_ref[...], jnp.concatenate([v1, pad], axis=0))
            + _dot_exact_lhs01(p2_ref[...], jnp.concatenate([v2, pad], axis=0)))
    cidx = lax.broadcasted_iota(jnp.int32, cand.shape, 0)
    cand = jnp.where(cidx < N_CAND, cand, -jnp.inf)
    work = cand
    sel = jnp.zeros(cand.shape, F32)
    for _ in range(PEER_TOPK):
        _, hit = _pop_max(work, cidx, break_ties)
        sel = jnp.where(hit, 1.0, sel)
        work = jnp.where(hit, -jnp.inf, work)
    top = v1[0:1, :] + v2[0:1, :]
    z = jnp.sum(sel * jnp.exp(jnp.where(sel > 0.0, cand - top, 0.0)), axis=0, keepdims=True)
    sel_pad = jnp.concatenate([sel, jnp.zeros((LANE - sel.shape[0], tokens), F32)], axis=0).astype(BF16)
    n_by_rank = jnp.dot(p1t_ref[...], sel_pad, preferred_element_type=F32)
    nrow = jnp.zeros(s1.shape, F32)
    for a in range(PEER_TOPK):
        nrow = jnp.where(rank1 == a, n_by_rank[a:a + 1, :], nrow)

    rank2_ref[0] = rank2.astype(F32).astype(BF16)
    e2_ref[0] = jnp.exp(s2 - v2[0:1, :]).astype(BF16)
    e1 = jnp.exp(s1 - v1[0:1, :]) * (0.5 / z)
    for g in range(PEER_KEYS // SUBLANE):
        for st in range(tokens // LANE):
            tile = (slice(g * SUBLANE, (g + 1) * SUBLANE), slice(st * LANE, (st + 1) * LANE))
            nrow_ref[0, g, st] = nrow[tile]
            e1_ref[0, g, st] = e1[tile]

    if break_ties:
        return None
    ranked = (jnp.sum(jnp.where(rank1 < PEER_TOPK, 1.0, 0.0), axis=0, keepdims=True)
              + jnp.sum(jnp.where(rank2 < PEER_TOPK, 1.0, 0.0), axis=0, keepdims=True)
              + jnp.sum(sel, axis=0, keepdims=True))
    return jnp.max(ranked) == 3.0 * PEER_TOPK


def peer_select(q, keys):
    T = q.shape[0]
    tm = PEER_SELECT_TOKENS
    hk = pl.BlockSpec((1, PEER_KEYS, tm), lambda i, h: (h, 0, i))
    groups = PEER_KEYS // SUBLANE
    hk1 = pl.BlockSpec((1, groups, tm // LANE, SUBLANE, LANE), lambda i, h: (h, 0, i, 0, 0))
    shp = lambda dt: jax.ShapeDtypeStruct((PEER_HEADS, PEER_KEYS, T), dt)
    shp1 = jax.ShapeDtypeStruct((PEER_HEADS, groups, T // LANE, SUBLANE, LANE), F32)
    cst = lambda a: pl.BlockSpec(a.shape, lambda i, h: (0, 0))
    p1, p2 = jnp.asarray(_P1, BF16), jnp.asarray(_P2, BF16)
    p1t = jnp.asarray(np.pad(_P1.T, ((0, 0), (0, LANE - _P1.shape[0]))), BF16)
    return pl.pallas_call(
        _peer_select_kernel,
        grid=(T // tm, PEER_HEADS),
        in_specs=[pl.BlockSpec((tm, PEER_QUERY_DIM), lambda i, h: (i, h)),
                  pl.BlockSpec((1, 2, PEER_KEYS, PEER_HALF), lambda i, h: (h, 0, 0, 0)),
                  cst(p1), cst(p2), cst(p1t)],
        out_specs=[hk, hk, hk1, hk1],
        out_shape=[shp(BF16), shp(BF16), shp1, shp1],
        compiler_params=_params("parallel", "parallel"),
        name="peer_select",
    )(q, keys, p1, p2, p1t)


def _peer_dense_kernel(xt_ref, u_ref, u_next_ref, vt_ref, rank2_ref, e2_ref, nrow_ref, e1_ref,
                       nrow_next_ref, e1_next_ref, o_ref, acc, gbuf, hbuf, wbuf):
    j = pl.program_id(1)
    rows = PEER_EXPERT_TILE // PEER_KEYS
    tc = PEER_TOKEN_CHUNK
    n_chunks = xt_ref.shape[0]
    pack = 16

    def first_matmul(c, u):
        hbuf[c % 2] = jnp.dot(u[...], xt_ref[c], preferred_element_type=F32)

    def gate_weights(c, nrow, e1):
        cols = slice(c * tc, (c + 1) * tc)
        for ii in range(rows):
            w = jnp.zeros((PEER_KEYS, tc), BF16)
            for h in range(PEER_HEADS):
                def spread(ref):
                    tiles = [jnp.broadcast_to(ref[h, 0, c * (tc // LANE) + lt, ii:ii + 1, :],
                                              (pack, LANE)).astype(BF16) for lt in range(tc // LANE)]
                    return jnp.concatenate([jnp.concatenate(tiles, axis=1)] * (PEER_KEYS // pack), axis=0)
                w = w + jnp.where(rank2_ref[h, :, cols] < spread(nrow), e2_ref[h, :, cols] * spread(e1),
                                  jnp.zeros_like(w))
            wbuf[c % 2, ii * PEER_KEYS:(ii + 1) * PEER_KEYS, :] = w

    def finish(c):
        for ii in range(rows):
            r = slice(ii * PEER_KEYS, (ii + 1) * PEER_KEYS)
            hr = hbuf[c % 2, r, :]
            act = hr * (1.0 + lax.erf(hr * (2.0 ** -0.5)))
            gbuf[c, r, :] = (wbuf[c % 2, r, :].astype(F32) * act).astype(BF16)
        acc[c] += jnp.dot(vt_ref[...], gbuf[c], preferred_element_type=F32)

    @pl.when(j == 0)
    def _():
        acc[...] = jnp.zeros(acc.shape, F32)
        first_matmul(0, u_ref)
        gate_weights(0, nrow_ref, e1_ref)

    for c in range(n_chunks):
        if c + 1 < n_chunks:
            first_matmul(c + 1, u_ref)
            gate_weights(c + 1, nrow_ref, e1_ref)
        else:
            first_matmul(0, u_next_ref)
            gate_weights(0, nrow_next_ref, e1_next_ref)
        finish(c)

    @pl.when(j == pl.num_programs(1) - 1)
    def _():
        for c in range(n_chunks):
            o_ref[c * tc:(c + 1) * tc, :] = acc[c].T


def peer_dense(xt, u, vt, rank2, e2, nrow, e1):
    n_slabs, D, tc = xt.shape
    T = n_slabs * tc
    tm, te = PEER_TOKENS, PEER_EXPERT_TILE
    rows = te // PEER_KEYS
    n_tiles = PEER_EXPERTS // te
    assert rows == SUBLANE and tc == PEER_TOKEN_CHUNK and (tm // tc) % 2 == 0
    nxt = lambda j: jnp.minimum(j + 1, n_tiles - 1)
    by_key2 = pl.BlockSpec((PEER_HEADS, PEER_KEYS, tm), lambda i, j: (0, 0, i))
    by_key1 = pl.BlockSpec((PEER_HEADS, 1, tm // LANE, rows, LANE), lambda i, j: (0, j, i, 0, 0))
    by_key1_next = pl.BlockSpec((PEER_HEADS, 1, tm // LANE, rows, LANE), lambda i, j: (0, nxt(j), i, 0, 0))
    return pl.pallas_call(
        _peer_dense_kernel,
        grid=(T // tm, n_tiles),
        in_specs=[pl.BlockSpec((tm // tc, D, tc), lambda i, j: (i, 0, 0)),
                  pl.BlockSpec((te, D), lambda i, j: (j, 0)),
                  pl.BlockSpec((te, D), lambda i, j: (nxt(j), 0)),
                  pl.BlockSpec((D, te), lambda i, j: (0, j)),
                  by_key2, by_key2, by_key1, by_key1, by_key1_next, by_key1_next],
        out_specs=pl.BlockSpec((tm, D), lambda i, j: (i, 0)),
        out_shape=jax.ShapeDtypeStruct((T, D), F32),
        scratch_shapes=[pltpu.VMEM((tm // tc, D, tc), F32), pltpu.VMEM((tm // tc, te, tc), BF16),
                        pltpu.VMEM((2, te, tc), F32), pltpu.VMEM((2, te, tc), BF16)],
        compiler_params=pltpu.CompilerParams(dimension_semantics=("parallel", "arbitrary"),
                                             vmem_limit_bytes=PEER_DENSE_VMEM),
        name="peer_dense",
    )(xt, u, u, vt, rank2, e2, nrow, e1, nrow, e1)


def peer_layer(h, g, w_q, keys, u, v):
    q, xt = norm_matmul(h, g, w_q.astype(BF16), F32, emit_t=True)
    rank2, e2, nrow, e1 = peer_select(q, keys)
    return peer_dense(xt, u.astype(BF16), v.T.astype(BF16), rank2, e2, nrow, e1)


def _pad_cols(w, n):
    return jnp.pad(w, ((0, 0), (0, n - w.shape[1])))


def kernel(x, p, mix_norm, ffn_norm, ple_norm, final_norm, ssd_w_in, ssd_conv_w, ssd_conv_b, ssd_dt_bias, ssd_a_log, ssd_d, ssd_norm, ssd_w_out, ret_w_in, ret_norm, ret_w_out, diff_w_in, diff_lambda, diff_norm, diff_w_out, fox_w_in, fox_b_f, fox_w_out, rel_bias, peer_w_q, peer_keys, peer_u, peer_v, ple_proj, ple_gate):
    B, S, D = x.shape
    T = B * S
    depth = mix_norm.shape[0]
    n_mixers = 4
    h = x.reshape(T, D)
    p_all = p.reshape(depth, T, PLE_DIM)
    for i in range(depth):
        m, j = i % n_mixers, i // n_mixers
        g = mix_norm[i]
        if m == 0:
            w = ssd_w_in[j]
            proj = norm_matmul(h, g, w[:, :SSD_MAIN].astype(BF16), BF16)
            dt_raw = norm_matmul(h, g, _pad_cols(w[:, SSD_MAIN:], LANE).astype(BF16), F32)
            y = ssd_core(proj.reshape(B, S, SSD_MAIN), dt_raw.reshape(B, S, LANE), ssd_conv_w[j],
                         ssd_conv_b[j], ssd_dt_bias[j], ssd_a_log[j], ssd_d[j], ssd_norm[j])
            w_out = ssd_w_out[j]
        elif m == 1:
            proj = norm_matmul(h, g, ret_w_in[j].astype(BF16), BF16)
            y = retention_core(proj.reshape(B, S, -1), ret_norm[j])
            w_out = ret_w_out[j]
        elif m == 2:
            lam_init = 0.8 - 0.6 * math.exp(-0.3 * i)
            proj = norm_matmul(h, g, diff_w_in[j].astype(BF16), BF16)
            y = diff_core(proj.reshape(B, S, -1), diff_lambda[j], diff_norm[j], rel_bias, lam_init)
            w_out = diff_w_out[j]
        else:
            w = fox_w_in[j]
            proj = norm_matmul(h, g, w[:, :3 * D].astype(BF16), BF16)
            c_t, c_rows = fox_gate(h.reshape(B, S, D), g, w[:, 3 * D:], fox_b_f[j])
            y = fox_core(proj.reshape(B, S, -1), c_t, c_rows)
            w_out = fox_w_out[j]
        h = matmul_residual(y.reshape(T, -1), w_out.astype(BF16), h)
        y = peer_layer(h, ffn_norm[i], peer_w_q[i], peer_keys[i], peer_u[i], peer_v[i])
        h = ple_layer(h, y, ple_norm[i], ple_gate[i].astype(BF16), p_all, i, ple_proj[i].astype(BF16),
                      final_norm, final=(i == depth - 1))
    return h.reshape(B, S, D)
```

```python
import functools
import math

import jax
import jax.numpy as jnp
import numpy as np
from jax import lax
from jax.experimental import pallas as pl
from jax.experimental.pallas import tpu as pltpu

F32 = jnp.float32
BF16 = jnp.bfloat16

D_MODEL = 1024
CHUNK = 64
NORM_EPS = 1e-6
ROPE_BASE = 10000.0
PLE_DIM = 256

SSD_D_INNER = 2 * D_MODEL
SSD_HEAD_DIM = 64
SSD_HEADS = SSD_D_INNER // SSD_HEAD_DIM
SSD_GROUPS = 4
SSD_HEADS_PER_GROUP = SSD_HEADS // SSD_GROUPS
SSD_STATE = 128
SSD_CONV = 4
SSD_CONV_DIM = SSD_D_INNER + 2 * SSD_GROUPS * SSD_STATE
SSD_MAIN = SSD_D_INNER + SSD_CONV_DIM
SSD_BLOCK = 128

RET_HEADS = 4
RET_QK_DIM = D_MODEL // RET_HEADS
RET_V_DIM = 2 * RET_QK_DIM
RET_V_WIDTH = RET_HEADS * RET_V_DIM
RET_BLOCK = 256

DIFF_HEADS = 8
DIFF_HEAD_DIM = D_MODEL // DIFF_HEADS // 2
FOX_HEADS = 16
FOX_HEAD_DIM = D_MODEL // FOX_HEADS
ATTN_TILE = 512
HEAD_LANES = 128

REL_BUCKETS = 32
REL_MAX_DIST = 128

PEER_KEYS = 128
PEER_EXPERTS = PEER_KEYS * PEER_KEYS
PEER_HEADS = 8
PEER_TOPK = 16
PEER_QUERY_DIM = 256
PEER_HALF = PEER_QUERY_DIM // 2
PEER_TOKENS = 1024
PEER_TOKEN_CHUNK = 256
PEER_EXPERT_TILE = 1024
PEER_SELECT_TOKENS = 256
PEER_DENSE_VMEM = 56 * 1024 * 1024

LANE = 128
SUBLANE = 8
VMEM_LIMIT = 48 * 1024 * 1024

_NT = (((1,), (1,)), ((), ()))


def _params(*sem):
    return pltpu.CompilerParams(dimension_semantics=sem, vmem_limit_bytes=VMEM_LIMIT)


def _rms(x, g):
    return x * lax.rsqrt(jnp.mean(x * x, axis=-1, keepdims=True) + NORM_EPS) * g


def _split3(x):
    hi = x.astype(BF16)
    r1 = x - hi.astype(F32)
    mid = r1.astype(BF16)
    lo = (r1 - mid.astype(F32)).astype(BF16)
    return hi, mid, lo


def _dot_exact_lhs01(m01, x):
    hi, mid, lo = _split3(x)
    d = functools.partial(jnp.dot, preferred_element_type=F32)
    return (d(m01, hi) + d(m01, mid)) + d(m01, lo)


def _norm_matmul_kernel(h_ref, g_ref, w_ref, o_ref, *rest, emit_t):
    if emit_t:
        xt_ref, xn_ref = rest
    else:
        (xn_ref,) = rest

    @pl.when(pl.program_id(1) == 0)
    def _():
        y = _rms(h_ref[...], g_ref[...])
        xn_ref[...] = y.astype(BF16)
        if emit_t:
            tc = xt_ref.shape[2]
            for c in range(xt_ref.shape[0]):
                xt_ref[c] = y[c * tc:(c + 1) * tc, :].T.astype(BF16)

    o_ref[...] = jnp.dot(xn_ref[...], w_ref[...], preferred_element_type=F32).astype(o_ref.dtype)


def norm_matmul(h, g, w, out_dtype, emit_t=False, tm=1024, tn=1024):
    T, D = h.shape
    N = w.shape[1]
    tn = min(tn, N)
    assert T % tm == 0 and N % tn == 0
    out_shape = [jax.ShapeDtypeStruct((T, N), out_dtype)]
    out_specs = [pl.BlockSpec((tm, tn), lambda i, j: (i, j))]
    if emit_t:
        tc = PEER_TOKEN_CHUNK
        out_shape.append(jax.ShapeDtypeStruct((T // tc, D, tc), BF16))
        out_specs.append(pl.BlockSpec((tm // tc, D, tc), lambda i, j: (i, 0, 0)))
    res = pl.pallas_call(
        functools.partial(_norm_matmul_kernel, emit_t=emit_t),
        grid=(T // tm, N // tn),
        in_specs=[pl.BlockSpec((tm, D), lambda i, j: (i, 0)),
                  pl.BlockSpec((1, D), lambda i, j: (0, 0)),
                  pl.BlockSpec((D, tn), lambda i, j: (0, j))],
        out_specs=out_specs,
        out_shape=out_shape,
        scratch_shapes=[pltpu.VMEM((tm, D), BF16)],
        compiler_params=_params("parallel", "arbitrary"),
        name="norm_matmul",
    )(h, g.reshape(1, D), w)
    return res if emit_t else res[0]


def _matmul_residual_kernel(a_ref, w_ref, h_ref, o_ref):
    o_ref[...] = h_ref[...] + jnp.dot(a_ref[...], w_ref[...], preferred_element_type=F32)


def matmul_residual(a, w, h, tm=512):
    T, K = a.shape
    N = w.shape[1]
    return pl.pallas_call(
        _matmul_residual_kernel,
        grid=(T // tm,),
        in_specs=[pl.BlockSpec((tm, K), lambda i: (i, 0)),
                  pl.BlockSpec((K, N), lambda i: (0, 0)),
                  pl.BlockSpec((tm, N), lambda i: (i, 0))],
        out_specs=pl.BlockSpec((tm, N), lambda i: (i, 0)),
        out_shape=jax.ShapeDtypeStruct((T, N), F32),
        input_output_aliases={2: 0},
        compiler_params=_params("parallel"),
        name="matmul_residual",
    )(a, w, h)


def _ple_kernel(h_ref, y_ref, g_ref, wg_ref, p_ref, wp_ref, fg_ref, o_ref, *, final):
    x = h_ref[...] + y_ref[...]
    xn = _rms(x, g_ref[...]).astype(BF16)
    gate = jax.nn.sigmoid(jnp.dot(xn, wg_ref[...], preferred_element_type=F32))
    proj = jnp.dot(p_ref[...].astype(BF16), wp_ref[...], preferred_element_type=F32)
    y = x + gate * proj
    if final:
        y = _rms(y, fg_ref[...])
    o_ref[...] = y


def ple_layer(h, y, g, w_gate, p_all, layer, w_proj, final_g, final, tm=512):
    T, D = h.shape
    return pl.pallas_call(
        functools.partial(_ple_kernel, final=final),
        grid=(T // tm,),
        in_specs=[pl.BlockSpec((tm, D), lambda i: (i, 0)),
                  pl.BlockSpec((tm, D), lambda i: (i, 0)),
                  pl.BlockSpec((1, D), lambda i: (0, 0)),
                  pl.BlockSpec((D, D), lambda i: (0, 0)),
                  pl.BlockSpec((None, tm, PLE_DIM), lambda i: (layer, i, 0)),
                  pl.BlockSpec((PLE_DIM, D), lambda i: (0, 0)),
                  pl.BlockSpec((1, D), lambda i: (0, 0))],
        out_specs=pl.BlockSpec((tm, D), lambda i: (i, 0)),
        out_shape=jax.ShapeDtypeStruct((T, D), F32),
        input_output_aliases={0: 0},
        compiler_params=_params("parallel"),
        name="ple_layer",
    )(h, y, g.reshape(1, D), w_gate, p_all, w_proj, final_g.reshape(1, D))


def _ssd_kernel(proj_ref, dt_ref, cw_ref, cb_ref, dtb_ref, alog_ref, dsk_ref, ng_ref, o_ref,
                xbuf, state, ybuf, *, L):
    DI, P, N, R = SSD_D_INNER, SSD_HEAD_DIM, SSD_STATE, SSD_HEADS_PER_GROUP
    GW = DI // SSD_GROUPS

    @pl.when(pl.program_id(1) == 0)
    def _():
        xbuf[0:8, :] = jnp.zeros((8, SSD_CONV_DIM), F32)
        state[...] = jnp.zeros(state.shape, F32)

    xbc = proj_ref[0, :, DI:].astype(F32)
    xbuf[8:8 + L, :] = xbc
    conv = cb_ref[...] + cw_ref[3:4, :] * xbc
    for j in range(1, SSD_CONV):
        conv = conv + cw_ref[SSD_CONV - 1 - j:SSD_CONV - j, :] * xbuf[8 - j:8 - j + L, :]
    xbuf[0:8, :] = xbuf[L:L + 8, :]
    act = conv * jax.nn.sigmoid(conv)
    xs = act[:, :DI]
    bm = act[:, DI:DI + SSD_GROUPS * N]
    cm = act[:, DI + SSD_GROUPS * N:]

    dt = jax.nn.softplus(dt_ref[0] + dtb_ref[...])
    a = dt * (-jnp.exp(alog_ref[...]))
    row = lax.broadcasted_iota(jnp.int32, (L, L), 0)
    col = lax.broadcasted_iota(jnp.int32, (L, L), 1)
    tril = row >= col
    cum = _dot_exact_lhs01(tril.astype(BF16), a)
    cum_t = cum.T
    dt_t = dt.T
    last = cum[L - 1:L, :]
    wst = jnp.exp(last - cum) * dt
    ecum = jnp.exp(cum)
    elast = jnp.exp(last)

    for g in range(SSD_GROUPS):
        bg = bm[:, g * N:(g + 1) * N]
        cg = cm[:, g * N:(g + 1) * N].astype(BF16)
        cb = lax.dot_general(cg, bg.astype(BF16), _NT, preferred_element_type=F32)
        bg_t = bg.T.astype(BF16)
        for r in range(R):
            hd = g * R + r
            xh = xs[:, hd * P:(hd + 1) * P]
            seg = cum[:, hd:hd + 1] - cum_t[hd:hd + 1, :]
            w = jnp.exp(jnp.where(tril, seg, -jnp.inf)) * cb * dt_t[hd:hd + 1, :]
            st = state[g, :, r * P:(r + 1) * P]
            y = jnp.dot(w.astype(BF16), xh.astype(BF16), preferred_element_type=F32)
            y = y + jnp.dot(cg, st.astype(BF16), preferred_element_type=F32) * ecum[:, hd:hd + 1]
            xw = (xh * wst[:, hd:hd + 1]).astype(BF16)
            state[g, :, r * P:(r + 1) * P] = (st * elast[:, hd:hd + 1]
                                              + jnp.dot(bg_t, xw, preferred_element_type=F32))
            ybuf[:, hd * P:(hd + 1) * P] = y

    y = ybuf[...] + dsk_ref[...] * xs
    z = proj_ref[0, :, :DI].astype(F32)
    y = y * (z * jax.nn.sigmoid(z))
    for g in range(SSD_GROUPS):
        yg = y[:, g * GW:(g + 1) * GW]
        o_ref[0, :, g * GW:(g + 1) * GW] = _rms(yg, ng_ref[:, g * GW:(g + 1) * GW]).astype(o_ref.dtype)


def ssd_core(proj, dt_raw, conv_w, conv_b, dt_bias, a_log, d_skip, norm_g):
    B, S, _ = proj.shape
    L = SSD_BLOCK
    pad = LANE - SSD_HEADS
    vec = lambda n: pl.BlockSpec((1, n), lambda b, c: (0, 0))
    return pl.pallas_call(
        functools.partial(_ssd_kernel, L=L),
        grid=(B, S // L),
        in_specs=[pl.BlockSpec((1, L, SSD_MAIN), lambda b, c: (b, c, 0)),
                  pl.BlockSpec((1, L, LANE), lambda b, c: (b, c, 0)),
                  pl.BlockSpec((SSD_CONV, SSD_CONV_DIM), lambda b, c: (0, 0)),
                  vec(SSD_CONV_DIM), vec(LANE), vec(LANE), vec(SSD_D_INNER), vec(SSD_D_INNER)],
        out_specs=pl.BlockSpec((1, L, SSD_D_INNER), lambda b, c: (b, c, 0)),
        out_shape=jax.ShapeDtypeStruct((B, S, SSD_D_INNER), BF16),
        scratch_shapes=[pltpu.VMEM((L + 8, SSD_CONV_DIM), F32),
                        pltpu.VMEM((SSD_GROUPS, SSD_STATE, SSD_D_INNER // SSD_GROUPS), F32),
                        pltpu.VMEM((L, SSD_D_INNER), F32)],
        compiler_params=_params("parallel", "arbitrary"),
        name="ssd_core",
    )(proj, dt_raw,
      conv_w.reshape(SSD_CONV, SSD_CONV_DIM), conv_b.reshape(1, SSD_CONV_DIM),
      jnp.pad(dt_bias, (0, pad)).reshape(1, LANE), jnp.pad(a_log, (0, pad)).reshape(1, LANE),
      jnp.repeat(d_skip, SSD_HEAD_DIM).reshape(1, SSD_D_INNER), norm_g.reshape(1, SSD_D_INNER))


def _retention_kernel(q_ref, k_ref, v_ref, gate_ref, cos_ref, sin_ref, dec_ref, qd_ref, kd_ref,
                      cd_ref, ng_ref, o_ref, state):
    half = RET_QK_DIM // 2

    @pl.when(pl.program_id(2) == 0)
    def _():
        state[...] = jnp.zeros(state.shape, F32)

    cos = cos_ref[...]
    sin = sin_ref[...]

    def rot(x):
        x1, x2 = x[:, :half], x[:, half:]
        return jnp.concatenate([x1 * cos - x2 * sin, x1 * sin + x2 * cos], axis=-1)

    q = rot(q_ref[0].astype(F32))
    k = rot(k_ref[0].astype(F32)) * (RET_QK_DIM ** -0.5)
    v = v_ref[0]
    qb = q.astype(BF16)
    s = lax.dot_general(qb, k.astype(BF16), _NT, preferred_element_type=F32) * dec_ref[0]
    o = jnp.dot(s.astype(BF16), v, preferred_element_type=F32)
    st = state[...]
    o = o + jnp.dot(qb, st.astype(BF16), preferred_element_type=F32) * qd_ref[0]
    kt = (k * kd_ref[0]).T.astype(BF16)
    state[...] = st * cd_ref[0] + jnp.dot(kt, v, preferred_element_type=F32)
    gate = gate_ref[0].astype(F32)
    o_ref[0] = (_rms(o, ng_ref[0]) * (gate * jax.nn.sigmoid(gate))).astype(o_ref.dtype)


def retention_core(proj, norm_g):
    B, S, _ = proj.shape
    H, dk, dv, L = RET_HEADS, RET_QK_DIM, RET_V_DIM, RET_BLOCK
    inv = 1.0 / (ROPE_BASE ** (jnp.arange(0, dk, 2, dtype=F32) / dk))
    ang = jnp.arange(S, dtype=F32)[:, None] * inv[None, :]
    log_gamma = jnp.log1p(-jnp.exp2(-5.0 - jnp.arange(H, dtype=F32)))
    idx = jnp.arange(L, dtype=F32)
    visible = (jnp.arange(L)[None, :] // CHUNK) <= (jnp.arange(L)[:, None] // CHUNK)
    decay = jnp.where(visible[None],
                      jnp.exp(log_gamma[:, None, None] * jnp.abs(idx[:, None] - idx[None, :])), 0.0)
    q_decay = jnp.exp(log_gamma[:, None] * (idx[None, :] + 1.0))[..., None]
    k_decay = jnp.exp(log_gamma[:, None] * (L - 1.0 - idx[None, :]))[..., None]
    block_decay = jnp.exp(log_gamma * L).reshape(H, 1, 1)
    return pl.pallas_call(
        _retention_kernel,
        grid=(B, H, S // L),
        in_specs=[pl.BlockSpec((1, L, dk), lambda b, h, c: (b, c, h)),
                  pl.BlockSpec((1, L, dk), lambda b, h, c: (b, c, H + h)),
                  pl.BlockSpec((1, L, dv), lambda b, h, c: (b, c, H + h)),
                  pl.BlockSpec((1, L, dv), lambda b, h, c: (b, c, 2 * H + h)),
                  pl.BlockSpec((L, dk // 2), lambda b, h, c: (c, 0)),
                  pl.BlockSpec((L, dk // 2), lambda b, h, c: (c, 0)),
                  pl.BlockSpec((1, L, L), lambda b, h, c: (h, 0, 0)),
                  pl.BlockSpec((1, L, 1), lambda b, h, c: (h, 0, 0)),
                  pl.BlockSpec((1, L, 1), lambda b, h, c: (h, 0, 0)),
                  pl.BlockSpec((1, 1, 1), lambda b, h, c: (h, 0, 0)),
                  pl.BlockSpec((1, 1, dv), lambda b, h, c: (h, 0, 0))],
        out_specs=pl.BlockSpec((1, L, dv), lambda b, h, c: (b, c, h)),
        out_shape=jax.ShapeDtypeStruct((B, S, RET_V_WIDTH), BF16),
        scratch_shapes=[pltpu.VMEM((dk, dv), F32)],
        compiler_params=_params("parallel", "parallel", "arbitrary"),
        name="retention_core",
    )(proj, proj, proj, proj, jnp.cos(ang), jnp.sin(ang), decay, q_decay, k_decay, block_decay,
      norm_g.reshape(H, 1, dv))


def _attn_kernel(*refs, mode, T):
    if mode == "diff":
        q_ref, k_ref, v_ref, bias_ref, lam_ref, ng_ref, o_ref, m_s, acc_s, l_s = refs
    else:
        q_ref, k_ref, v_ref, ck_ref, cq_ref, o_ref, m_s, acc_s, cq_s = refs
    i = pl.program_id(2)
    half = HEAD_LANES // 2
    reps = T // HEAD_LANES
    lane = lax.broadcasted_iota(jnp.int32, (1, HEAD_LANES), 1)
    first = lane < half
    q = q_ref[0] * jnp.asarray(half ** -0.5, BF16)
    zero = jnp.zeros_like(q)
    qs = (jnp.where(first, q, zero), jnp.where(first, zero, q))

    m_s[...] = jnp.full(m_s.shape, -jnp.inf, F32)
    acc_s[...] = jnp.zeros(acc_s.shape, F32)
    if mode == "diff":
        l_s[...] = jnp.zeros(l_s.shape, F32)
    else:
        for a in range(2):
            cq_s[a] = jnp.broadcast_to(cq_ref[0, 0, :, a:a + 1], (T, T))

    def step(j, diagonal):
        start = pl.multiple_of(j * T, T)
        k = k_ref[0, pl.ds(start, T), :]
        v = v_ref[0, pl.ds(start, T), :]
        if diagonal:
            row = lax.broadcasted_iota(jnp.int32, (T, T), 0)
            col = lax.broadcasted_iota(jnp.int32, (T, T), 1)
            if mode == "diff":
                visible = (col // CHUNK) <= (row // CHUNK)
            else:
                visible = col <= row
        for a in range(2):
            s = lax.dot_general(qs[a], k, _NT, preferred_element_type=F32)
            if mode == "diff":
                s = s + bias_ref[0, jnp.minimum(i - j, 2)]
                va = v
            else:
                s = (s - ck_ref[0, 0, j, a:a + 1, :]) + cq_s[a]
                va = jnp.where(first if a == 0 else jnp.logical_not(first), v, jnp.ones_like(v))
            if diagonal:
                s = jnp.where(visible, s, -jnp.inf)
            m_prev = m_s[a]
            m_new = jnp.maximum(m_prev, jnp.max(s, axis=-1, keepdims=True))
            alpha = jnp.exp(m_prev - m_new)
            p = jnp.exp(s - jnp.concatenate([m_new] * reps, axis=1))
            if mode == "diff":
                l_s[a] = alpha * l_s[a] + jnp.sum(p, axis=-1, keepdims=True)
            acc_s[a] = alpha * acc_s[a] + jnp.dot(p.astype(BF16), va, preferred_element_type=F32)
            m_s[a] = m_new

    lax.fori_loop(0, i, lambda j, c: (step(j, False), c)[1], 0)
    step(i, True)

    if mode == "diff":
        o = acc_s[0] / l_s[0] - lam_ref[0] * (acc_s[1] / l_s[1])
        o = _rms(o, ng_ref[...])
    else:
        o0, o1 = acc_s[0], acc_s[1]
        o = jnp.where(first, o0 / pltpu.roll(o0, half, 1), o1 / pltpu.roll(o1, half, 1))
    o_ref[0] = o.astype(o_ref.dtype)


def _attn_call(mode, proj, n_blocks, extra_inputs, extra_specs):
    B, S, _ = proj.shape
    T = ATTN_TILE
    return pl.pallas_call(
        functools.partial(_attn_kernel, mode=mode, T=T),
        grid=(B, n_blocks, S // T),
        in_specs=[pl.BlockSpec((1, T, HEAD_LANES), lambda b, h, i: (b, i, h)),
                  pl.BlockSpec((1, S, HEAD_LANES), lambda b, h, i: (b, 0, n_blocks + h)),
                  pl.BlockSpec((1, S, HEAD_LANES), lambda b, h, i: (b, 0, 2 * n_blocks + h))] + extra_specs,
        out_specs=pl.BlockSpec((1, T, HEAD_LANES), lambda b, h, i: (b, i, h)),
        out_shape=jax.ShapeDtypeStruct((B, S, D_MODEL), BF16),
        scratch_shapes=[pltpu.VMEM((2, T, HEAD_LANES), F32), pltpu.VMEM((2, T, HEAD_LANES), F32),
                        pltpu.VMEM((2, T, HEAD_LANES if mode == "diff" else T), F32)],
        compiler_params=_params("parallel", "parallel", "arbitrary"),
        name=mode + "_attention",
    )(proj, proj, proj, *extra_inputs)


def _t5_bucket(rel):
    nb = REL_BUCKETS // 2
    max_exact = nb // 2
    ret = (rel > 0).astype(jnp.int32) * nb
    n = jnp.abs(rel)
    nf = jnp.maximum(n, 1).astype(F32)
    large = max_exact + (jnp.log(nf / max_exact) / math.log(REL_MAX_DIST / max_exact)
                         * (nb - max_exact)).astype(jnp.int32)
    large = jnp.minimum(large, nb - 1)
    return ret + jnp.where(n < max_exact, n, large)


def _bias_tiles_kernel(bucket_ref, table_ref, o_ref):
    h = pl.program_id(0)
    for d in range(bucket_ref.shape[0]):
        bucket = bucket_ref[d]
        tile = jnp.zeros(bucket.shape, F32)
        for b in range(REL_BUCKETS):
            tile = jnp.where(bucket == b, table_ref[b * DIFF_HEADS + h], tile)
        o_ref[0, d] = tile


def diff_bias_tiles(rel_bias):
    T = ATTN_TILE
    assert T >= REL_MAX_DIST
    off = jnp.arange(T)
    rel = (off[None, None, :] - off[None, :, None]) - (jnp.arange(3) * T)[:, None, None]
    return pl.pallas_call(
        _bias_tiles_kernel,
        grid=(DIFF_HEADS,),
        in_specs=[pl.BlockSpec((3, T, T), lambda h: (0, 0, 0)),
                  pl.BlockSpec(memory_space=pltpu.SMEM)],
        out_specs=pl.BlockSpec((1, 3, T, T), lambda h: (h, 0, 0, 0)),
        out_shape=jax.ShapeDtypeStruct((DIFF_HEADS, 3, T, T), F32),
        compiler_params=_params("parallel"),
        name="diff_bias_tiles",
    )(_t5_bucket(rel), rel_bias.astype(F32).reshape(REL_BUCKETS * DIFF_HEADS))


def diff_core(proj, lam_vecs, norm_g, rel_bias, lam_init):
    T = ATTN_TILE
    lv = lam_vecs.astype(F32)
    lam = jnp.exp(jnp.sum(lv[0] * lv[1])) - jnp.exp(jnp.sum(lv[2] * lv[3])) + lam_init
    bias = diff_bias_tiles(rel_bias)
    g = (norm_g * (1.0 - lam_init)).reshape(1, HEAD_LANES)
    extra_specs = [pl.BlockSpec((1, 3, T, T), lambda b, h, i: (h, 0, 0, 0)),
                   pl.BlockSpec(memory_space=pltpu.SMEM),
                   pl.BlockSpec((1, HEAD_LANES), lambda b, h, i: (0, 0))]
    return _attn_call("diff", proj, DIFF_HEADS, [bias, lam.reshape(1), g], extra_specs)


def _fox_gate_kernel(h_ref, g_ref, w_ref, b_ref, ct_ref, c_ref, carry, *, L):
    @pl.when(pl.program_id(1) == 0)
    def _():
        carry[...] = jnp.zeros(carry.shape, F32)

    xn = _rms(h_ref[0], g_ref[...])
    logits = lax.dot_general(w_ref[...], xn.astype(BF16), _NT, preferred_element_type=F32)
    log_f = jax.nn.log_sigmoid(logits + b_ref[...])
    row = lax.broadcasted_iota(jnp.int32, (L, L), 0)
    col = lax.broadcasted_iota(jnp.int32, (L, L), 1)
    triu = (row <= col).astype(BF16)
    hi, mid, lo = _split3(log_f)
    d = functools.partial(jnp.dot, preferred_element_type=F32)
    c = carry[...] + ((d(hi, triu) + d(mid, triu)) + d(lo, triu))
    carry[...] = c[:, L - 1:L]
    ct_ref[0] = c
    c_ref[0] = c.T


def fox_gate(h3, g, w_f, b_f):
    B, S, D = h3.shape
    L = 512
    pad = LANE - FOX_HEADS
    return pl.pallas_call(
        functools.partial(_fox_gate_kernel, L=L),
        grid=(B, S // L),
        in_specs=[pl.BlockSpec((1, L, D), lambda b, c: (b, c, 0)),
                  pl.BlockSpec((1, D), lambda b, c: (0, 0)),
                  pl.BlockSpec((LANE, D), lambda b, c: (0, 0)),
                  pl.BlockSpec((LANE, 1), lambda b, c: (0, 0))],
        out_specs=[pl.BlockSpec((1, LANE, L), lambda b, c: (b, 0, c)),
                   pl.BlockSpec((1, L, LANE), lambda b, c: (b, c, 0))],
        out_shape=[jax.ShapeDtypeStruct((B, LANE, S), F32), jax.ShapeDtypeStruct((B, S, LANE), F32)],
        scratch_shapes=[pltpu.VMEM((LANE, 1), F32)],
        compiler_params=_params("parallel", "arbitrary"),
        name="fox_gate",
    )(h3, g.reshape(1, D), jnp.pad(w_f.T, ((0, pad), (0, 0))).astype(BF16),
      jnp.pad(b_f, (0, pad)).reshape(LANE, 1))


def fox_core(proj, c_t, c_rows):
    B, S, _ = proj.shape
    T = ATTN_TILE
    nb = FOX_HEADS // 2
    ck = jnp.transpose(c_t[:, :FOX_HEADS].reshape(B, nb, 2, S // T, T), (0, 1, 3, 2, 4))
    cq = jnp.transpose(c_rows[:, :, :FOX_HEADS].reshape(B, S, nb, 2), (0, 2, 1, 3))
    extra_specs = [pl.BlockSpec((1, 1, S // T, 2, T), lambda b, h, i: (b, h, 0, 0, 0)),
                   pl.BlockSpec((1, 1, T, 2), lambda b, h, i: (b, h, i, 0))]
    return _attn_call("fox", proj, nb, [ck, cq], extra_specs)


def _candidate_tables():
    pairs = [(a, b) for a in range(PEER_TOPK) for b in range(PEER_TOPK) if (a + 1) * (b + 1) <= PEER_TOPK]
    rows = 64
    p1 = np.zeros((rows, LANE), np.float32)
    p2 = np.zeros((rows, LANE), np.float32)
    for r, (a, b) in enumerate(pairs):
        p1[r, a] = 1.0
        p2[r, b] = 1.0
    return len(pairs), p1, p2


N_CAND, _P1, _P2 = _candidate_tables()


def _pop_max(work, idx, break_ties):
    mx = jnp.max(work, axis=0, keepdims=True)
    hit = work == mx
    if break_ties:
        pos = jnp.min(jnp.where(hit, idx, work.shape[0]), axis=0, keepdims=True)
        hit = idx == pos
    return mx, hit


def _top16_ranks(s, break_ties):
    n, t = s.shape
    idx = lax.broadcasted_iota(jnp.int32, (n, t), 0)
    ridx = lax.broadcasted_iota(jnp.int32, (PEER_TOPK, t), 0)
    rank = jnp.full((n, t), PEER_TOPK, jnp.int32)
    vals = jnp.zeros((PEER_TOPK, t), F32)
    work = s
    for r in range(PEER_TOPK):
        mx, hit = _pop_max(work, idx, break_ties)
        rank = jnp.where(hit, r, rank)
        work = jnp.where(hit, -jnp.inf, work)
        vals = jnp.where(ridx == r, mx, vals)
    return rank, vals


def _peer_select_kernel(q_ref, keys_ref, p1_ref, p2_ref, p1t_ref, rank2_ref, e2_ref, nrow_ref, e1_ref):
    def scores(c):
        kk = keys_ref[0, c]
        qq = q_ref[:, c * PEER_HALF:(c + 1) * PEER_HALF]
        kh, km, _ = _split3(kk)
        qh, qm, _ = _split3(qq)
        d = lambda a, b: lax.dot_general(a, b, _NT, preferred_element_type=F32)
        return d(kh, qh) + (d(kh, qm) + d(km, qh))

    s1 = scores(0)
    s2 = scores(1)
    out_refs = (rank2_ref, e2_ref, nrow_ref, e1_ref)
    clean = _peer_select_pass(s1, s2, p1_ref, p2_ref, p1t_ref, out_refs, break_ties=False)

    @pl.when(jnp.logical_not(clean))
    def _():
        _peer_select_pass(s1, s2, p1_ref, p2_ref, p1t_ref, out_refs, break_ties=True)


def _peer_select_pass(s1, s2, p1_ref, p2_ref, p1t_ref, out_refs, break_ties):
    rank2_ref, e2_ref, nrow_ref, e1_ref = out_refs
    rank1, v1 = _top16_ranks(s1, break_ties)
    rank2, v2 = _top16_ranks(s2, break_ties)

    tokens = s1.shape[1]
    pad = jnp.zeros((LANE - PEER_TOPK, tokens), F32)
    cand = (_dot_exact_lhs01(p1_ref[...], jnp.concatenate([v1, pad], axis=0))
            + _dot_exact_lhs01(p2_ref[...], jnp.concatenate([v2, pad], axis=0)))
    cidx = lax.broadcasted_iota(jnp.int32, cand.shape, 0)
    cand = jnp.where(cidx < N_CAND, cand, -jnp.inf)
    work = cand
    sel = jnp.zeros(cand.shape, F32)
    for _ in range(PEER_TOPK):
        _, hit = _pop_max(work, cidx, break_ties)
        sel = jnp.where(hit, 1.0, sel)
        work = jnp.where(hit, -jnp.inf, work)
    top = v1[0:1, :] + v2[0:1, :]
    z = jnp.sum(sel * jnp.exp(jnp.where(sel > 0.0, cand - top, 0.0)), axis=0, keepdims=True)
    sel_pad = jnp.concatenate([sel, jnp.zeros((LANE - sel.shape[0], tokens), F32)], axis=0).astype(BF16)
    n_by_rank = jnp.dot(p1t_ref[...], sel_pad, preferred_element_type=F32)
    nrow = jnp.zeros(s1.shape, F32)
    for a in range(PEER_TOPK):
        nrow = jnp.where(rank1 == a, n_by_rank[a:a + 1, :], nrow)

    rank2_ref[0] = rank2.astype(F32).astype(BF16)
    e2_ref[0] = jnp.exp(s2 - v2[0:1, :]).astype(BF16)
    e1 = jnp.exp(s1 - v1[0:1, :]) * (0.5 / z)
    for g in range(PEER_KEYS // SUBLANE):
        for st in range(tokens // LANE):
            tile = (slice(g * SUBLANE, (g + 1) * SUBLANE), slice(st * LANE, (st + 1) * LANE))
            nrow_ref[0, g, st] = nrow[tile]
            e1_ref[0, g, st] = e1[tile]

    if break_ties:
        return None
    ranked = (jnp.sum(jnp.where(rank1 < PEER_TOPK, 1.0, 0.0), axis=0, keepdims=True)
              + jnp.sum(jnp.where(rank2 < PEER_TOPK, 1.0, 0.0), axis=0, keepdims=True)
              + jnp.sum(sel, axis=0, keepdims=True))
    return jnp.max(ranked) == 3.0 * PEER_TOPK


def peer_select(q, keys):
    T = q.shape[0]
    tm = PEER_SELECT_TOKENS
    hk = pl.BlockSpec((1, PEER_KEYS, tm), lambda i, h: (h, 0, i))
    groups = PEER_KEYS // SUBLANE
    hk1 = pl.BlockSpec((1, groups, tm // LANE, SUBLANE, LANE), lambda i, h: (h, 0, i, 0, 0))
    shp = lambda dt: jax.ShapeDtypeStruct((PEER_HEADS, PEER_KEYS, T), dt)
    shp1 = jax.ShapeDtypeStruct((PEER_HEADS, groups, T // LANE, SUBLANE, LANE), F32)
    cst = lambda a: pl.BlockSpec(a.shape, lambda i, h: (0, 0))
    p1, p2 = jnp.asarray(_P1, BF16), jnp.asarray(_P2, BF16)
    p1t = jnp.asarray(np.pad(_P1.T, ((0, 0), (0, LANE - _P1.shape[0]))), BF16)
    return pl.pallas_call(
        _peer_select_kernel,
        grid=(T // tm, PEER_HEADS),
        in_specs=[pl.BlockSpec((tm, PEER_QUERY_DIM), lambda i, h: (i, h)),
                  pl.BlockSpec((1, 2, PEER_KEYS, PEER_HALF), lambda i, h: (h, 0, 0, 0)),
                  cst(p1), cst(p2), cst(p1t)],
        out_specs=[hk, hk, hk1, hk1],
        out_shape=[shp(BF16), shp(BF16), shp1, shp1],
        compiler_params=_params("parallel", "parallel"),
        name="peer_select",
    )(q, keys, p1, p2, p1t)


def _peer_dense_kernel(xt_ref, u_ref, u_next_ref, vt_ref, rank2_ref, e2_ref, nrow_ref, e1_ref,
                       nrow_next_ref, e1_next_ref, o_ref, acc, gbuf, hbuf, wbuf):
    j = pl.program_id(1)
    rows = PEER_EXPERT_TILE // PEER_KEYS
    tc = PEER_TOKEN_CHUNK
    n_chunks = xt_ref.shape[0]
    pack = 16

    def first_matmul(c, u):
        hbuf[c % 2] = jnp.dot(u[...], xt_ref[c], preferred_element_type=F32).astype(BF16)

    def gate_weights(c, nrow, e1):
        cols = slice(c * tc, (c + 1) * tc)
        for ii in range(rows):
            w = jnp.zeros((PEER_KEYS, tc), BF16)
            for h in range(PEER_HEADS):
                def spread(ref):
                    tiles = [jnp.broadcast_to(ref[h, 0, c * (tc // LANE) + lt, ii:ii + 1, :],
                                              (pack, LANE)).astype(BF16) for lt in range(tc // LANE)]
                    return jnp.concatenate([jnp.concatenate(tiles, axis=1)] * (PEER_KEYS // pack), axis=0)
                w = w + jnp.where(rank2_ref[h, :, cols] < spread(nrow), e2_ref[h, :, cols] * spread(e1),
                                  jnp.zeros_like(w))
            wbuf[c % 2, ii * PEER_KEYS:(ii + 1) * PEER_KEYS, :] = w

    def finish(c):
        for ii in range(rows):
            r = slice(ii * PEER_KEYS, (ii + 1) * PEER_KEYS)
            hr = hbuf[c % 2, r, :]
            act = hr * (1.0 + lax.erf(hr * jnp.asarray(2.0 ** -0.5, BF16)))
            gbuf[c, r, :] = wbuf[c % 2, r, :] * act
        acc[c] += jnp.dot(vt_ref[...], gbuf[c], preferred_element_type=F32)

    @pl.when(j == 0)
    def _():
        acc[...] = jnp.zeros(acc.shape, F32)
        first_matmul(0, u_ref)
        gate_weights(0, nrow_ref, e1_ref)

    for c in range(n_chunks):
        if c + 1 < n_chunks:
            first_matmul(c + 1, u_ref)
            gate_weights(c + 1, nrow_ref, e1_ref)
        else:
            first_matmul(0, u_next_ref)
            gate_weights(0, nrow_next_ref, e1_next_ref)
        finish(c)

    @pl.when(j == pl.num_programs(1) - 1)
    def _():
        for c in range(n_chunks):
            o_ref[c * tc:(c + 1) * tc, :] = acc[c].T


def peer_dense(xt, u, vt, rank2, e2, nrow, e1):
    n_slabs, D, tc = xt.shape
    T = n_slabs * tc
    tm, te = PEER_TOKENS, PEER_EXPERT_TILE
    rows = te // PEER_KEYS
    n_tiles = PEER_EXPERTS // te
    assert rows == SUBLANE and tc == PEER_TOKEN_CHUNK and (tm // tc) % 2 == 0
    nxt = lambda j: jnp.minimum(j + 1, n_tiles - 1)
    by_key2 = pl.BlockSpec((PEER_HEADS, PEER_KEYS, tm), lambda i, j: (0, 0, i))
    by_key1 = pl.BlockSpec((PEER_HEADS, 1, tm // LANE, rows, LANE), lambda i, j: (0, j, i, 0, 0))
    by_key1_next = pl.BlockSpec((PEER_HEADS, 1, tm // LANE, rows, LANE), lambda i, j: (0, nxt(j), i, 0, 0))
    return pl.pallas_call(
        _peer_dense_kernel,
        grid=(T // tm, n_tiles),
        in_specs=[pl.BlockSpec((tm // tc, D, tc), lambda i, j: (i, 0, 0)),
                  pl.BlockSpec((te, D), lambda i, j: (j, 0)),
                  pl.BlockSpec((te, D), lambda i, j: (nxt(j), 0)),
                  pl.BlockSpec((D, te), lambda i, j: (0, j)),
                  by_key2, by_key2, by_key1, by_key1, by_key1_next, by_key1_next],
        out_specs=pl.BlockSpec((tm, D), lambda i, j: (i, 0)),
        out_shape=jax.ShapeDtypeStruct((T, D), F32),
        scratch_shapes=[pltpu.VMEM((tm // tc, D, tc), F32), pltpu.VMEM((tm // tc, te, tc), BF16),
                        pltpu.VMEM((2, te, tc), BF16), pltpu.VMEM((2, te, tc), BF16)],
        compiler_params=pltpu.CompilerParams(dimension_semantics=("parallel", "arbitrary"),
                                             vmem_limit_bytes=PEER_DENSE_VMEM),
        name="peer_dense",
    )(xt, u, u, vt, rank2, e2, nrow, e1, nrow, e1)


def peer_layer(h, g, w_q, keys, u, v):
    q, xt = norm_matmul(h, g, w_q.astype(BF16), F32, emit_t=True)
    rank2, e2, nrow, e1 = peer_select(q, keys)
    return peer_dense(xt, u.astype(BF16), v.T.astype(BF16), rank2, e2, nrow, e1)


def _pad_cols(w, n):
    return jnp.pad(w, ((0, 0), (0, n - w.shape[1])))


def kernel(x, p, mix_norm, ffn_norm, ple_norm, final_norm, ssd_w_in, ssd_conv_w, ssd_conv_b, ssd_dt_bias, ssd_a_log, ssd_d, ssd_norm, ssd_w_out, ret_w_in, ret_norm, ret_w_out, diff_w_in, diff_lambda, diff_norm, diff_w_out, fox_w_in, fox_b_f, fox_w_out, rel_bias, peer_w_q, peer_keys, peer_u, peer_v, ple_proj, ple_gate):
    B, S, D = x.shape
    T = B * S
    depth = mix_norm.shape[0]
    n_mixers = 4
    h = x.reshape(T, D)
    p_all = p.reshape(depth, T, PLE_DIM)
    for i in range(depth):
        m, j = i % n_mixers, i // n_mixers
        g = mix_norm[i]
        if m == 0:
            w = ssd_w_in[j]
            proj = norm_matmul(h, g, w[:, :SSD_MAIN].astype(BF16), BF16)
            dt_raw = norm_matmul(h, g, _pad_cols(w[:, SSD_MAIN:], LANE).astype(BF16), F32)
            y = ssd_core(proj.reshape(B, S, SSD_MAIN), dt_raw.reshape(B, S, LANE), ssd_conv_w[j],
                         ssd_conv_b[j], ssd_dt_bias[j], ssd_a_log[j], ssd_d[j], ssd_norm[j])
            w_out = ssd_w_out[j]
        elif m == 1:
            proj = norm_matmul(h, g, ret_w_in[j].astype(BF16), BF16)
            y = retention_core(proj.reshape(B, S, -1), ret_norm[j])
            w_out = ret_w_out[j]
        elif m == 2:
            lam_init = 0.8 - 0.6 * math.exp(-0.3 * i)
            proj = norm_matmul(h, g, diff_w_in[j].astype(BF16), BF16)
            y = diff_core(proj.reshape(B, S, -1), diff_lambda[j], diff_norm[j], rel_bias, lam_init)
            w_out = diff_w_out[j]
        else:
            w = fox_w_in[j]
            proj = norm_matmul(h, g, w[:, :3 * D].astype(BF16), BF16)
            c_t, c_rows = fox_gate(h.reshape(B, S, D), g, w[:, 3 * D:], fox_b_f[j])
            y = fox_core(proj.reshape(B, S, -1), c_t, c_rows)
            w_out = fox_w_out[j]
        h = matmul_residual(y.reshape(T, -1), w_out.astype(BF16), h)
        y = peer_layer(h, ffn_norm[i], peer_w_q[i], peer_keys[i], peer_u[i], peer_v[i])
        h = ple_layer(h, y, ple_norm[i], ple_gate[i].astype(BF16), p_all, i, ple_proj[i].astype(BF16),
                      final_norm, final=(i == depth - 1))
    return h.reshape(B, S, D)
```

```python
import functools
import math

import jax
import jax.numpy as jnp
import numpy as np
from jax import lax
from jax.experimental import pallas as pl
from jax.experimental.pallas import tpu as pltpu

F32 = jnp.float32
BF16 = jnp.bfloat16

D_MODEL = 1024
CHUNK = 64
NORM_EPS = 1e-6
ROPE_BASE = 10000.0
PLE_DIM = 256

SSD_D_INNER = 2 * D_MODEL
SSD_HEAD_DIM = 64
SSD_HEADS = SSD_D_INNER // SSD_HEAD_DIM
SSD_GROUPS = 4
SSD_HEADS_PER_GROUP = SSD_HEADS // SSD_GROUPS
SSD_STATE = 128
SSD_CONV = 4
SSD_CONV_DIM = SSD_D_INNER + 2 * SSD_GROUPS * SSD_STATE
SSD_MAIN = SSD_D_INNER + SSD_CONV_DIM
SSD_BLOCK = 128

RET_HEADS = 4
RET_QK_DIM = D_MODEL // RET_HEADS
RET_V_DIM = 2 * RET_QK_DIM
RET_V_WIDTH = RET_HEADS * RET_V_DIM
RET_BLOCK = 256

DIFF_HEADS = 8
DIFF_HEAD_DIM = D_MODEL // DIFF_HEADS // 2
FOX_HEADS = 16
FOX_HEAD_DIM = D_MODEL // FOX_HEADS
ATTN_TILE = 512
HEAD_LANES = 128

REL_BUCKETS = 32
REL_MAX_DIST = 128

PEER_KEYS = 128
PEER_EXPERTS = PEER_KEYS * PEER_KEYS
PEER_HEADS = 8
PEER_TOPK = 16
PEER_QUERY_DIM = 256
PEER_HALF = PEER_QUERY_DIM // 2
PEER_TOKENS = 1024
PEER_TOKEN_CHUNK = 512
PEER_EXPERT_TILE = 1024
PEER_SELECT_TOKENS = 256
PEER_DENSE_VMEM = 56 * 1024 * 1024

LANE = 128
SUBLANE = 8
VMEM_LIMIT = 48 * 1024 * 1024

_NT = (((1,), (1,)), ((), ()))


def _params(*sem):
    return pltpu.CompilerParams(dimension_semantics=sem, vmem_limit_bytes=VMEM_LIMIT)


def _rms(x, g):
    return x * lax.rsqrt(jnp.mean(x * x, axis=-1, keepdims=True) + NORM_EPS) * g


def _split3(x):
    hi = x.astype(BF16)
    r1 = x - hi.astype(F32)
    mid = r1.astype(BF16)
    lo = (r1 - mid.astype(F32)).astype(BF16)
    return hi, mid, lo


def _dot_exact_lhs01(m01, x):
    hi, mid, lo = _split3(x)
    d = functools.partial(jnp.dot, preferred_element_type=F32)
    return (d(m01, hi) + d(m01, mid)) + d(m01, lo)


def _norm_matmul_kernel(h_ref, g_ref, w_ref, o_ref, *rest, emit_t):
    if emit_t:
        xt_ref, xn_ref = rest
    else:
        (xn_ref,) = rest

    @pl.when(pl.program_id(1) == 0)
    def _():
        y = _rms(h_ref[...], g_ref[...])
        xn_ref[...] = y.astype(BF16)
        if emit_t:
            tc = xt_ref.shape[2]
            for c in range(xt_ref.shape[0]):
                xt_ref[c] = y[c * tc:(c + 1) * tc, :].T.astype(BF16)

    o_ref[...] = jnp.dot(xn_ref[...], w_ref[...], preferred_element_type=F32).astype(o_ref.dtype)


def norm_matmul(h, g, w, out_dtype, emit_t=False, tm=1024, tn=1024):
    T, D = h.shape
    N = w.shape[1]
    tn = min(tn, N)
    assert T % tm == 0 and N % tn == 0
    out_shape = [jax.ShapeDtypeStruct((T, N), out_dtype)]
    out_specs = [pl.BlockSpec((tm, tn), lambda i, j: (i, j))]
    if emit_t:
        tc = PEER_TOKEN_CHUNK
        out_shape.append(jax.ShapeDtypeStruct((T // tc, D, tc), BF16))
        out_specs.append(pl.BlockSpec((tm // tc, D, tc), lambda i, j: (i, 0, 0)))
    res = pl.pallas_call(
        functools.partial(_norm_matmul_kernel, emit_t=emit_t),
        grid=(T // tm, N // tn),
        in_specs=[pl.BlockSpec((tm, D), lambda i, j: (i, 0)),
                  pl.BlockSpec((1, D), lambda i, j: (0, 0)),
                  pl.BlockSpec((D, tn), lambda i, j: (0, j))],
        out_specs=out_specs,
        out_shape=out_shape,
        scratch_shapes=[pltpu.VMEM((tm, D), BF16)],
        compiler_params=_params("parallel", "arbitrary"),
        name="norm_matmul",
    )(h, g.reshape(1, D), w)
    return res if emit_t else res[0]


def _matmul_residual_kernel(a_ref, w_ref, h_ref, o_ref):
    o_ref[...] = h_ref[...] + jnp.dot(a_ref[...], w_ref[...], preferred_element_type=F32)


def matmul_residual(a, w, h, tm=512):
    T, K = a.shape
    N = w.shape[1]
    return pl.pallas_call(
        _matmul_residual_kernel,
        grid=(T // tm,),
        in_specs=[pl.BlockSpec((tm, K), lambda i: (i, 0)),
                  pl.BlockSpec((K, N), lambda i: (0, 0)),
                  pl.BlockSpec((tm, N), lambda i: (i, 0))],
        out_specs=pl.BlockSpec((tm, N), lambda i: (i, 0)),
        out_shape=jax.ShapeDtypeStruct((T, N), F32),
        input_output_aliases={2: 0},
        compiler_params=_params("parallel"),
        name="matmul_residual",
    )(a, w, h)


def _ple_kernel(h_ref, y_ref, g_ref, wg_ref, p_ref, wp_ref, fg_ref, o_ref, *, final):
    x = h_ref[...] + y_ref[...]
    xn = _rms(x, g_ref[...]).astype(BF16)
    gate = jax.nn.sigmoid(jnp.dot(xn, wg_ref[...], preferred_element_type=F32))
    proj = jnp.dot(p_ref[...].astype(BF16), wp_ref[...], preferred_element_type=F32)
    y = x + gate * proj
    if final:
        y = _rms(y, fg_ref[...])
    o_ref[...] = y


def ple_layer(h, y, g, w_gate, p_all, layer, w_proj, final_g, final, tm=512):
    T, D = h.shape
    return pl.pallas_call(
        functools.partial(_ple_kernel, final=final),
        grid=(T // tm,),
        in_specs=[pl.BlockSpec((tm, D), lambda i: (i, 0)),
                  pl.BlockSpec((tm, D), lambda i: (i, 0)),
                  pl.BlockSpec((1, D), lambda i: (0, 0)),
                  pl.BlockSpec((D, D), lambda i: (0, 0)),
                  pl.BlockSpec((None, tm, PLE_DIM), lambda i: (layer, i, 0)),
                  pl.BlockSpec((PLE_DIM, D), lambda i: (0, 0)),
                  pl.BlockSpec((1, D), lambda i: (0, 0))],
        out_specs=pl.BlockSpec((tm, D), lambda i: (i, 0)),
        out_shape=jax.ShapeDtypeStruct((T, D), F32),
        input_output_aliases={0: 0},
        compiler_params=_params("parallel"),
        name="ple_layer",
    )(h, y, g.reshape(1, D), w_gate, p_all, w_proj, final_g.reshape(1, D))


def _ssd_kernel(proj_ref, dt_ref, cw_ref, cb_ref, dtb_ref, alog_ref, dsk_ref, ng_ref, o_ref,
                xbuf, state, ybuf, *, L):
    DI, P, N, R = SSD_D_INNER, SSD_HEAD_DIM, SSD_STATE, SSD_HEADS_PER_GROUP
    GW = DI // SSD_GROUPS

    @pl.when(pl.program_id(1) == 0)
    def _():
        xbuf[0:8, :] = jnp.zeros((8, SSD_CONV_DIM), F32)
        state[...] = jnp.zeros(state.shape, F32)

    xbc = proj_ref[0, :, DI:].astype(F32)
    xbuf[8:8 + L, :] = xbc
    conv = cb_ref[...] + cw_ref[3:4, :] * xbc
    for j in range(1, SSD_CONV):
        conv = conv + cw_ref[SSD_CONV - 1 - j:SSD_CONV - j, :] * xbuf[8 - j:8 - j + L, :]
    xbuf[0:8, :] = xbuf[L:L + 8, :]
    act = conv * jax.nn.sigmoid(conv)
    xs = act[:, :DI]
    bm = act[:, DI:DI + SSD_GROUPS * N]
    cm = act[:, DI + SSD_GROUPS * N:]

    dt = jax.nn.softplus(dt_ref[0] + dtb_ref[...])
    a = dt * (-jnp.exp(alog_ref[...]))
    row = lax.broadcasted_iota(jnp.int32, (L, L), 0)
    col = lax.broadcasted_iota(jnp.int32, (L, L), 1)
    tril = row >= col
    cum = _dot_exact_lhs01(tril.astype(BF16), a)
    cum_t = cum.T
    dt_t = dt.T
    last = cum[L - 1:L, :]
    wst = jnp.exp(last - cum) * dt
    ecum = jnp.exp(cum)
    elast = jnp.exp(last)

    for g in range(SSD_GROUPS):
        bg = bm[:, g * N:(g + 1) * N]
        cg = cm[:, g * N:(g + 1) * N].astype(BF16)
        cb = lax.dot_general(cg, bg.astype(BF16), _NT, preferred_element_type=F32)
        bg_t = bg.T.astype(BF16)
        for r in range(R):
            hd = g * R + r
            xh = xs[:, hd * P:(hd + 1) * P]
            seg = cum[:, hd:hd + 1] - cum_t[hd:hd + 1, :]
            w = jnp.exp(jnp.where(tril, seg, -jnp.inf)) * cb * dt_t[hd:hd + 1, :]
            st = state[g, :, r * P:(r + 1) * P]
            y = jnp.dot(w.astype(BF16), xh.astype(BF16), preferred_element_type=F32)
            y = y + jnp.dot(cg, st.astype(BF16), preferred_element_type=F32) * ecum[:, hd:hd + 1]
            xw = (xh * wst[:, hd:hd + 1]).astype(BF16)
            state[g, :, r * P:(r + 1) * P] = (st * elast[:, hd:hd + 1]
                                              + jnp.dot(bg_t, xw, preferred_element_type=F32))
            ybuf[:, hd * P:(hd + 1) * P] = y

    y = ybuf[...] + dsk_ref[...] * xs
    z = proj_ref[0, :, :DI].astype(F32)
    y = y * (z * jax.nn.sigmoid(z))
    for g in range(SSD_GROUPS):
        yg = y[:, g * GW:(g + 1) * GW]
        o_ref[0, :, g * GW:(g + 1) * GW] = _rms(yg, ng_ref[:, g * GW:(g + 1) * GW]).astype(o_ref.dtype)


def ssd_core(proj, dt_raw, conv_w, conv_b, dt_bias, a_log, d_skip, norm_g):
    B, S, _ = proj.shape
    L = SSD_BLOCK
    pad = LANE - SSD_HEADS
    vec = lambda n: pl.BlockSpec((1, n), lambda b, c: (0, 0))
    return pl.pallas_call(
        functools.partial(_ssd_kernel, L=L),
        grid=(B, S // L),
        in_specs=[pl.BlockSpec((1, L, SSD_MAIN), lambda b, c: (b, c, 0)),
                  pl.BlockSpec((1, L, LANE), lambda b, c: (b, c, 0)),
                  pl.BlockSpec((SSD_CONV, SSD_CONV_DIM), lambda b, c: (0, 0)),
                  vec(SSD_CONV_DIM), vec(LANE), vec(LANE), vec(SSD_D_INNER), vec(SSD_D_INNER)],
        out_specs=pl.BlockSpec((1, L, SSD_D_INNER), lambda b, c: (b, c, 0)),
        out_shape=jax.ShapeDtypeStruct((B, S, SSD_D_INNER), BF16),
        scratch_shapes=[pltpu.VMEM((L + 8, SSD_CONV_DIM), F32),
                        pltpu.VMEM((SSD_GROUPS, SSD_STATE, SSD_D_INNER // SSD_GROUPS), F32),
                        pltpu.VMEM((L, SSD_D_INNER), F32)],
        compiler_params=_params("parallel", "arbitrary"),
        name="ssd_core",
    )(proj, dt_raw,
      conv_w.reshape(SSD_CONV, SSD_CONV_DIM), conv_b.reshape(1, SSD_CONV_DIM),
      jnp.pad(dt_bias, (0, pad)).reshape(1, LANE), jnp.pad(a_log, (0, pad)).reshape(1, LANE),
      jnp.repeat(d_skip, SSD_HEAD_DIM).reshape(1, SSD_D_INNER), norm_g.reshape(1, SSD_D_INNER))


def _retention_kernel(q_ref, k_ref, v_ref, gate_ref, cos_ref, sin_ref, dec_ref, qd_ref, kd_ref,
                      cd_ref, ng_ref, o_ref, state):
    half = RET_QK_DIM // 2

    @pl.when(pl.program_id(2) == 0)
    def _():
        state[...] = jnp.zeros(state.shape, F32)

    cos = cos_ref[...]
    sin = sin_ref[...]

    def rot(x):
        x1, x2 = x[:, :half], x[:, half:]
        return jnp.concatenate([x1 * cos - x2 * sin, x1 * sin + x2 * cos], axis=-1)

    q = rot(q_ref[0].astype(F32))
    k = rot(k_ref[0].astype(F32)) * (RET_QK_DIM ** -0.5)
    v = v_ref[0]
    qb = q.astype(BF16)
    s = lax.dot_general(qb, k.astype(BF16), _NT, preferred_element_type=F32) * dec_ref[0]
    o = jnp.dot(s.astype(BF16), v, preferred_element_type=F32)
    st = state[...]
    o = o + jnp.dot(qb, st.astype(BF16), preferred_element_type=F32) * qd_ref[0]
    kt = (k * kd_ref[0]).T.astype(BF16)
    state[...] = st * cd_ref[0] + jnp.dot(kt, v, preferred_element_type=F32)
    gate = gate_ref[0].astype(F32)
    o_ref[0] = (_rms(o, ng_ref[0]) * (gate * jax.nn.sigmoid(gate))).astype(o_ref.dtype)


def retention_core(proj, norm_g):
    B, S, _ = proj.shape
    H, dk, dv, L = RET_HEADS, RET_QK_DIM, RET_V_DIM, RET_BLOCK
    inv = 1.0 / (ROPE_BASE ** (jnp.arange(0, dk, 2, dtype=F32) / dk))
    ang = jnp.arange(S, dtype=F32)[:, None] * inv[None, :]
    log_gamma = jnp.log1p(-jnp.exp2(-5.0 - jnp.arange(H, dtype=F32)))
    idx = jnp.arange(L, dtype=F32)
    visible = (jnp.arange(L)[None, :] // CHUNK) <= (jnp.arange(L)[:, None] // CHUNK)
    decay = jnp.where(visible[None],
                      jnp.exp(log_gamma[:, None, None] * jnp.abs(idx[:, None] - idx[None, :])), 0.0)
    q_decay = jnp.exp(log_gamma[:, None] * (idx[None, :] + 1.0))[..., None]
    k_decay = jnp.exp(log_gamma[:, None] * (L - 1.0 - idx[None, :]))[..., None]
    block_decay = jnp.exp(log_gamma * L).reshape(H, 1, 1)
    return pl.pallas_call(
        _retention_kernel,
        grid=(B, H, S // L),
        in_specs=[pl.BlockSpec((1, L, dk), lambda b, h, c: (b, c, h)),
                  pl.BlockSpec((1, L, dk), lambda b, h, c: (b, c, H + h)),
                  pl.BlockSpec((1, L, dv), lambda b, h, c: (b, c, H + h)),
                  pl.BlockSpec((1, L, dv), lambda b, h, c: (b, c, 2 * H + h)),
                  pl.BlockSpec((L, dk // 2), lambda b, h, c: (c, 0)),
                  pl.BlockSpec((L, dk // 2), lambda b, h, c: (c, 0)),
                  pl.BlockSpec((1, L, L), lambda b, h, c: (h, 0, 0)),
                  pl.BlockSpec((1, L, 1), lambda b, h, c: (h, 0, 0)),
                  pl.BlockSpec((1, L, 1), lambda b, h, c: (h, 0, 0)),
                  pl.BlockSpec((1, 1, 1), lambda b, h, c: (h, 0, 0)),
                  pl.BlockSpec((1, 1, dv), lambda b, h, c: (h, 0, 0))],
        out_specs=pl.BlockSpec((1, L, dv), lambda b, h, c: (b, c, h)),
        out_shape=jax.ShapeDtypeStruct((B, S, RET_V_WIDTH), BF16),
        scratch_shapes=[pltpu.VMEM((dk, dv), F32)],
        compiler_params=_params("parallel", "parallel", "arbitrary"),
        name="retention_core",
    )(proj, proj, proj, proj, jnp.cos(ang), jnp.sin(ang), decay, q_decay, k_decay, block_decay,
      norm_g.reshape(H, 1, dv))


def _attn_kernel(*refs, mode, T):
    if mode == "diff":
        q_ref, k_ref, v_ref, bias_ref, lam_ref, ng_ref, o_ref, m_s, acc_s, l_s = refs
    else:
        q_ref, k_ref, v_ref, ck_ref, cq_ref, o_ref, m_s, acc_s, cq_s = refs
    i = pl.program_id(2)
    half = HEAD_LANES // 2
    reps = T // HEAD_LANES
    lane = lax.broadcasted_iota(jnp.int32, (1, HEAD_LANES), 1)
    first = lane < half
    q = q_ref[0] * jnp.asarray(half ** -0.5, BF16)
    zero = jnp.zeros_like(q)
    qs = (jnp.where(first, q, zero), jnp.where(first, zero, q))

    m_s[...] = jnp.full(m_s.shape, -jnp.inf, F32)
    acc_s[...] = jnp.zeros(acc_s.shape, F32)
    if mode == "diff":
        l_s[...] = jnp.zeros(l_s.shape, F32)
    else:
        for a in range(2):
            cq_s[a] = jnp.broadcast_to(cq_ref[0, 0, :, a:a + 1], (T, T))

    def step(j, diagonal):
        start = pl.multiple_of(j * T, T)
        k = k_ref[0, pl.ds(start, T), :]
        v = v_ref[0, pl.ds(start, T), :]
        if diagonal:
            row = lax.broadcasted_iota(jnp.int32, (T, T), 0)
            col = lax.broadcasted_iota(jnp.int32, (T, T), 1)
            if mode == "diff":
                visible = (col // CHUNK) <= (row // CHUNK)
            else:
                visible = col <= row
        for a in range(2):
            s = lax.dot_general(qs[a], k, _NT, preferred_element_type=F32)
            if mode == "diff":
                s = s + bias_ref[0, jnp.minimum(i - j, 2)]
                va = v
            else:
                s = (s - ck_ref[0, 0, j, a:a + 1, :]) + cq_s[a]
                va = jnp.where(first if a == 0 else jnp.logical_not(first), v, jnp.ones_like(v))
            if diagonal:
                s = jnp.where(visible, s, -jnp.inf)
            m_prev = m_s[a]
            m_new = jnp.maximum(m_prev, jnp.max(s, axis=-1, keepdims=True))
            alpha = jnp.exp(m_prev - m_new)
            p = jnp.exp(s - jnp.concatenate([m_new] * reps, axis=1))
            if mode == "diff":
                l_s[a] = alpha * l_s[a] + jnp.sum(p, axis=-1, keepdims=True)
            acc_s[a] = alpha * acc_s[a] + jnp.dot(p.astype(BF16), va, preferred_element_type=F32)
            m_s[a] = m_new

    lax.fori_loop(0, i, lambda j, c: (step(j, False), c)[1], 0)
    step(i, True)

    if mode == "diff":
        o = acc_s[0] / l_s[0] - lam_ref[0] * (acc_s[1] / l_s[1])
        o = _rms(o, ng_ref[...])
    else:
        o0, o1 = acc_s[0], acc_s[1]
        o = jnp.where(first, o0 / pltpu.roll(o0, half, 1), o1 / pltpu.roll(o1, half, 1))
    o_ref[0] = o.astype(o_ref.dtype)


def _attn_call(mode, proj, n_blocks, extra_inputs, extra_specs):
    B, S, _ = proj.shape
    T = ATTN_TILE
    return pl.pallas_call(
        functools.partial(_attn_kernel, mode=mode, T=T),
        grid=(B, n_blocks, S // T),
        in_specs=[pl.BlockSpec((1, T, HEAD_LANES), lambda b, h, i: (b, i, h)),
                  pl.BlockSpec((1, S, HEAD_LANES), lambda b, h, i: (b, 0, n_blocks + h)),
                  pl.BlockSpec((1, S, HEAD_LANES), lambda b, h, i: (b, 0, 2 * n_blocks + h))] + extra_specs,
        out_specs=pl.BlockSpec((1, T, HEAD_LANES), lambda b, h, i: (b, i, h)),
        out_shape=jax.ShapeDtypeStruct((B, S, D_MODEL), BF16),
        scratch_shapes=[pltpu.VMEM((2, T, HEAD_LANES), F32), pltpu.VMEM((2, T, HEAD_LANES), F32),
                        pltpu.VMEM((2, T, HEAD_LANES if mode == "diff" else T), F32)],
        compiler_params=_params("parallel", "parallel", "arbitrary"),
        name=mode + "_attention",
    )(proj, proj, proj, *extra_inputs)


def _t5_bucket(rel):
    nb = REL_BUCKETS // 2
    max_exact = nb // 2
    ret = (rel > 0).astype(jnp.int32) * nb
    n = jnp.abs(rel)
    nf = jnp.maximum(n, 1).astype(F32)
    large = max_exact + (jnp.log(nf / max_exact) / math.log(REL_MAX_DIST / max_exact)
                         * (nb - max_exact)).astype(jnp.int32)
    large = jnp.minimum(large, nb - 1)
    return ret + jnp.where(n < max_exact, n, large)


def _bias_tiles_kernel(bucket_ref, table_ref, o_ref):
    h = pl.program_id(0)
    for d in range(bucket_ref.shape[0]):
        bucket = bucket_ref[d]
        tile = jnp.zeros(bucket.shape, F32)
        for b in range(REL_BUCKETS):
            tile = jnp.where(bucket == b, table_ref[b * DIFF_HEADS + h], tile)
        o_ref[0, d] = tile


def diff_bias_tiles(rel_bias):
    T = ATTN_TILE
    assert T >= REL_MAX_DIST
    off = jnp.arange(T)
    rel = (off[None, None, :] - off[None, :, None]) - (jnp.arange(3) * T)[:, None, None]
    return pl.pallas_call(
        _bias_tiles_kernel,
        grid=(DIFF_HEADS,),
        in_specs=[pl.BlockSpec((3, T, T), lambda h: (0, 0, 0)),
                  pl.BlockSpec(memory_space=pltpu.SMEM)],
        out_specs=pl.BlockSpec((1, 3, T, T), lambda h: (h, 0, 0, 0)),
        out_shape=jax.ShapeDtypeStruct((DIFF_HEADS, 3, T, T), F32),
        compiler_params=_params("parallel"),
        name="diff_bias_tiles",
    )(_t5_bucket(rel), rel_bias.astype(F32).reshape(REL_BUCKETS * DIFF_HEADS))


def diff_core(proj, lam_vecs, norm_g, rel_bias, lam_init):
    T = ATTN_TILE
    lv = lam_vecs.astype(F32)
    lam = jnp.exp(jnp.sum(lv[0] * lv[1])) - jnp.exp(jnp.sum(lv[2] * lv[3])) + lam_init
    bias = diff_bias_tiles(rel_bias)
    g = (norm_g * (1.0 - lam_init)).reshape(1, HEAD_LANES)
    extra_specs = [pl.BlockSpec((1, 3, T, T), lambda b, h, i: (h, 0, 0, 0)),
                   pl.BlockSpec(memory_space=pltpu.SMEM),
                   pl.BlockSpec((1, HEAD_LANES), lambda b, h, i: (0, 0))]
    return _attn_call("diff", proj, DIFF_HEADS, [bias, lam.reshape(1), g], extra_specs)


def _fox_gate_kernel(h_ref, g_ref, w_ref, b_ref, ct_ref, c_ref, carry, *, L):
    @pl.when(pl.program_id(1) == 0)
    def _():
        carry[...] = jnp.zeros(carry.shape, F32)

    xn = _rms(h_ref[0], g_ref[...])
    logits = lax.dot_general(w_ref[...], xn.astype(BF16), _NT, preferred_element_type=F32)
    log_f = jax.nn.log_sigmoid(logits + b_ref[...])
    row = lax.broadcasted_iota(jnp.int32, (L, L), 0)
    col = lax.broadcasted_iota(jnp.int32, (L, L), 1)
    triu = (row <= col).astype(BF16)
    hi, mid, lo = _split3(log_f)
    d = functools.partial(jnp.dot, preferred_element_type=F32)
    c = carry[...] + ((d(hi, triu) + d(mid, triu)) + d(lo, triu))
    carry[...] = c[:, L - 1:L]
    ct_ref[0] = c
    c_ref[0] = c.T


def fox_gate(h3, g, w_f, b_f):
    B, S, D = h3.shape
    L = 512
    pad = LANE - FOX_HEADS
    return pl.pallas_call(
        functools.partial(_fox_gate_kernel, L=L),
        grid=(B, S // L),
        in_specs=[pl.BlockSpec((1, L, D), lambda b, c: (b, c, 0)),
                  pl.BlockSpec((1, D), lambda b, c: (0, 0)),
                  pl.BlockSpec((LANE, D), lambda b, c: (0, 0)),
                  pl.BlockSpec((LANE, 1), lambda b, c: (0, 0))],
        out_specs=[pl.BlockSpec((1, LANE, L), lambda b, c: (b, 0, c)),
                   pl.BlockSpec((1, L, LANE), lambda b, c: (b, c, 0))],
        out_shape=[jax.ShapeDtypeStruct((B, LANE, S), F32), jax.ShapeDtypeStruct((B, S, LANE), F32)],
        scratch_shapes=[pltpu.VMEM((LANE, 1), F32)],
        compiler_params=_params("parallel", "arbitrary"),
        name="fox_gate",
    )(h3, g.reshape(1, D), jnp.pad(w_f.T, ((0, pad), (0, 0))).astype(BF16),
      jnp.pad(b_f, (0, pad)).reshape(LANE, 1))


def fox_core(proj, c_t, c_rows):
    B, S, _ = proj.shape
    T = ATTN_TILE
    nb = FOX_HEADS // 2
    ck = jnp.transpose(c_t[:, :FOX_HEADS].reshape(B, nb, 2, S // T, T), (0, 1, 3, 2, 4))
    cq = jnp.transpose(c_rows[:, :, :FOX_HEADS].reshape(B, S, nb, 2), (0, 2, 1, 3))
    extra_specs = [pl.BlockSpec((1, 1, S // T, 2, T), lambda b, h, i: (b, h, 0, 0, 0)),
                   pl.BlockSpec((1, 1, T, 2), lambda b, h, i: (b, h, i, 0))]
    return _attn_call("fox", proj, nb, [ck, cq], extra_specs)


def _candidate_tables():
    pairs = [(a, b) for a in range(PEER_TOPK) for b in range(PEER_TOPK) if (a + 1) * (b + 1) <= PEER_TOPK]
    rows = 64
    p1 = np.zeros((rows, LANE), np.float32)
    p2 = np.zeros((rows, LANE), np.float32)
    for r, (a, b) in enumerate(pairs):
        p1[r, a] = 1.0
        p2[r, b] = 1.0
    return len(pairs), p1, p2


N_CAND, _P1, _P2 = _candidate_tables()


_INT_MIN = -2 ** 31
_FLIP = 0x7FFFFFFF


def _ordered_int(x):
    b = lax.bitcast_convert_type(x + 0.0, jnp.int32)
    return jnp.where(b < 0, b ^ _FLIP, b)


def _ordered_float(k):
    return lax.bitcast_convert_type(jnp.where(k < 0, k ^ _FLIP, k), F32)


def _pop_top16(keys, break_ties):
    n, t = keys.shape
    idx = lax.broadcasted_iota(jnp.int32, (n, t), 0)
    ridx = lax.broadcasted_iota(jnp.int32, (PEER_TOPK, t), 0)
    vals = jnp.zeros((PEER_TOPK, t), jnp.int32)
    work = keys
    for r in range(PEER_TOPK):
        mx = jnp.max(work, axis=0, keepdims=True)
        hit = work == mx
        if break_ties:
            pos = jnp.min(jnp.where(hit, idx, n), axis=0, keepdims=True)
            hit = idx == pos
        work = jnp.where(hit, _INT_MIN + r, work)
        vals = jnp.where(ridx == r, mx, vals)
    return work, vals


def _top16_ranks(s, break_ties):
    work, vals = _pop_top16(_ordered_int(s), break_ties)
    rank = jnp.where(work < _INT_MIN + PEER_TOPK, work & (2 * PEER_TOPK - 1), PEER_TOPK)
    return rank, _ordered_float(vals)


def _peer_select_kernel(q_ref, keys_ref, p1_ref, p2_ref, p1t_ref, rank2_ref, e2_ref, nrow_ref, e1_ref):
    def scores(c):
        kk = keys_ref[0, c]
        qq = q_ref[:, c * PEER_HALF:(c + 1) * PEER_HALF]
        kh, km, _ = _split3(kk)
        qh, qm, _ = _split3(qq)
        d = lambda a, b: lax.dot_general(a, b, _NT, preferred_element_type=F32)
        return d(kh, qh) + (d(kh, qm) + d(km, qh))

    s1 = scores(0)
    s2 = scores(1)
    out_refs = (rank2_ref, e2_ref, nrow_ref, e1_ref)
    clean = _peer_select_pass(s1, s2, p1_ref, p2_ref, p1t_ref, out_refs, break_ties=False)

    @pl.when(jnp.logical_not(clean))
    def _():
        _peer_select_pass(s1, s2, p1_ref, p2_ref, p1t_ref, out_refs, break_ties=True)


def _peer_select_pass(s1, s2, p1_ref, p2_ref, p1t_ref, out_refs, break_ties):
    rank2_ref, e2_ref, nrow_ref, e1_ref = out_refs
    rank1, v1 = _top16_ranks(s1, break_ties)
    rank2, v2 = _top16_ranks(s2, break_ties)

    tokens = s1.shape[1]
    pad = jnp.zeros((LANE - PEER_TOPK, tokens), F32)
    cand = (_dot_exact_lhs01(p1_ref[...], jnp.concatenate([v1, pad], axis=0))
            + _dot_exact_lhs01(p2_ref[...], jnp.concatenate([v2, pad], axis=0)))
    cidx = lax.broadcasted_iota(jnp.int32, cand.shape, 0)
    cand = jnp.where(cidx < N_CAND, cand, -jnp.inf)
    popped, _ = _pop_top16(_ordered_int(cand), break_ties)
    sel = jnp.where(popped < _INT_MIN + PEER_TOPK, 1.0, 0.0)
    top = v1[0:1, :] + v2[0:1, :]
    z = jnp.sum(sel * jnp.exp(jnp.where(sel > 0.0, cand - top, 0.0)), axis=0, keepdims=True)
    sel_pad = jnp.concatenate([sel, jnp.zeros((LANE - sel.shape[0], tokens), F32)], axis=0).astype(BF16)
    n_by_rank = jnp.dot(p1t_ref[...], sel_pad, preferred_element_type=F32)
    nrow = jnp.zeros(s1.shape, F32)
    for a in range(PEER_TOPK):
        nrow = jnp.where(rank1 == a, n_by_rank[a:a + 1, :], nrow)

    rank2_ref[0] = rank2.astype(F32).astype(BF16)
    e2_ref[0] = jnp.exp(s2 - v2[0:1, :]).astype(BF16)
    e1 = jnp.exp(s1 - v1[0:1, :]) * (0.5 / z)
    for g in range(PEER_KEYS // SUBLANE):
        for st in range(tokens // LANE):
            tile = (slice(g * SUBLANE, (g + 1) * SUBLANE), slice(st * LANE, (st + 1) * LANE))
            nrow_ref[0, g, st] = nrow[tile]
            e1_ref[0, g, st] = e1[tile]

    if break_ties:
        return None
    ranked = (jnp.sum(jnp.where(rank1 < PEER_TOPK, 1.0, 0.0), axis=0, keepdims=True)
              + jnp.sum(jnp.where(rank2 < PEER_TOPK, 1.0, 0.0), axis=0, keepdims=True)
              + jnp.sum(sel, axis=0, keepdims=True))
    return jnp.max(ranked) == 3.0 * PEER_TOPK


def peer_select(q, keys):
    T = q.shape[0]
    tm = PEER_SELECT_TOKENS
    hk = pl.BlockSpec((1, PEER_KEYS, tm), lambda i, h: (h, 0, i))
    groups = PEER_KEYS // SUBLANE
    hk1 = pl.BlockSpec((1, groups, tm // LANE, SUBLANE, LANE), lambda i, h: (h, 0, i, 0, 0))
    shp = lambda dt: jax.ShapeDtypeStruct((PEER_HEADS, PEER_KEYS, T), dt)
    shp1 = jax.ShapeDtypeStruct((PEER_HEADS, groups, T // LANE, SUBLANE, LANE), F32)
    cst = lambda a: pl.BlockSpec(a.shape, lambda i, h: (0, 0))
    p1, p2 = jnp.asarray(_P1, BF16), jnp.asarray(_P2, BF16)
    p1t = jnp.asarray(np.pad(_P1.T, ((0, 0), (0, LANE - _P1.shape[0]))), BF16)
    return pl.pallas_call(
        _peer_select_kernel,
        grid=(T // tm, PEER_HEADS),
        in_specs=[pl.BlockSpec((tm, PEER_QUERY_DIM), lambda i, h: (i, h)),
                  pl.BlockSpec((1, 2, PEER_KEYS, PEER_HALF), lambda i, h: (h, 0, 0, 0)),
                  cst(p1), cst(p2), cst(p1t)],
        out_specs=[hk, hk, hk1, hk1],
        out_shape=[shp(BF16), shp(BF16), shp1, shp1],
        compiler_params=_params("parallel", "parallel"),
        name="peer_select",
    )(q, keys, p1, p2, p1t)


def _peer_dense_kernel(xt_ref, u_ref, u_next_ref, vt_ref, rank2_ref, e2_ref, nrow_ref, e1_ref,
                       nrow_next_ref, e1_next_ref, o_ref, *scratch):
    n = xt_ref.shape[0]
    acc, gbuf, hbuf, wbuf = scratch[:n], scratch[n:2 * n], scratch[2 * n:2 * n + 2], scratch[2 * n + 2:]
    j = pl.program_id(1)
    rows = PEER_EXPERT_TILE // PEER_KEYS
    tc = PEER_TOKEN_CHUNK
    n_chunks = xt_ref.shape[0]
    pack = 16

    def first_matmul(c, u):
        hbuf[c % 2][...] = jnp.dot(u[...], xt_ref[c], preferred_element_type=F32).astype(BF16)

    def gate_weights(c, nrow, e1):
        cols = slice(c * tc, (c + 1) * tc)
        for ii in range(rows):
            w = jnp.zeros((PEER_KEYS, tc), BF16)
            for h in range(PEER_HEADS):
                def spread(ref):
                    tiles = [jnp.broadcast_to(ref[h, 0, c * (tc // LANE) + lt, ii:ii + 1, :],
                                              (pack, LANE)).astype(BF16) for lt in range(tc // LANE)]
                    return jnp.concatenate([jnp.concatenate(tiles, axis=1)] * (PEER_KEYS // pack), axis=0)
                w = w + jnp.where(rank2_ref[h, :, cols] < spread(nrow), e2_ref[h, :, cols] * spread(e1),
                                  jnp.zeros_like(w))
            wbuf[c % 2][ii * PEER_KEYS:(ii + 1) * PEER_KEYS, :] = w

    def finish(c):
        for ii in range(rows):
            r = slice(ii * PEER_KEYS, (ii + 1) * PEER_KEYS)
            hr = hbuf[c % 2][r, :]
            act = hr * (1.0 + lax.erf(hr * jnp.asarray(2.0 ** -0.5, BF16)))
            gbuf[c][r, :] = wbuf[c % 2][r, :] * act
        acc[c][...] += jnp.dot(vt_ref[...], gbuf[c][...], preferred_element_type=F32)

    @pl.when(j == 0)
    def _():
        for a in acc:
            a[...] = jnp.zeros(a.shape, F32)
        first_matmul(0, u_ref)
        gate_weights(0, nrow_ref, e1_ref)

    for c in range(n_chunks):
        if c + 1 < n_chunks:
            first_matmul(c + 1, u_ref)
            gate_weights(c + 1, nrow_ref, e1_ref)
        else:
            first_matmul(0, u_next_ref)
            gate_weights(0, nrow_next_ref, e1_next_ref)
        finish(c)

    @pl.when(j == pl.num_programs(1) - 1)
    def _():
        for c in range(n_chunks):
            o_ref[c * tc:(c + 1) * tc, :] = acc[c][...].T


def peer_dense(xt, u, vt, rank2, e2, nrow, e1):
    n_slabs, D, tc = xt.shape
    T = n_slabs * tc
    tm, te = PEER_TOKENS, PEER_EXPERT_TILE
    rows = te // PEER_KEYS
    n_tiles = PEER_EXPERTS // te
    assert rows == SUBLANE and tc == PEER_TOKEN_CHUNK and (tm // tc) % 2 == 0
    nxt = lambda j: jnp.minimum(j + 1, n_tiles - 1)
    by_key2 = pl.BlockSpec((PEER_HEADS, PEER_KEYS, tm), lambda i, j: (0, 0, i))
    by_key1 = pl.BlockSpec((PEER_HEADS, 1, tm // LANE, rows, LANE), lambda i, j: (0, j, i, 0, 0))
    by_key1_next = pl.BlockSpec((PEER_HEADS, 1, tm // LANE, rows, LANE), lambda i, j: (0, nxt(j), i, 0, 0))
    return pl.pallas_call(
        _peer_dense_kernel,
        grid=(T // tm, n_tiles),
        in_specs=[pl.BlockSpec((tm // tc, D, tc), lambda i, j: (i, 0, 0)),
                  pl.BlockSpec((te, D), lambda i, j: (j, 0)),
                  pl.BlockSpec((te, D), lambda i, j: (nxt(j), 0)),
                  pl.BlockSpec((D, te), lambda i, j: (0, j)),
                  by_key2, by_key2, by_key1, by_key1, by_key1_next, by_key1_next],
        out_specs=pl.BlockSpec((tm, D), lambda i, j: (i, 0)),
        out_shape=jax.ShapeDtypeStruct((T, D), F32),
        scratch_shapes=([pltpu.VMEM((D, tc), F32)] * (tm // tc) + [pltpu.VMEM((te, tc), BF16)] * (tm // tc)
                        + [pltpu.VMEM((te, tc), BF16)] * 4),
        compiler_params=pltpu.CompilerParams(dimension_semantics=("parallel", "arbitrary"),
                                             vmem_limit_bytes=PEER_DENSE_VMEM),
        name="peer_dense",
    )(xt, u, u, vt, rank2, e2, nrow, e1, nrow, e1)


def peer_layer(h, g, w_q, keys, u, v):
    q, xt = norm_matmul(h, g, w_q.astype(BF16), F32, emit_t=True)
    rank2, e2, nrow, e1 = peer_select(q, keys)
    return peer_dense(xt, u.astype(BF16), v.T.astype(BF16), rank2, e2, nrow, e1)


def _pad_cols(w, n):
    return jnp.pad(w, ((0, 0), (0, n - w.shape[1])))


def kernel(x, p, mix_norm, ffn_norm, ple_norm, final_norm, ssd_w_in, ssd_conv_w, ssd_conv_b, ssd_dt_bias, ssd_a_log, ssd_d, ssd_norm, ssd_w_out, ret_w_in, ret_norm, ret_w_out, diff_w_in, diff_lambda, diff_norm, diff_w_out, fox_w_in, fox_b_f, fox_w_out, rel_bias, peer_w_q, peer_keys, peer_u, peer_v, ple_proj, ple_gate):
    B, S, D = x.shape
    T = B * S
    depth = mix_norm.shape[0]
    n_mixers = 4
    h = x.reshape(T, D)
    p_all = p.reshape(depth, T, PLE_DIM)
    for i in range(depth):
        m, j = i % n_mixers, i // n_mixers
        g = mix_norm[i]
        if m == 0:
            w = ssd_w_in[j]
            proj = norm_matmul(h, g, w[:, :SSD_MAIN].astype(BF16), BF16)
            dt_raw = norm_matmul(h, g, _pad_cols(w[:, SSD_MAIN:], LANE).astype(BF16), F32)
            y = ssd_core(proj.reshape(B, S, SSD_MAIN), dt_raw.reshape(B, S, LANE), ssd_conv_w[j],
                         ssd_conv_b[j], ssd_dt_bias[j], ssd_a_log[j], ssd_d[j], ssd_norm[j])
            w_out = ssd_w_out[j]
        elif m == 1:
            proj = norm_matmul(h, g, ret_w_in[j].astype(BF16), BF16)
            y = retention_core(proj.reshape(B, S, -1), ret_norm[j])
            w_out = ret_w_out[j]
        elif m == 2:
            lam_init = 0.8 - 0.6 * math.exp(-0.3 * i)
            proj = norm_matmul(h, g, diff_w_in[j].astype(BF16), BF16)
            y = diff_core(proj.reshape(B, S, -1), diff_lambda[j], diff_norm[j], rel_bias, lam_init)
            w_out = diff_w_out[j]
        else:
            w = fox_w_in[j]
            proj = norm_matmul(h, g, w[:, :3 * D].astype(BF16), BF16)
            c_t, c_rows = fox_gate(h.reshape(B, S, D), g, w[:, 3 * D:], fox_b_f[j])
            y = fox_core(proj.reshape(B, S, -1), c_t, c_rows)
            w_out = fox_w_out[j]
        h = matmul_residual(y.reshape(T, -1), w_out.astype(BF16), h)
        y = peer_layer(h, ffn_norm[i], peer_w_q[i], peer_keys[i], peer_u[i], peer_v[i])
        h = ple_layer(h, y, ple_norm[i], ple_gate[i].astype(BF16), p_all, i, ple_proj[i].astype(BF16),
                      final_norm, final=(i == depth - 1))
    return h.reshape(B, S, D)
```

```python
import functools
import math

import jax
import jax.numpy as jnp
import numpy as np
from jax import lax
from jax.experimental import pallas as pl
from jax.experimental.pallas import tpu as pltpu

F32 = jnp.float32
BF16 = jnp.bfloat16

D_MODEL = 1024
CHUNK = 64
NORM_EPS = 1e-6
LOG2E = math.log2(math.e)
ROPE_BASE = 10000.0
PLE_DIM = 256

SSD_D_INNER = 2 * D_MODEL
SSD_HEAD_DIM = 64
SSD_HEADS = SSD_D_INNER // SSD_HEAD_DIM
SSD_GROUPS = 4
SSD_HEADS_PER_GROUP = SSD_HEADS // SSD_GROUPS
SSD_STATE = 128
SSD_CONV = 4
SSD_CONV_DIM = SSD_D_INNER + 2 * SSD_GROUPS * SSD_STATE
SSD_MAIN = SSD_D_INNER + SSD_CONV_DIM
SSD_BLOCK = 128

RET_HEADS = 4
RET_QK_DIM = D_MODEL // RET_HEADS
RET_V_DIM = 2 * RET_QK_DIM
RET_V_WIDTH = RET_HEADS * RET_V_DIM
RET_BLOCK = 256

DIFF_HEADS = 8
DIFF_HEAD_DIM = D_MODEL // DIFF_HEADS // 2
FOX_HEADS = 16
FOX_HEAD_DIM = D_MODEL // FOX_HEADS
ATTN_TILE = 512
ATTN_QUERY_TILES = 2
HEAD_LANES = 128

REL_BUCKETS = 32
REL_MAX_DIST = 128

PEER_KEYS = 128
PEER_EXPERTS = PEER_KEYS * PEER_KEYS
PEER_HEADS = 8
PEER_TOPK = 16
PEER_QUERY_DIM = 256
PEER_HALF = PEER_QUERY_DIM // 2
PEER_TOKENS = 1024
PEER_TOKEN_CHUNK = 512
PEER_EXPERT_TILE = 1024
PEER_SELECT_TOKENS = 1024
PEER_DENSE_VMEM = 56 * 1024 * 1024

LANE = 128
SUBLANE = 8
VMEM_LIMIT = 48 * 1024 * 1024

_NT = (((1,), (1,)), ((), ()))


def _params(*sem):
    return pltpu.CompilerParams(dimension_semantics=sem, vmem_limit_bytes=VMEM_LIMIT)


def _rms(x, g):
    return x * lax.rsqrt(jnp.mean(x * x, axis=-1, keepdims=True) + NORM_EPS) * g


def _split3(x):
    hi = x.astype(BF16)
    r1 = x - hi.astype(F32)
    mid = r1.astype(BF16)
    lo = (r1 - mid.astype(F32)).astype(BF16)
    return hi, mid, lo


def _dot_exact_lhs01(m01, x):
    hi, mid, lo = _split3(x)
    d = functools.partial(jnp.dot, preferred_element_type=F32)
    return (d(m01, hi) + d(m01, mid)) + d(m01, lo)


def _norm_matmul_kernel(h_ref, g_ref, w_ref, o_ref, *rest, emit_t):
    if emit_t:
        xt_ref, xn_ref = rest
    else:
        (xn_ref,) = rest

    @pl.when(pl.program_id(1) == 0)
    def _():
        y = _rms(h_ref[...], g_ref[...])
        xn_ref[...] = y.astype(BF16)
        if emit_t:
            tc = xt_ref.shape[2]
            for c in range(xt_ref.shape[0]):
                xt_ref[c] = y[c * tc:(c + 1) * tc, :].T.astype(BF16)

    o_ref[...] = jnp.dot(xn_ref[...], w_ref[...], preferred_element_type=F32).astype(o_ref.dtype)


def norm_matmul(h, g, w, out_dtype, emit_t=False, tm=1024, tn=1024):
    T, D = h.shape
    N = w.shape[1]
    tn = min(tn, N)
    assert T % tm == 0 and N % tn == 0
    out_shape = [jax.ShapeDtypeStruct((T, N), out_dtype)]
    out_specs = [pl.BlockSpec((tm, tn), lambda i, j: (i, j))]
    if emit_t:
        tc = PEER_TOKEN_CHUNK
        out_shape.append(jax.ShapeDtypeStruct((T // tc, D, tc), BF16))
        out_specs.append(pl.BlockSpec((tm // tc, D, tc), lambda i, j: (i, 0, 0)))
    res = pl.pallas_call(
        functools.partial(_norm_matmul_kernel, emit_t=emit_t),
        grid=(T // tm, N // tn),
        in_specs=[pl.BlockSpec((tm, D), lambda i, j: (i, 0)),
                  pl.BlockSpec((1, D), lambda i, j: (0, 0)),
                  pl.BlockSpec((D, tn), lambda i, j: (0, j))],
        out_specs=out_specs,
        out_shape=out_shape,
        scratch_shapes=[pltpu.VMEM((tm, D), BF16)],
        compiler_params=_params("parallel", "arbitrary"),
        name="norm_matmul",
    )(h, g.reshape(1, D), w)
    return res if emit_t else res[0]


def _matmul_residual_kernel(a_ref, w_ref, h_ref, o_ref):
    o_ref[...] = h_ref[...] + jnp.dot(a_ref[...], w_ref[...], preferred_element_type=F32)


def matmul_residual(a, w, h, tm=512):
    T, K = a.shape
    N = w.shape[1]
    return pl.pallas_call(
        _matmul_residual_kernel,
        grid=(T // tm,),
        in_specs=[pl.BlockSpec((tm, K), lambda i: (i, 0)),
                  pl.BlockSpec((K, N), lambda i: (0, 0)),
                  pl.BlockSpec((tm, N), lambda i: (i, 0))],
        out_specs=pl.BlockSpec((tm, N), lambda i: (i, 0)),
        out_shape=jax.ShapeDtypeStruct((T, N), F32),
        input_output_aliases={2: 0},
        compiler_params=_params("parallel"),
        name="matmul_residual",
    )(a, w, h)


def _ple_kernel(h_ref, y_ref, g_ref, wg_ref, p_ref, wp_ref, fg_ref, o_ref, *, final):
    x = h_ref[...] + y_ref[...]
    xn = _rms(x, g_ref[...]).astype(BF16)
    gate = jax.nn.sigmoid(jnp.dot(xn, wg_ref[...], preferred_element_type=F32))
    proj = jnp.dot(p_ref[...].astype(BF16), wp_ref[...], preferred_element_type=F32)
    y = x + gate * proj
    if final:
        y = _rms(y, fg_ref[...])
    o_ref[...] = y


def ple_layer(h, y, g, w_gate, p_all, layer, w_proj, final_g, final, tm=512):
    T, D = h.shape
    return pl.pallas_call(
        functools.partial(_ple_kernel, final=final),
        grid=(T // tm,),
        in_specs=[pl.BlockSpec((tm, D), lambda i: (i, 0)),
                  pl.BlockSpec((tm, D), lambda i: (i, 0)),
                  pl.BlockSpec((1, D), lambda i: (0, 0)),
                  pl.BlockSpec((D, D), lambda i: (0, 0)),
                  pl.BlockSpec((None, tm, PLE_DIM), lambda i: (layer, i, 0)),
                  pl.BlockSpec((PLE_DIM, D), lambda i: (0, 0)),
                  pl.BlockSpec((1, D), lambda i: (0, 0))],
        out_specs=pl.BlockSpec((tm, D), lambda i: (i, 0)),
        out_shape=jax.ShapeDtypeStruct((T, D), F32),
        input_output_aliases={0: 0},
        compiler_params=_params("parallel"),
        name="ple_layer",
    )(h, y, g.reshape(1, D), w_gate, p_all, w_proj, final_g.reshape(1, D))


def _ssd_kernel(proj_ref, dt_ref, cw_ref, cb_ref, dtb_ref, alog_ref, dsk_ref, ng_ref, o_ref,
                xbuf, state, ybuf, *, L):
    DI, P, N, R = SSD_D_INNER, SSD_HEAD_DIM, SSD_STATE, SSD_HEADS_PER_GROUP
    GW = DI // SSD_GROUPS

    @pl.when(pl.program_id(1) == 0)
    def _():
        xbuf[0:8, :] = jnp.zeros((8, SSD_CONV_DIM), F32)
        state[...] = jnp.zeros(state.shape, F32)

    xbc = proj_ref[0, :, DI:].astype(F32)
    xbuf[8:8 + L, :] = xbc
    conv = cb_ref[...] + cw_ref[3:4, :] * xbc
    for j in range(1, SSD_CONV):
        conv = conv + cw_ref[SSD_CONV - 1 - j:SSD_CONV - j, :] * xbuf[8 - j:8 - j + L, :]
    xbuf[0:8, :] = xbuf[L:L + 8, :]
    act = conv * jax.nn.sigmoid(conv)
    xs = act[:, :DI]
    bm = act[:, DI:DI + SSD_GROUPS * N]
    cm = act[:, DI + SSD_GROUPS * N:]

    dt = jax.nn.softplus(dt_ref[0] + dtb_ref[...])
    a = dt * (-jnp.exp(alog_ref[...]))
    row = lax.broadcasted_iota(jnp.int32, (L, L), 0)
    col = lax.broadcasted_iota(jnp.int32, (L, L), 1)
    tril = row >= col
    cum = _dot_exact_lhs01(tril.astype(BF16), a)
    cum_t = cum.T
    dt_t = dt.T
    last = cum[L - 1:L, :]
    wst = jnp.exp(last - cum) * dt
    ecum = jnp.exp(cum)
    elast = jnp.exp(last)

    for g in range(SSD_GROUPS):
        bg = bm[:, g * N:(g + 1) * N]
        cg = cm[:, g * N:(g + 1) * N].astype(BF16)
        cb = lax.dot_general(cg, bg.astype(BF16), _NT, preferred_element_type=F32)
        bg_t = bg.T.astype(BF16)
        for r in range(R):
            hd = g * R + r
            xh = xs[:, hd * P:(hd + 1) * P]
            seg = cum[:, hd:hd + 1] - cum_t[hd:hd + 1, :]
            w = jnp.exp(jnp.where(tril, seg, -jnp.inf)) * cb * dt_t[hd:hd + 1, :]
            st = state[g, :, r * P:(r + 1) * P]
            y = jnp.dot(w.astype(BF16), xh.astype(BF16), preferred_element_type=F32)
            y = y + jnp.dot(cg, st.astype(BF16), preferred_element_type=F32) * ecum[:, hd:hd + 1]
            xw = (xh * wst[:, hd:hd + 1]).astype(BF16)
            state[g, :, r * P:(r + 1) * P] = (st * elast[:, hd:hd + 1]
                                              + jnp.dot(bg_t, xw, preferred_element_type=F32))
            ybuf[:, hd * P:(hd + 1) * P] = y

    y = ybuf[...] + dsk_ref[...] * xs
    z = proj_ref[0, :, :DI].astype(F32)
    y = y * (z * jax.nn.sigmoid(z))
    for g in range(SSD_GROUPS):
        yg = y[:, g * GW:(g + 1) * GW]
        o_ref[0, :, g * GW:(g + 1) * GW] = _rms(yg, ng_ref[:, g * GW:(g + 1) * GW]).astype(o_ref.dtype)


def ssd_core(proj, dt_raw, conv_w, conv_b, dt_bias, a_log, d_skip, norm_g):
    B, S, _ = proj.shape
    L = SSD_BLOCK
    assert S % L == 0
    pad = LANE - SSD_HEADS
    vec = lambda n: pl.BlockSpec((1, n), lambda b, c: (0, 0))
    return pl.pallas_call(
        functools.partial(_ssd_kernel, L=L),
        grid=(B, S // L),
        in_specs=[pl.BlockSpec((1, L, SSD_MAIN), lambda b, c: (b, c, 0)),
                  pl.BlockSpec((1, L, LANE), lambda b, c: (b, c, 0)),
                  pl.BlockSpec((SSD_CONV, SSD_CONV_DIM), lambda b, c: (0, 0)),
                  vec(SSD_CONV_DIM), vec(LANE), vec(LANE), vec(SSD_D_INNER), vec(SSD_D_INNER)],
        out_specs=pl.BlockSpec((1, L, SSD_D_INNER), lambda b, c: (b, c, 0)),
        out_shape=jax.ShapeDtypeStruct((B, S, SSD_D_INNER), BF16),
        scratch_shapes=[pltpu.VMEM((L + 8, SSD_CONV_DIM), F32),
                        pltpu.VMEM((SSD_GROUPS, SSD_STATE, SSD_D_INNER // SSD_GROUPS), F32),
                        pltpu.VMEM((L, SSD_D_INNER), F32)],
        compiler_params=_params("parallel", "arbitrary"),
        name="ssd_core",
    )(proj, dt_raw,
      conv_w.reshape(SSD_CONV, SSD_CONV_DIM), conv_b.reshape(1, SSD_CONV_DIM),
      jnp.pad(dt_bias, (0, pad)).reshape(1, LANE), jnp.pad(a_log, (0, pad)).reshape(1, LANE),
      jnp.repeat(d_skip, SSD_HEAD_DIM).reshape(1, SSD_D_INNER), norm_g.reshape(1, SSD_D_INNER))


def _retention_kernel(q_ref, k_ref, v_ref, gate_ref, cos_ref, sin_ref, dec_ref, qd_ref, kd_ref,
                      cd_ref, ng_ref, o_ref, state):
    half = RET_QK_DIM // 2

    @pl.when(pl.program_id(2) == 0)
    def _():
        state[...] = jnp.zeros(state.shape, F32)

    cos = cos_ref[...]
    sin = sin_ref[...]

    def rot(x):
        x1, x2 = x[:, :half], x[:, half:]
        return jnp.concatenate([x1 * cos - x2 * sin, x1 * sin + x2 * cos], axis=-1)

    q = rot(q_ref[0].astype(F32))
    k = rot(k_ref[0].astype(F32)) * (RET_QK_DIM ** -0.5)
    v = v_ref[0]
    qb = q.astype(BF16)
    s = lax.dot_general(qb, k.astype(BF16), _NT, preferred_element_type=F32) * dec_ref[0]
    o = jnp.dot(s.astype(BF16), v, preferred_element_type=F32)
    st = state[...]
    o = o + jnp.dot(qb, st.astype(BF16), preferred_element_type=F32) * qd_ref[0]
    kt = (k * kd_ref[0]).T.astype(BF16)
    state[...] = st * cd_ref[0] + jnp.dot(kt, v, preferred_element_type=F32)
    gate = gate_ref[0].astype(F32)
    o_ref[0] = (_rms(o, ng_ref[0]) * (gate * jax.nn.sigmoid(gate))).astype(o_ref.dtype)


def retention_core(proj, norm_g):
    B, S, _ = proj.shape
    H, dk, dv, L = RET_HEADS, RET_QK_DIM, RET_V_DIM, RET_BLOCK
    assert S % L == 0 and L % CHUNK == 0
    inv = 1.0 / (ROPE_BASE ** (jnp.arange(0, dk, 2, dtype=F32) / dk))
    ang = jnp.arange(S, dtype=F32)[:, None] * inv[None, :]
    log_gamma = jnp.log1p(-jnp.exp2(-5.0 - jnp.arange(H, dtype=F32)))
    idx = jnp.arange(L, dtype=F32)
    visible = (jnp.arange(L)[None, :] // CHUNK) <= (jnp.arange(L)[:, None] // CHUNK)
    decay = jnp.where(visible[None],
                      jnp.exp(log_gamma[:, None, None] * jnp.abs(idx[:, None] - idx[None, :])), 0.0)
    q_decay = jnp.exp(log_gamma[:, None] * (idx[None, :] + 1.0))[..., None]
    k_decay = jnp.exp(log_gamma[:, None] * (L - 1.0 - idx[None, :]))[..., None]
    block_decay = jnp.exp(log_gamma * L).reshape(H, 1, 1)
    return pl.pallas_call(
        _retention_kernel,
        grid=(B, H, S // L),
        in_specs=[pl.BlockSpec((1, L, dk), lambda b, h, c: (b, c, h)),
                  pl.BlockSpec((1, L, dk), lambda b, h, c: (b, c, H + h)),
                  pl.BlockSpec((1, L, dv), lambda b, h, c: (b, c, H + h)),
                  pl.BlockSpec((1, L, dv), lambda b, h, c: (b, c, 2 * H + h)),
                  pl.BlockSpec((L, dk // 2), lambda b, h, c: (c, 0)),
                  pl.BlockSpec((L, dk // 2), lambda b, h, c: (c, 0)),
                  pl.BlockSpec((1, L, L), lambda b, h, c: (h, 0, 0)),
                  pl.BlockSpec((1, L, 1), lambda b, h, c: (h, 0, 0)),
                  pl.BlockSpec((1, L, 1), lambda b, h, c: (h, 0, 0)),
                  pl.BlockSpec((1, 1, 1), lambda b, h, c: (h, 0, 0)),
                  pl.BlockSpec((1, 1, dv), lambda b, h, c: (h, 0, 0))],
        out_specs=pl.BlockSpec((1, L, dv), lambda b, h, c: (b, c, h)),
        out_shape=jax.ShapeDtypeStruct((B, S, RET_V_WIDTH), BF16),
        scratch_shapes=[pltpu.VMEM((dk, dv), F32)],
        compiler_params=_params("parallel", "parallel", "arbitrary"),
        name="retention_core",
    )(proj, proj, proj, proj, jnp.cos(ang), jnp.sin(ang), decay, q_decay, k_decay, block_decay,
      norm_g.reshape(H, 1, dv))


def _attn_kernel(*refs, mode, T):
    if mode == "diff":
        q_ref, k_ref, v_ref, bias_ref, lam_ref, ng_ref, o_ref, m_s, acc_s, l_s = refs
    else:
        q_ref, k_ref, v_ref, ck_ref, cq_ref, o_ref, m_s, acc_s, cq_s = refs
    i = pl.program_id(2)
    NQ = ATTN_QUERY_TILES
    half = HEAD_LANES // 2
    reps = T // HEAD_LANES
    lane = lax.broadcasted_iota(jnp.int32, (1, HEAD_LANES), 1)
    first = lane < half
    second = jnp.logical_not(first)
    qs = []
    for u in range(NQ):
        q = (q_ref[0, u * T:(u + 1) * T, :].astype(F32) * (half ** -0.5 * LOG2E)).astype(BF16)
        zero = jnp.zeros_like(q)
        qs.append((jnp.where(first, q, zero), jnp.where(first, zero, q)))

    m_s[...] = jnp.full(m_s.shape, -jnp.inf, F32)
    acc_s[...] = jnp.zeros(acc_s.shape, F32)
    if mode == "diff":
        l_s[...] = jnp.zeros(l_s.shape, F32)
    else:
        for u in range(NQ):
            for a in range(2):
                cq_s[u, a] = jnp.broadcast_to(cq_ref[0, 0, u * T:(u + 1) * T, a:a + 1], (T, T))

    def step(j, subtiles, diagonal):
        start = pl.multiple_of(j * T, T)
        k = k_ref[0, pl.ds(start, T), :]
        v = v_ref[0, pl.ds(start, T), :]
        if mode == "diff":
            vs = (v, v)
        else:
            ones = jnp.ones_like(v)
            vs = (jnp.where(first, v, ones), jnp.where(second, v, ones))
        if diagonal is not None:
            row = lax.broadcasted_iota(jnp.int32, (T, T), 0)
            col = lax.broadcasted_iota(jnp.int32, (T, T), 1)
            if mode == "diff":
                visible = (col // CHUNK) <= (row // CHUNK)
            else:
                visible = col <= row
        for u in subtiles:
            for a in range(2):
                s = lax.dot_general(qs[u][a], k, _NT, preferred_element_type=F32)
                if mode == "diff":
                    s = s + bias_ref[0, jnp.minimum(NQ * i + u - j, 2)]
                else:
                    s = (s - ck_ref[0, 0, j, a:a + 1, :]) + cq_s[u, a]
                if u == diagonal:
                    s = jnp.where(visible, s, -jnp.inf)
                m_prev = m_s[u, a]
                m_new = jnp.maximum(m_prev, jnp.max(s, axis=-1, keepdims=True))
                alpha = jnp.exp2(m_prev - m_new)
                p = jnp.exp2(s - jnp.concatenate([m_new] * reps, axis=1))
                if mode == "diff":
                    l_s[u, a] = alpha * l_s[u, a] + jnp.sum(p, axis=-1, keepdims=True)
                acc_s[u, a] = alpha * acc_s[u, a] + jnp.dot(p.astype(BF16), vs[a], preferred_element_type=F32)
                m_s[u, a] = m_new

    everyone = tuple(range(NQ))
    lax.fori_loop(0, NQ * i, lambda j, c: (step(j, everyone, None), c)[1], 0)
    for d in range(NQ):
        step(NQ * i + d, everyone[d:], d)

    for u in range(NQ):
        if mode == "diff":
            o = acc_s[u, 0] / l_s[u, 0] - lam_ref[0] * (acc_s[u, 1] / l_s[u, 1])
            o = _rms(o, ng_ref[...])
        else:
            o0, o1 = acc_s[u, 0], acc_s[u, 1]
            o = jnp.where(first, o0 / pltpu.roll(o0, half, 1), o1 / pltpu.roll(o1, half, 1))
        o_ref[0, u * T:(u + 1) * T, :] = o.astype(o_ref.dtype)


def _attn_call(mode, proj, n_blocks, extra_inputs, extra_specs):
    B, S, _ = proj.shape
    T, NQ = ATTN_TILE, ATTN_QUERY_TILES
    assert S % (NQ * T) == 0
    stats = pltpu.VMEM((NQ, 2, T, HEAD_LANES), F32)
    return pl.pallas_call(
        functools.partial(_attn_kernel, mode=mode, T=T),
        grid=(B, n_blocks, S // (NQ * T)),
        in_specs=[pl.BlockSpec((1, NQ * T, HEAD_LANES), lambda b, h, i: (b, i, h)),
                  pl.BlockSpec((1, S, HEAD_LANES), lambda b, h, i: (b, 0, n_blocks + h)),
                  pl.BlockSpec((1, S, HEAD_LANES), lambda b, h, i: (b, 0, 2 * n_blocks + h))] + extra_specs,
        out_specs=pl.BlockSpec((1, NQ * T, HEAD_LANES), lambda b, h, i: (b, i, h)),
        out_shape=jax.ShapeDtypeStruct((B, S, D_MODEL), BF16),
        scratch_shapes=[stats, stats, stats if mode == "diff" else pltpu.VMEM((NQ, 2, T, T), F32)],
        compiler_params=_params("parallel", "parallel", "arbitrary"),
        name=mode + "_attention",
    )(proj, proj, proj, *extra_inputs)


def _t5_bucket(rel):
    nb = REL_BUCKETS // 2
    max_exact = nb // 2
    ret = (rel > 0).astype(jnp.int32) * nb
    n = jnp.abs(rel)
    nf = jnp.maximum(n, 1).astype(F32)
    large = max_exact + (jnp.log(nf / max_exact) / math.log(REL_MAX_DIST / max_exact)
                         * (nb - max_exact)).astype(jnp.int32)
    large = jnp.minimum(large, nb - 1)
    return ret + jnp.where(n < max_exact, n, large)


def _bias_tiles_kernel(bucket_ref, table_ref, o_ref):
    h = pl.program_id(0)
    for d in range(bucket_ref.shape[0]):
        bucket = bucket_ref[d]
        tile = jnp.zeros(bucket.shape, F32)
        for b in range(REL_BUCKETS):
            tile = jnp.where(bucket == b, table_ref[b * DIFF_HEADS + h] * LOG2E, tile)
        o_ref[0, d] = tile


def diff_bias_tiles(rel_bias):
    T = ATTN_TILE
    assert T >= REL_MAX_DIST
    off = jnp.arange(T)
    rel = (off[None, None, :] - off[None, :, None]) - (jnp.arange(3) * T)[:, None, None]
    return pl.pallas_call(
        _bias_tiles_kernel,
        grid=(DIFF_HEADS,),
        in_specs=[pl.BlockSpec((3, T, T), lambda h: (0, 0, 0)),
                  pl.BlockSpec(memory_space=pltpu.SMEM)],
        out_specs=pl.BlockSpec((1, 3, T, T), lambda h: (h, 0, 0, 0)),
        out_shape=jax.ShapeDtypeStruct((DIFF_HEADS, 3, T, T), F32),
        compiler_params=_params("parallel"),
        name="diff_bias_tiles",
    )(_t5_bucket(rel), rel_bias.astype(F32).reshape(REL_BUCKETS * DIFF_HEADS))


def diff_core(proj, lam_vecs, norm_g, rel_bias, lam_init):
    T = ATTN_TILE
    lv = lam_vecs.astype(F32)
    lam = jnp.exp(jnp.sum(lv[0] * lv[1])) - jnp.exp(jnp.sum(lv[2] * lv[3])) + lam_init
    bias = diff_bias_tiles(rel_bias)
    g = (norm_g * (1.0 - lam_init)).reshape(1, HEAD_LANES)
    extra_specs = [pl.BlockSpec((1, 3, T, T), lambda b, h, i: (h, 0, 0, 0)),
                   pl.BlockSpec(memory_space=pltpu.SMEM),
                   pl.BlockSpec((1, HEAD_LANES), lambda b, h, i: (0, 0))]
    return _attn_call("diff", proj, DIFF_HEADS, [bias, lam.reshape(1), g], extra_specs)


def _fox_gate_kernel(h_ref, g_ref, w_ref, b_ref, ct_ref, c_ref, carry, *, L):
    @pl.when(pl.program_id(1) == 0)
    def _():
        carry[...] = jnp.zeros(carry.shape, F32)

    xn = _rms(h_ref[0], g_ref[...])
    logits = lax.dot_general(w_ref[...], xn.astype(BF16), _NT, preferred_element_type=F32)
    log_f = jax.nn.log_sigmoid(logits + b_ref[...])
    row = lax.broadcasted_iota(jnp.int32, (L, L), 0)
    col = lax.broadcasted_iota(jnp.int32, (L, L), 1)
    triu = (row <= col).astype(BF16)
    hi, mid, lo = _split3(log_f)
    d = functools.partial(jnp.dot, preferred_element_type=F32)
    c = carry[...] + ((d(hi, triu) + d(mid, triu)) + d(lo, triu))
    carry[...] = c[:, L - 1:L]
    c2 = c * LOG2E
    ct_ref[0] = c2
    c_ref[0] = c2.T


def fox_gate(h3, g, w_f, b_f):
    B, S, D = h3.shape
    L = 512
    pad = LANE - FOX_HEADS
    return pl.pallas_call(
        functools.partial(_fox_gate_kernel, L=L),
        grid=(B, S // L),
        in_specs=[pl.BlockSpec((1, L, D), lambda b, c: (b, c, 0)),
                  pl.BlockSpec((1, D), lambda b, c: (0, 0)),
                  pl.BlockSpec((LANE, D), lambda b, c: (0, 0)),
                  pl.BlockSpec((LANE, 1), lambda b, c: (0, 0))],
        out_specs=[pl.BlockSpec((1, LANE, L), lambda b, c: (b, 0, c)),
                   pl.BlockSpec((1, L, LANE), lambda b, c: (b, c, 0))],
        out_shape=[jax.ShapeDtypeStruct((B, LANE, S), F32), jax.ShapeDtypeStruct((B, S, LANE), F32)],
        scratch_shapes=[pltpu.VMEM((LANE, 1), F32)],
        compiler_params=_params("parallel", "arbitrary"),
        name="fox_gate",
    )(h3, g.reshape(1, D), jnp.pad(w_f.T, ((0, pad), (0, 0))).astype(BF16),
      jnp.pad(b_f, (0, pad)).reshape(LANE, 1))


def fox_core(proj, c_t, c_rows):
    B, S, _ = proj.shape
    T = ATTN_TILE
    nb = FOX_HEADS // 2
    ck = jnp.transpose(c_t[:, :FOX_HEADS].reshape(B, nb, 2, S // T, T), (0, 1, 3, 2, 4))
    cq = jnp.transpose(c_rows[:, :, :FOX_HEADS].reshape(B, S, nb, 2), (0, 2, 1, 3))
    extra_specs = [pl.BlockSpec((1, 1, S // T, 2, T), lambda b, h, i: (b, h, 0, 0, 0)),
                   pl.BlockSpec((1, 1, ATTN_QUERY_TILES * T, 2), lambda b, h, i: (b, h, i, 0))]
    return _attn_call("fox", proj, nb, [ck, cq], extra_specs)


def _candidate_tables():
    pairs = [(a, b) for a in range(PEER_TOPK) for b in range(PEER_TOPK) if (a + 1) * (b + 1) <= PEER_TOPK]
    rows = 64
    p1 = np.zeros((rows, LANE), np.float32)
    p2 = np.zeros((rows, LANE), np.float32)
    for r, (a, b) in enumerate(pairs):
        p1[r, a] = 1.0
        p2[r, b] = 1.0
    return len(pairs), p1, p2


N_CAND, _P1, _P2 = _candidate_tables()


_INT_MIN = -2 ** 31
_FLIP = 0x7FFFFFFF


def _ordered_int(x):
    b = lax.bitcast_convert_type(x + 0.0, jnp.int32)
    return jnp.where(b < 0, b ^ _FLIP, b)


def _ordered_float(k):
    return lax.bitcast_convert_type(jnp.where(k < 0, k ^ _FLIP, k), F32)


def _pop_top16(keys, break_ties):
    n, t = keys.shape
    idx = lax.broadcasted_iota(jnp.int32, (n, t), 0)
    ridx = lax.broadcasted_iota(jnp.int32, (PEER_TOPK, t), 0)
    vals = jnp.zeros((PEER_TOPK, t), jnp.int32)
    work = keys
    for r in range(PEER_TOPK):
        mx = jnp.max(work, axis=0, keepdims=True)
        hit = work == mx
        if break_ties:
            pos = jnp.min(jnp.where(hit, idx, n), axis=0, keepdims=True)
            hit = idx == pos
        work = jnp.where(hit, _INT_MIN + r, work)
        vals = jnp.where(ridx == r, mx, vals)
    return work, vals


def _top16_ranks(s, break_ties):
    work, vals = _pop_top16(_ordered_int(s), break_ties)
    rank = jnp.where(work < _INT_MIN + PEER_TOPK, work & (2 * PEER_TOPK - 1), PEER_TOPK)
    return rank, _ordered_float(vals)


def _peer_select_kernel(q_ref, keys_ref, p1_ref, p2_ref, p1t_ref, rank2_ref, e2_ref, nrow_ref, e1_ref):
    def scores(c):
        kk = keys_ref[0, c]
        qq = q_ref[:, c * PEER_HALF:(c + 1) * PEER_HALF]
        kh, km, _ = _split3(kk)
        qh, qm, _ = _split3(qq)
        d = lambda a, b: lax.dot_general(a, b, _NT, preferred_element_type=F32)
        return d(kh, qh) + (d(kh, qm) + d(km, qh))

    s1 = scores(0)
    s2 = scores(1)
    out_refs = (rank2_ref, e2_ref, nrow_ref, e1_ref)
    clean = _peer_select_pass(s1, s2, p1_ref, p2_ref, p1t_ref, out_refs, break_ties=False)

    @pl.when(jnp.logical_not(clean))
    def _():
        _peer_select_pass(s1, s2, p1_ref, p2_ref, p1t_ref, out_refs, break_ties=True)


def _peer_select_pass(s1, s2, p1_ref, p2_ref, p1t_ref, out_refs, break_ties):
    rank2_ref, e2_ref, nrow_ref, e1_ref = out_refs
    rank1, v1 = _top16_ranks(s1, break_ties)
    rank2, v2 = _top16_ranks(s2, break_ties)

    tokens = s1.shape[1]
    pad = jnp.zeros((LANE - PEER_TOPK, tokens), F32)
    cand = (_dot_exact_lhs01(p1_ref[...], jnp.concatenate([v1, pad], axis=0))
            + _dot_exact_lhs01(p2_ref[...], jnp.concatenate([v2, pad], axis=0)))
    cidx = lax.broadcasted_iota(jnp.int32, cand.shape, 0)
    cand = jnp.where(cidx < N_CAND, cand, -jnp.inf)
    popped, _ = _pop_top16(_ordered_int(cand), break_ties)
    sel = jnp.where(popped < _INT_MIN + PEER_TOPK, 1.0, 0.0)
    top = v1[0:1, :] + v2[0:1, :]
    z = jnp.sum(sel * jnp.exp(jnp.where(sel > 0.0, cand - top, 0.0)), axis=0, keepdims=True)
    sel_pad = jnp.concatenate([sel, jnp.zeros((LANE - sel.shape[0], tokens), F32)], axis=0).astype(BF16)
    n_by_rank = jnp.dot(p1t_ref[...], sel_pad, preferred_element_type=F32)
    nrow = jnp.zeros(s1.shape, F32)
    for a in range(PEER_TOPK):
        nrow = jnp.where(rank1 == a, n_by_rank[a:a + 1, :], nrow)

    rank2_ref[0] = rank2.astype(F32).astype(BF16)
    e2_ref[0] = jnp.exp(s2 - v2[0:1, :]).astype(BF16)
    e1 = jnp.exp(s1 - v1[0:1, :]) * (0.5 / z)
    for g in range(PEER_KEYS // SUBLANE):
        for st in range(tokens // LANE):
            tile = (slice(g * SUBLANE, (g + 1) * SUBLANE), slice(st * LANE, (st + 1) * LANE))
            nrow_ref[0, g, st] = nrow[tile]
            e1_ref[0, g, st] = e1[tile]

    if break_ties:
        return None
    ranked = (jnp.sum(jnp.where(rank1 < PEER_TOPK, 1.0, 0.0), axis=0, keepdims=True)
              + jnp.sum(jnp.where(rank2 < PEER_TOPK, 1.0, 0.0), axis=0, keepdims=True)
              + jnp.sum(sel, axis=0, keepdims=True))
    return jnp.max(ranked) == 3.0 * PEER_TOPK


def peer_select(q, keys):
    T = q.shape[0]
    tm = PEER_SELECT_TOKENS
    assert T % tm == 0
    hk = pl.BlockSpec((1, PEER_KEYS, tm), lambda i, h: (h, 0, i))
    groups = PEER_KEYS // SUBLANE
    hk1 = pl.BlockSpec((1, groups, tm // LANE, SUBLANE, LANE), lambda i, h: (h, 0, i, 0, 0))
    shp = lambda dt: jax.ShapeDtypeStruct((PEER_HEADS, PEER_KEYS, T), dt)
    shp1 = jax.ShapeDtypeStruct((PEER_HEADS, groups, T // LANE, SUBLANE, LANE), F32)
    cst = lambda a: pl.BlockSpec(a.shape, lambda i, h: (0, 0))
    p1, p2 = jnp.asarray(_P1, BF16), jnp.asarray(_P2, BF16)
    p1t = jnp.asarray(np.pad(_P1.T, ((0, 0), (0, LANE - _P1.shape[0]))), BF16)
    return pl.pallas_call(
        _peer_select_kernel,
        grid=(T // tm, PEER_HEADS),
        in_specs=[pl.BlockSpec((tm, PEER_QUERY_DIM), lambda i, h: (i, h)),
                  pl.BlockSpec((1, 2, PEER_KEYS, PEER_HALF), lambda i, h: (h, 0, 0, 0)),
                  cst(p1), cst(p2), cst(p1t)],
        out_specs=[hk, hk, hk1, hk1],
        out_shape=[shp(BF16), shp(BF16), shp1, shp1],
        compiler_params=_params("parallel", "parallel"),
        name="peer_select",
    )(q, keys, p1, p2, p1t)


def _peer_dense_kernel(xt_ref, u_ref, u_next_ref, vt_ref, rank2_ref, e2_ref, nrow_ref, e1_ref,
                       nrow_next_ref, e1_next_ref, o_ref, *scratch):
    n = xt_ref.shape[0]
    acc, gbuf, hbuf, wbuf = scratch[:n], scratch[n:2 * n], scratch[2 * n:2 * n + 2], scratch[2 * n + 2:]
    j = pl.program_id(1)
    rows = PEER_EXPERT_TILE // PEER_KEYS
    tc = PEER_TOKEN_CHUNK
    n_chunks = xt_ref.shape[0]
    pack = 16

    def first_matmul(c, u):
        hbuf[c % 2][...] = jnp.dot(u[...], xt_ref[c], preferred_element_type=F32).astype(BF16)

    def gate_weights(c, nrow, e1):
        cols = slice(c * tc, (c + 1) * tc)
        for ii in range(rows):
            w = jnp.zeros((PEER_KEYS, tc), BF16)
            for h in range(PEER_HEADS):
                def spread(ref):
                    tiles = [jnp.broadcast_to(ref[h, 0, c * (tc // LANE) + lt, ii:ii + 1, :],
                                              (pack, LANE)).astype(BF16) for lt in range(tc // LANE)]
                    return jnp.concatenate([jnp.concatenate(tiles, axis=1)] * (PEER_KEYS // pack), axis=0)
                w = w + jnp.where(rank2_ref[h, :, cols] < spread(nrow), e2_ref[h, :, cols] * spread(e1),
                                  jnp.zeros_like(w))
            wbuf[c % 2][ii * PEER_KEYS:(ii + 1) * PEER_KEYS, :] = w

    def finish(c):
        for ii in range(rows):
            r = slice(ii * PEER_KEYS, (ii + 1) * PEER_KEYS)
            hr = hbuf[c % 2][r, :]
            act = hr * (1.0 + lax.erf(hr * jnp.asarray(2.0 ** -0.5, BF16)))
            gbuf[c][r, :] = wbuf[c % 2][r, :] * act
        acc[c][...] += jnp.dot(vt_ref[...], gbuf[c][...], preferred_element_type=F32)

    @pl.when(j == 0)
    def _():
        for a in acc:
            a[...] = jnp.zeros(a.shape, F32)
        first_matmul(0, u_ref)
        gate_weights(0, nrow_ref, e1_ref)

    for c in range(n_chunks):
        if c + 1 < n_chunks:
            first_matmul(c + 1, u_ref)
            gate_weights(c + 1, nrow_ref, e1_ref)
        else:
            first_matmul(0, u_next_ref)
            gate_weights(0, nrow_next_ref, e1_next_ref)
        finish(c)

    @pl.when(j == pl.num_programs(1) - 1)
    def _():
        for c in range(n_chunks):
            o_ref[c * tc:(c + 1) * tc, :] = acc[c][...].T


def peer_dense(xt, u, vt, rank2, e2, nrow, e1):
    n_slabs, D, tc = xt.shape
    T = n_slabs * tc
    tm, te = PEER_TOKENS, PEER_EXPERT_TILE
    rows = te // PEER_KEYS
    n_tiles = PEER_EXPERTS // te
    assert rows == SUBLANE and tc == PEER_TOKEN_CHUNK and (tm // tc) % 2 == 0
    assert T % tm == 0
    nxt = lambda j: jnp.minimum(j + 1, n_tiles - 1)
    by_key2 = pl.BlockSpec((PEER_HEADS, PEER_KEYS, tm), lambda i, j: (0, 0, i))
    by_key1 = pl.BlockSpec((PEER_HEADS, 1, tm // LANE, rows, LANE), lambda i, j: (0, j, i, 0, 0))
    by_key1_next = pl.BlockSpec((PEER_HEADS, 1, tm // LANE, rows, LANE), lambda i, j: (0, nxt(j), i, 0, 0))
    return pl.pallas_call(
        _peer_dense_kernel,
        grid=(T // tm, n_tiles),
        in_specs=[pl.BlockSpec((tm // tc, D, tc), lambda i, j: (i, 0, 0)),
                  pl.BlockSpec((te, D), lambda i, j: (j, 0)),
                  pl.BlockSpec((te, D), lambda i, j: (nxt(j), 0)),
                  pl.BlockSpec((D, te), lambda i, j: (0, j)),
                  by_key2, by_key2, by_key1, by_key1, by_key1_next, by_key1_next],
        out_specs=pl.BlockSpec((tm, D), lambda i, j: (i, 0)),
        out_shape=jax.ShapeDtypeStruct((T, D), F32),
        scratch_shapes=([pltpu.VMEM((D, tc), F32)] * (tm // tc) + [pltpu.VMEM((te, tc), BF16)] * (tm // tc)
                        + [pltpu.VMEM((te, tc), BF16)] * 4),
        compiler_params=pltpu.CompilerParams(dimension_semantics=("parallel", "arbitrary"),
                                             vmem_limit_bytes=PEER_DENSE_VMEM),
        name="peer_dense",
    )(xt, u, u, vt, rank2, e2, nrow, e1, nrow, e1)


def peer_layer(h, g, w_q, keys, u, v):
    q, xt = norm_matmul(h, g, w_q.astype(BF16), F32, emit_t=True)
    rank2, e2, nrow, e1 = peer_select(q, keys)
    return peer_dense(xt, u.astype(BF16), v.T.astype(BF16), rank2, e2, nrow, e1)


def _pad_cols(w, n):
    return jnp.pad(w, ((0, 0), (0, n - w.shape[1])))


def kernel(x, p, mix_norm, ffn_norm, ple_norm, final_norm, ssd_w_in, ssd_conv_w, ssd_conv_b, ssd_dt_bias, ssd_a_log, ssd_d, ssd_norm, ssd_w_out, ret_w_in, ret_norm, ret_w_out, diff_w_in, diff_lambda, diff_norm, diff_w_out, fox_w_in, fox_b_f, fox_w_out, rel_bias, peer_w_q, peer_keys, peer_u, peer_v, ple_proj, ple_gate):
    B, S, D = x.shape
    T = B * S
    depth = mix_norm.shape[0]
    n_mixers = 4
    h = x.reshape(T, D)
    p_all = p.reshape(depth, T, PLE_DIM)
    for i in range(depth):
        m, j = i % n_mixers, i // n_mixers
        g = mix_norm[i]
        if m == 0:
            w = ssd_w_in[j]
            proj = norm_matmul(h, g, w[:, :SSD_MAIN].astype(BF16), BF16)
            dt_raw = norm_matmul(h, g, _pad_cols(w[:, SSD_MAIN:], LANE).astype(BF16), F32)
            y = ssd_core(proj.reshape(B, S, SSD_MAIN), dt_raw.reshape(B, S, LANE), ssd_conv_w[j],
                         ssd_conv_b[j], ssd_dt_bias[j], ssd_a_log[j], ssd_d[j], ssd_norm[j])
            w_out = ssd_w_out[j]
        elif m == 1:
            proj = norm_matmul(h, g, ret_w_in[j].astype(BF16), BF16)
            y = retention_core(proj.reshape(B, S, -1), ret_norm[j])
            w_out = ret_w_out[j]
        elif m == 2:
            lam_init = 0.8 - 0.6 * math.exp(-0.3 * i)
            proj = norm_matmul(h, g, diff_w_in[j].astype(BF16), BF16)
            y = diff_core(proj.reshape(B, S, -1), diff_lambda[j], diff_norm[j], rel_bias, lam_init)
            w_out = diff_w_out[j]
        else:
            w = fox_w_in[j]
            proj = norm_matmul(h, g, w[:, :3 * D].astype(BF16), BF16)
            c_t, c_rows = fox_gate(h.reshape(B, S, D), g, w[:, 3 * D:], fox_b_f[j])
            y = fox_core(proj.reshape(B, S, -1), c_t, c_rows)
            w_out = fox_w_out[j]
        h = matmul_residual(y.reshape(T, -1), w_out.astype(BF16), h)
        y = peer_layer(h, ffn_norm[i], peer_w_q[i], peer_keys[i], peer_u[i], peer_v[i])
        h = ple_layer(h, y, ple_norm[i], ple_gate[i].astype(BF16), p_all, i, ple_proj[i].astype(BF16),
                      final_norm, final=(i == depth - 1))
    return h.reshape(B, S, D)
```

```python
import functools
import math

import jax
import jax.numpy as jnp
import numpy as np
from jax import lax
from jax.experimental import pallas as pl
from jax.experimental.pallas import tpu as pltpu

F32 = jnp.float32
BF16 = jnp.bfloat16

D_MODEL = 1024
CHUNK = 64
NORM_EPS = 1e-6
LOG2E = math.log2(math.e)
ROPE_BASE = 10000.0
PLE_DIM = 256

SSD_D_INNER = 2 * D_MODEL
SSD_HEAD_DIM = 64
SSD_HEADS = SSD_D_INNER // SSD_HEAD_DIM
SSD_GROUPS = 4
SSD_HEADS_PER_GROUP = SSD_HEADS // SSD_GROUPS
SSD_STATE = 128
SSD_CONV = 4
SSD_CONV_DIM = SSD_D_INNER + 2 * SSD_GROUPS * SSD_STATE
SSD_MAIN = SSD_D_INNER + SSD_CONV_DIM
SSD_BLOCK = 128

RET_HEADS = 4
RET_QK_DIM = D_MODEL // RET_HEADS
RET_V_DIM = 2 * RET_QK_DIM
RET_V_WIDTH = RET_HEADS * RET_V_DIM
RET_BLOCK = 256

DIFF_HEADS = 8
DIFF_HEAD_DIM = D_MODEL // DIFF_HEADS // 2
FOX_HEADS = 16
FOX_HEAD_DIM = D_MODEL // FOX_HEADS
ATTN_TILE = 512
ATTN_QUERY_TILES = 2
HEAD_LANES = 128

REL_BUCKETS = 32
REL_MAX_DIST = 128

PEER_KEYS = 128
PEER_EXPERTS = PEER_KEYS * PEER_KEYS
PEER_HEADS = 8
PEER_TOPK = 16
PEER_QUERY_DIM = 256
PEER_HALF = PEER_QUERY_DIM // 2
PEER_TOKENS = 1024
PEER_TOKEN_CHUNK = 512
PEER_EXPERT_TILE = 1024
PEER_SELECT_TOKENS = 1024
PEER_DENSE_VMEM = 56 * 1024 * 1024

LANE = 128
SUBLANE = 8
VMEM_LIMIT = 48 * 1024 * 1024

_NT = (((1,), (1,)), ((), ()))


def _params(*sem):
    return pltpu.CompilerParams(dimension_semantics=sem, vmem_limit_bytes=VMEM_LIMIT)


def _rms(x, g):
    return x * lax.rsqrt(jnp.mean(x * x, axis=-1, keepdims=True) + NORM_EPS) * g


def _split3(x):
    hi = x.astype(BF16)
    r1 = x - hi.astype(F32)
    mid = r1.astype(BF16)
    lo = (r1 - mid.astype(F32)).astype(BF16)
    return hi, mid, lo


def _dot_exact_lhs01(m01, x):
    hi, mid, lo = _split3(x)
    d = functools.partial(jnp.dot, preferred_element_type=F32)
    return (d(m01, hi) + d(m01, mid)) + d(m01, lo)


def _norm_matmul_kernel(h_ref, g_ref, w_ref, o_ref, *rest, emit_t):
    if emit_t:
        xt_ref, xn_ref = rest
    else:
        (xn_ref,) = rest

    @pl.when(pl.program_id(1) == 0)
    def _():
        y = _rms(h_ref[...], g_ref[...])
        xn_ref[...] = y.astype(BF16)
        if emit_t:
            tc = xt_ref.shape[2]
            for c in range(xt_ref.shape[0]):
                xt_ref[c] = y[c * tc:(c + 1) * tc, :].T.astype(BF16)

    o_ref[...] = jnp.dot(xn_ref[...], w_ref[...], preferred_element_type=F32).astype(o_ref.dtype)


def norm_matmul(h, g, w, out_dtype, emit_t=False, tm=1024, tn=1024):
    T, D = h.shape
    N = w.shape[1]
    tn = min(tn, N)
    assert T % tm == 0 and N % tn == 0
    out_shape = [jax.ShapeDtypeStruct((T, N), out_dtype)]
    out_specs = [pl.BlockSpec((tm, tn), lambda i, j: (i, j))]
    if emit_t:
        tc = PEER_TOKEN_CHUNK
        out_shape.append(jax.ShapeDtypeStruct((T // tc, D, tc), BF16))
        out_specs.append(pl.BlockSpec((tm // tc, D, tc), lambda i, j: (i, 0, 0)))
    res = pl.pallas_call(
        functools.partial(_norm_matmul_kernel, emit_t=emit_t),
        grid=(T // tm, N // tn),
        in_specs=[pl.BlockSpec((tm, D), lambda i, j: (i, 0)),
                  pl.BlockSpec((1, D), lambda i, j: (0, 0)),
                  pl.BlockSpec((D, tn), lambda i, j: (0, j))],
        out_specs=out_specs,
        out_shape=out_shape,
        scratch_shapes=[pltpu.VMEM((tm, D), BF16)],
        compiler_params=_params("parallel", "arbitrary"),
        name="norm_matmul",
    )(h, g.reshape(1, D), w)
    return res if emit_t else res[0]


def _matmul_residual_kernel(a_ref, w_ref, h_ref, o_ref):
    o_ref[...] = h_ref[...] + jnp.dot(a_ref[...], w_ref[...], preferred_element_type=F32)


def matmul_residual(a, w, h, tm=512):
    T, K = a.shape
    N = w.shape[1]
    return pl.pallas_call(
        _matmul_residual_kernel,
        grid=(T // tm,),
        in_specs=[pl.BlockSpec((tm, K), lambda i: (i, 0)),
                  pl.BlockSpec((K, N), lambda i: (0, 0)),
                  pl.BlockSpec((tm, N), lambda i: (i, 0))],
        out_specs=pl.BlockSpec((tm, N), lambda i: (i, 0)),
        out_shape=jax.ShapeDtypeStruct((T, N), F32),
        input_output_aliases={2: 0},
        compiler_params=_params("parallel"),
        name="matmul_residual",
    )(a, w, h)


def _ple_kernel(h_ref, y_ref, g_ref, wg_ref, p_ref, wp_ref, fg_ref, o_ref, *, final):
    x = h_ref[...] + y_ref[...]
    xn = _rms(x, g_ref[...]).astype(BF16)
    gate = jax.nn.sigmoid(jnp.dot(xn, wg_ref[...], preferred_element_type=F32))
    proj = jnp.dot(p_ref[...].astype(BF16), wp_ref[...], preferred_element_type=F32)
    y = x + gate * proj
    if final:
        y = _rms(y, fg_ref[...])
    o_ref[...] = y


def ple_layer(h, y, g, w_gate, p_all, layer, w_proj, final_g, final, tm=512):
    T, D = h.shape
    return pl.pallas_call(
        functools.partial(_ple_kernel, final=final),
        grid=(T // tm,),
        in_specs=[pl.BlockSpec((tm, D), lambda i: (i, 0)),
                  pl.BlockSpec((tm, D), lambda i: (i, 0)),
                  pl.BlockSpec((1, D), lambda i: (0, 0)),
                  pl.BlockSpec((D, D), lambda i: (0, 0)),
                  pl.BlockSpec((None, tm, PLE_DIM), lambda i: (layer, i, 0)),
                  pl.BlockSpec((PLE_DIM, D), lambda i: (0, 0)),
                  pl.BlockSpec((1, D), lambda i: (0, 0))],
        out_specs=pl.BlockSpec((tm, D), lambda i: (i, 0)),
        out_shape=jax.ShapeDtypeStruct((T, D), F32),
        input_output_aliases={0: 0},
        compiler_params=_params("parallel"),
        name="ple_layer",
    )(h, y, g.reshape(1, D), w_gate, p_all, w_proj, final_g.reshape(1, D))


def _ssd_kernel(proj_ref, dt_ref, cw_ref, cb_ref, dtb_ref, alog_ref, dsk_ref, ng_ref, ewide_ref, efeat_ref,
                o_ref, xbuf, state, ybuf, ibuf, cwide, cfeat, xw_s, *, L):
    DI, P, N, R = SSD_D_INNER, SSD_HEAD_DIM, SSD_STATE, SSD_HEADS_PER_GROUP
    GW = DI // SSD_GROUPS
    PAIR = 2 * P
    assert PAIR == LANE and L == LANE

    @pl.when(pl.program_id(1) == 0)
    def _():
        xbuf[0:8, :] = jnp.zeros((8, SSD_CONV_DIM), F32)
        state[...] = jnp.zeros(state.shape, F32)

    xbc = proj_ref[0, :, DI:].astype(F32)
    xbuf[8:8 + L, :] = xbc
    conv = cb_ref[...] + cw_ref[3:4, :] * xbc
    for j in range(1, SSD_CONV):
        conv = conv + cw_ref[SSD_CONV - 1 - j:SSD_CONV - j, :] * xbuf[8 - j:8 - j + L, :]
    xbuf[0:8, :] = xbuf[L:L + 8, :]
    act = conv * jax.nn.sigmoid(conv)
    xs = act[:, :DI]
    bm = act[:, DI:DI + SSD_GROUPS * N]
    cm = act[:, DI + SSD_GROUPS * N:]

    dt = jax.nn.softplus(dt_ref[0] + dtb_ref[...])
    a = dt * (-jnp.exp(alog_ref[...]))
    row = lax.broadcasted_iota(jnp.int32, (L, L), 0)
    col = lax.broadcasted_iota(jnp.int32, (L, L), 1)
    tril = row >= col
    cum = _dot_exact_lhs01(tril.astype(BF16), a)
    cum_t = cum.T
    dt_t = dt.T

    split = lambda x: jnp.concatenate(_split3(x), axis=1)
    cum3 = split(cum)
    cwide[...] = jnp.dot(cum3, ewide_ref[...], preferred_element_type=F32)
    cfeat[...] = jnp.dot(cum3, efeat_ref[...], preferred_element_type=F32)
    dfeat = jnp.dot(split(dt), efeat_ref[...], preferred_element_type=F32)
    last = cfeat[L - 1:L, :]
    xw_s[...] = (xs * (jnp.exp(last - cfeat[...]) * dfeat)).astype(BF16)
    elast = jnp.exp(last)
    xs_b = xs.astype(BF16)
    lane = lax.broadcasted_iota(jnp.int32, (1, PAIR), 1)
    lower = lane < P
    zero = jnp.zeros((L, PAIR), BF16)

    for g in range(SSD_GROUPS):
        bg = bm[:, g * N:(g + 1) * N]
        cg = cm[:, g * N:(g + 1) * N].astype(BF16)
        cb = lax.dot_general(cg, bg.astype(BF16), _NT, preferred_element_type=F32)
        bg_t = bg.T.astype(BF16)
        for pr in range(R // 2):
            blk = slice((g * (R // 2) + pr) * PAIR, (g * (R // 2) + pr + 1) * PAIR)
            xp = xs_b[:, blk]
            halves = (jnp.where(lower, xp, zero), jnp.where(lower, zero, xp))
            y = None
            for k in range(2):
                hd = g * R + 2 * pr + k
                seg = cwide[:, hd * LANE:(hd + 1) * LANE] - cum_t[hd:hd + 1, :]
                w = jnp.exp(jnp.where(tril, seg, -jnp.inf)) * cb * dt_t[hd:hd + 1, :]
                yk = jnp.dot(w.astype(BF16), halves[k], preferred_element_type=F32)
                y = yk if y is None else y + yk
            st = state[g, :, pr * PAIR:(pr + 1) * PAIR]
            ybuf[:, blk] = y
            ibuf[:, blk] = jnp.dot(cg, st.astype(BF16), preferred_element_type=F32)
            state[g, :, pr * PAIR:(pr + 1) * PAIR] = (st * elast[:, blk]
                                                      + jnp.dot(bg_t, xw_s[:, blk], preferred_element_type=F32))

    y = ybuf[...] + ibuf[...] * jnp.exp(cfeat[...]) + dsk_ref[...] * xs
    z = proj_ref[0, :, :DI].astype(F32)
    y = y * (z * jax.nn.sigmoid(z))
    for g in range(SSD_GROUPS):
        yg = y[:, g * GW:(g + 1) * GW]
        o_ref[0, :, g * GW:(g + 1) * GW] = _rms(yg, ng_ref[:, g * GW:(g + 1) * GW]).astype(o_ref.dtype)


def ssd_core(proj, dt_raw, conv_w, conv_b, dt_bias, a_log, d_skip, norm_g):
    B, S, _ = proj.shape
    L = SSD_BLOCK
    assert S % L == 0
    pad = LANE - SSD_HEADS
    vec = lambda n: pl.BlockSpec((1, n), lambda b, c: (0, 0))
    whole = lambda a: pl.BlockSpec(a.shape, lambda b, c: (0, 0))

    def spread(width):
        e = np.zeros((LANE, SSD_HEADS * width), np.float32)
        for h in range(SSD_HEADS):
            e[h, h * width:(h + 1) * width] = 1.0
        return jnp.asarray(np.concatenate([e, e, e], axis=0), BF16)

    e_wide, e_feat = spread(LANE), spread(SSD_HEAD_DIM)
    return pl.pallas_call(
        functools.partial(_ssd_kernel, L=L),
        grid=(B, S // L),
        in_specs=[pl.BlockSpec((1, L, SSD_MAIN), lambda b, c: (b, c, 0)),
                  pl.BlockSpec((1, L, LANE), lambda b, c: (b, c, 0)),
                  pl.BlockSpec((SSD_CONV, SSD_CONV_DIM), lambda b, c: (0, 0)),
                  vec(SSD_CONV_DIM), vec(LANE), vec(LANE), vec(SSD_D_INNER), vec(SSD_D_INNER),
                  whole(e_wide), whole(e_feat)],
        out_specs=pl.BlockSpec((1, L, SSD_D_INNER), lambda b, c: (b, c, 0)),
        out_shape=jax.ShapeDtypeStruct((B, S, SSD_D_INNER), BF16),
        scratch_shapes=[pltpu.VMEM((L + 8, SSD_CONV_DIM), F32),
                        pltpu.VMEM((SSD_GROUPS, SSD_STATE, SSD_D_INNER // SSD_GROUPS), F32),
                        pltpu.VMEM((L, SSD_D_INNER), F32), pltpu.VMEM((L, SSD_D_INNER), F32),
                        pltpu.VMEM((L, SSD_HEADS * LANE), F32), pltpu.VMEM((L, SSD_D_INNER), F32),
                        pltpu.VMEM((L, SSD_D_INNER), BF16)],
        compiler_params=_params("parallel", "arbitrary"),
        name="ssd_core",
    )(proj, dt_raw,
      conv_w.reshape(SSD_CONV, SSD_CONV_DIM), conv_b.reshape(1, SSD_CONV_DIM),
      jnp.pad(dt_bias, (0, pad)).reshape(1, LANE), jnp.pad(a_log, (0, pad)).reshape(1, LANE),
      jnp.repeat(d_skip, SSD_HEAD_DIM).reshape(1, SSD_D_INNER), norm_g.reshape(1, SSD_D_INNER),
      e_wide, e_feat)


def _retention_kernel(q_ref, k_ref, v_ref, gate_ref, cos_ref, sin_ref, dec_ref, qd_ref, kd_ref,
                      cd_ref, ng_ref, o_ref, state):
    half = RET_QK_DIM // 2

    @pl.when(pl.program_id(2) == 0)
    def _():
        state[...] = jnp.zeros(state.shape, F32)

    cos = cos_ref[...]
    sin = sin_ref[...]

    def rot(x):
        x1, x2 = x[:, :half], x[:, half:]
        return jnp.concatenate([x1 * cos - x2 * sin, x1 * sin + x2 * cos], axis=-1)

    q = rot(q_ref[0].astype(F32))
    k = rot(k_ref[0].astype(F32)) * (RET_QK_DIM ** -0.5)
    v = v_ref[0]
    qb = q.astype(BF16)
    s = lax.dot_general(qb, k.astype(BF16), _NT, preferred_element_type=F32) * dec_ref[0]
    o = jnp.dot(s.astype(BF16), v, preferred_element_type=F32)
    st = state[...]
    o = o + jnp.dot(qb, st.astype(BF16), preferred_element_type=F32) * qd_ref[0]
    kt = (k * kd_ref[0]).T.astype(BF16)
    state[...] = st * cd_ref[0] + jnp.dot(kt, v, preferred_element_type=F32)
    gate = gate_ref[0].astype(F32)
    o_ref[0] = (_rms(o, ng_ref[0]) * (gate * jax.nn.sigmoid(gate))).astype(o_ref.dtype)


def retention_core(proj, norm_g):
    B, S, _ = proj.shape
    H, dk, dv, L = RET_HEADS, RET_QK_DIM, RET_V_DIM, RET_BLOCK
    assert S % L == 0 and L % CHUNK == 0
    inv = 1.0 / (ROPE_BASE ** (jnp.arange(0, dk, 2, dtype=F32) / dk))
    ang = jnp.arange(S, dtype=F32)[:, None] * inv[None, :]
    log_gamma = jnp.log1p(-jnp.exp2(-5.0 - jnp.arange(H, dtype=F32)))
    idx = jnp.arange(L, dtype=F32)
    visible = (jnp.arange(L)[None, :] // CHUNK) <= (jnp.arange(L)[:, None] // CHUNK)
    decay = jnp.where(visible[None],
                      jnp.exp(log_gamma[:, None, None] * jnp.abs(idx[:, None] - idx[None, :])), 0.0)
    q_decay = jnp.exp(log_gamma[:, None] * (idx[None, :] + 1.0))[..., None]
    k_decay = jnp.exp(log_gamma[:, None] * (L - 1.0 - idx[None, :]))[..., None]
    block_decay = jnp.exp(log_gamma * L).reshape(H, 1, 1)
    return pl.pallas_call(
        _retention_kernel,
        grid=(B, H, S // L),
        in_specs=[pl.BlockSpec((1, L, dk), lambda b, h, c: (b, c, h)),
                  pl.BlockSpec((1, L, dk), lambda b, h, c: (b, c, H + h)),
                  pl.BlockSpec((1, L, dv), lambda b, h, c: (b, c, H + h)),
                  pl.BlockSpec((1, L, dv), lambda b, h, c: (b, c, 2 * H + h)),
                  pl.BlockSpec((L, dk // 2), lambda b, h, c: (c, 0)),
                  pl.BlockSpec((L, dk // 2), lambda b, h, c: (c, 0)),
                  pl.BlockSpec((1, L, L), lambda b, h, c: (h, 0, 0)),
                  pl.BlockSpec((1, L, 1), lambda b, h, c: (h, 0, 0)),
                  pl.BlockSpec((1, L, 1), lambda b, h, c: (h, 0, 0)),
                  pl.BlockSpec((1, 1, 1), lambda b, h, c: (h, 0, 0)),
                  pl.BlockSpec((1, 1, dv), lambda b, h, c: (h, 0, 0))],
        out_specs=pl.BlockSpec((1, L, dv), lambda b, h, c: (b, c, h)),
        out_shape=jax.ShapeDtypeStruct((B, S, RET_V_WIDTH), BF16),
        scratch_shapes=[pltpu.VMEM((dk, dv), F32)],
        compiler_params=_params("parallel", "parallel", "arbitrary"),
        name="retention_core",
    )(proj, proj, proj, proj, jnp.cos(ang), jnp.sin(ang), decay, q_decay, k_decay, block_decay,
      norm_g.reshape(H, 1, dv))


def _attn_kernel(*refs, mode, T):
    if mode == "diff":
        q_ref, k_ref, v_ref, bias_ref, lam_ref, ng_ref, o_ref, m_s, acc_s, l_s = refs
    else:
        q_ref, k_ref, v_ref, ck_ref, cq_ref, o_ref, m_s, acc_s, cq_s = refs
    i = pl.program_id(2)
    NQ = ATTN_QUERY_TILES
    half = HEAD_LANES // 2
    reps = T // HEAD_LANES
    lane = lax.broadcasted_iota(jnp.int32, (1, HEAD_LANES), 1)
    first = lane < half
    second = jnp.logical_not(first)
    qs = []
    for u in range(NQ):
        q = (q_ref[0, u * T:(u + 1) * T, :].astype(F32) * (half ** -0.5 * LOG2E)).astype(BF16)
        zero = jnp.zeros_like(q)
        qs.append((jnp.where(first, q, zero), jnp.where(first, zero, q)))

    m_s[...] = jnp.full(m_s.shape, -jnp.inf, F32)
    acc_s[...] = jnp.zeros(acc_s.shape, F32)
    if mode == "diff":
        l_s[...] = jnp.zeros(l_s.shape, F32)
    else:
        for u in range(NQ):
            for a in range(2):
                cq_s[u, a] = jnp.broadcast_to(cq_ref[0, 0, u * T:(u + 1) * T, a:a + 1], (T, T))

    def step(j, subtiles, diagonal):
        start = pl.multiple_of(j * T, T)
        k = k_ref[0, pl.ds(start, T), :]
        v = v_ref[0, pl.ds(start, T), :]
        if mode == "diff":
            vs = (v, v)
        else:
            ones = jnp.ones_like(v)
            vs = (jnp.where(first, v, ones), jnp.where(second, v, ones))
        if diagonal is not None:
            row = lax.broadcasted_iota(jnp.int32, (T, T), 0)
            col = lax.broadcasted_iota(jnp.int32, (T, T), 1)
            if mode == "diff":
                visible = (col // CHUNK) <= (row // CHUNK)
            else:
                visible = col <= row
        for u in subtiles:
            for a in range(2):
                s = lax.dot_general(qs[u][a], k, _NT, preferred_element_type=F32)
                if mode == "diff":
                    s = s + bias_ref[0, jnp.minimum(NQ * i + u - j, 2)]
                else:
                    s = (s - ck_ref[0, 0, j, a:a + 1, :]) + cq_s[u, a]
                if u == diagonal:
                    s = jnp.where(visible, s, -jnp.inf)
                m_prev = m_s[u, a]
                m_new = jnp.maximum(m_prev, jnp.max(s, axis=-1, keepdims=True))
                alpha = jnp.exp2(m_prev - m_new)
                p = jnp.exp2(s - jnp.concatenate([m_new] * reps, axis=1))
                if mode == "diff":
                    l_s[u, a] = alpha * l_s[u, a] + jnp.sum(p, axis=-1, keepdims=True)
                acc_s[u, a] = alpha * acc_s[u, a] + jnp.dot(p.astype(BF16), vs[a], preferred_element_type=F32)
                m_s[u, a] = m_new

    everyone = tuple(range(NQ))
    lax.fori_loop(0, NQ * i, lambda j, c: (step(j, everyone, None), c)[1], 0)
    for d in range(NQ):
        step(NQ * i + d, everyone[d:], d)

    for u in range(NQ):
        if mode == "diff":
            o = acc_s[u, 0] / l_s[u, 0] - lam_ref[0] * (acc_s[u, 1] / l_s[u, 1])
            o = _rms(o, ng_ref[...])
        else:
            o0, o1 = acc_s[u, 0], acc_s[u, 1]
            o = jnp.where(first, o0 / pltpu.roll(o0, half, 1), o1 / pltpu.roll(o1, half, 1))
        o_ref[0, u * T:(u + 1) * T, :] = o.astype(o_ref.dtype)


def _attn_call(mode, proj, n_blocks, extra_inputs, extra_specs):
    B, S, _ = proj.shape
    T, NQ = ATTN_TILE, ATTN_QUERY_TILES
    assert S % (NQ * T) == 0
    stats = pltpu.VMEM((NQ, 2, T, HEAD_LANES), F32)
    return pl.pallas_call(
        functools.partial(_attn_kernel, mode=mode, T=T),
        grid=(B, n_blocks, S // (NQ * T)),
        in_specs=[pl.BlockSpec((1, NQ * T, HEAD_LANES), lambda b, h, i: (b, i, h)),
                  pl.BlockSpec((1, S, HEAD_LANES), lambda b, h, i: (b, 0, n_blocks + h)),
                  pl.BlockSpec((1, S, HEAD_LANES), lambda b, h, i: (b, 0, 2 * n_blocks + h))] + extra_specs,
        out_specs=pl.BlockSpec((1, NQ * T, HEAD_LANES), lambda b, h, i: (b, i, h)),
        out_shape=jax.ShapeDtypeStruct((B, S, D_MODEL), BF16),
        scratch_shapes=[stats, stats, stats if mode == "diff" else pltpu.VMEM((NQ, 2, T, T), F32)],
        compiler_params=_params("parallel", "parallel", "arbitrary"),
        name=mode + "_attention",
    )(proj, proj, proj, *extra_inputs)


def _t5_bucket(rel):
    nb = REL_BUCKETS // 2
    max_exact = nb // 2
    ret = (rel > 0).astype(jnp.int32) * nb
    n = jnp.abs(rel)
    nf = jnp.maximum(n, 1).astype(F32)
    large = max_exact + (jnp.log(nf / max_exact) / math.log(REL_MAX_DIST / max_exact)
                         * (nb - max_exact)).astype(jnp.int32)
    large = jnp.minimum(large, nb - 1)
    return ret + jnp.where(n < max_exact, n, large)


def _bias_tiles_kernel(bucket_ref, table_ref, o_ref):
    h = pl.program_id(0)
    for d in range(bucket_ref.shape[0]):
        bucket = bucket_ref[d]
        tile = jnp.zeros(bucket.shape, F32)
        for b in range(REL_BUCKETS):
            tile = jnp.where(bucket == b, table_ref[b * DIFF_HEADS + h] * LOG2E, tile)
        o_ref[0, d] = tile


def diff_bias_tiles(rel_bias):
    T = ATTN_TILE
    assert T >= REL_MAX_DIST
    off = jnp.arange(T)
    rel = (off[None, None, :] - off[None, :, None]) - (jnp.arange(3) * T)[:, None, None]
    return pl.pallas_call(
        _bias_tiles_kernel,
        grid=(DIFF_HEADS,),
        in_specs=[pl.BlockSpec((3, T, T), lambda h: (0, 0, 0)),
                  pl.BlockSpec(memory_space=pltpu.SMEM)],
        out_specs=pl.BlockSpec((1, 3, T, T), lambda h: (h, 0, 0, 0)),
        out_shape=jax.ShapeDtypeStruct((DIFF_HEADS, 3, T, T), F32),
        compiler_params=_params("parallel"),
        name="diff_bias_tiles",
    )(_t5_bucket(rel), rel_bias.astype(F32).reshape(REL_BUCKETS * DIFF_HEADS))


def diff_core(proj, lam_vecs, norm_g, rel_bias, lam_init):
    T = ATTN_TILE
    lv = lam_vecs.astype(F32)
    lam = jnp.exp(jnp.sum(lv[0] * lv[1])) - jnp.exp(jnp.sum(lv[2] * lv[3])) + lam_init
    bias = diff_bias_tiles(rel_bias)
    g = (norm_g * (1.0 - lam_init)).reshape(1, HEAD_LANES)
    extra_specs = [pl.BlockSpec((1, 3, T, T), lambda b, h, i: (h, 0, 0, 0)),
                   pl.BlockSpec(memory_space=pltpu.SMEM),
                   pl.BlockSpec((1, HEAD_LANES), lambda b, h, i: (0, 0))]
    return _attn_call("diff", proj, DIFF_HEADS, [bias, lam.reshape(1), g], extra_specs)


def _fox_gate_kernel(h_ref, g_ref, w_ref, b_ref, ct_ref, c_ref, carry, *, L):
    @pl.when(pl.program_id(1) == 0)
    def _():
        carry[...] = jnp.zeros(carry.shape, F32)

    xn = _rms(h_ref[0], g_ref[...])
    logits = lax.dot_general(w_ref[...], xn.astype(BF16), _NT, preferred_element_type=F32)
    log_f = jax.nn.log_sigmoid(logits + b_ref[...])
    row = lax.broadcasted_iota(jnp.int32, (L, L), 0)
    col = lax.broadcasted_iota(jnp.int32, (L, L), 1)
    triu = (row <= col).astype(BF16)
    hi, mid, lo = _split3(log_f)
    d = functools.partial(jnp.dot, preferred_element_type=F32)
    c = carry[...] + ((d(hi, triu) + d(mid, triu)) + d(lo, triu))
    carry[...] = c[:, L - 1:L]
    c2 = c * LOG2E
    ct_ref[0] = c2
    c_ref[0] = c2.T


def fox_gate(h3, g, w_f, b_f):
    B, S, D = h3.shape
    L = 512
    pad = LANE - FOX_HEADS
    return pl.pallas_call(
        functools.partial(_fox_gate_kernel, L=L),
        grid=(B, S // L),
        in_specs=[pl.BlockSpec((1, L, D), lambda b, c: (b, c, 0)),
                  pl.BlockSpec((1, D), lambda b, c: (0, 0)),
                  pl.BlockSpec((LANE, D), lambda b, c: (0, 0)),
                  pl.BlockSpec((LANE, 1), lambda b, c: (0, 0))],
        out_specs=[pl.BlockSpec((1, LANE, L), lambda b, c: (b, 0, c)),
                   pl.BlockSpec((1, L, LANE), lambda b, c: (b, c, 0))],
        out_shape=[jax.ShapeDtypeStruct((B, LANE, S), F32), jax.ShapeDtypeStruct((B, S, LANE), F32)],
        scratch_shapes=[pltpu.VMEM((LANE, 1), F32)],
        compiler_params=_params("parallel", "arbitrary"),
        name="fox_gate",
    )(h3, g.reshape(1, D), jnp.pad(w_f.T, ((0, pad), (0, 0))).astype(BF16),
      jnp.pad(b_f, (0, pad)).reshape(LANE, 1))


def fox_core(proj, c_t, c_rows):
    B, S, _ = proj.shape
    T = ATTN_TILE
    nb = FOX_HEADS // 2
    ck = jnp.transpose(c_t[:, :FOX_HEADS].reshape(B, nb, 2, S // T, T), (0, 1, 3, 2, 4))
    cq = jnp.transpose(c_rows[:, :, :FOX_HEADS].reshape(B, S, nb, 2), (0, 2, 1, 3))
    extra_specs = [pl.BlockSpec((1, 1, S // T, 2, T), lambda b, h, i: (b, h, 0, 0, 0)),
                   pl.BlockSpec((1, 1, ATTN_QUERY_TILES * T, 2), lambda b, h, i: (b, h, i, 0))]
    return _attn_call("fox", proj, nb, [ck, cq], extra_specs)


def _candidate_tables():
    pairs = [(a, b) for a in range(PEER_TOPK) for b in range(PEER_TOPK) if (a + 1) * (b + 1) <= PEER_TOPK]
    rows = 64
    p1 = np.zeros((rows, LANE), np.float32)
    p2 = np.zeros((rows, LANE), np.float32)
    for r, (a, b) in enumerate(pairs):
        p1[r, a] = 1.0
        p2[r, b] = 1.0
    return len(pairs), p1, p2


N_CAND, _P1, _P2 = _candidate_tables()


_INT_MIN = -2 ** 31
_FLIP = 0x7FFFFFFF


def _ordered_int(x):
    b = lax.bitcast_convert_type(x + 0.0, jnp.int32)
    return jnp.where(b < 0, b ^ _FLIP, b)


def _ordered_float(k):
    return lax.bitcast_convert_type(jnp.where(k < 0, k ^ _FLIP, k), F32)


def _pop_top16(keys, break_ties):
    n, t = keys.shape
    idx = lax.broadcasted_iota(jnp.int32, (n, t), 0)
    ridx = lax.broadcasted_iota(jnp.int32, (PEER_TOPK, t), 0)
    vals = jnp.zeros((PEER_TOPK, t), jnp.int32)
    work = keys
    for r in range(PEER_TOPK):
        mx = jnp.max(work, axis=0, keepdims=True)
        hit = work == mx
        if break_ties:
            pos = jnp.min(jnp.where(hit, idx, n), axis=0, keepdims=True)
            hit = idx == pos
        work = jnp.where(hit, _INT_MIN + r, work)
        vals = jnp.where(ridx == r, mx, vals)
    return work, vals


def _top16_ranks(s, break_ties):
    work, vals = _pop_top16(_ordered_int(s), break_ties)
    rank = jnp.where(work < _INT_MIN + PEER_TOPK, work & (2 * PEER_TOPK - 1), PEER_TOPK)
    return rank, _ordered_float(vals)


def _peer_select_kernel(q_ref, keys_ref, p1_ref, p2_ref, p1t_ref, rank2_ref, e2_ref, nrow_ref, e1_ref):
    def scores(c):
        kk = keys_ref[0, c]
        qq = q_ref[:, c * PEER_HALF:(c + 1) * PEER_HALF]
        kh, km, _ = _split3(kk)
        qh, qm, _ = _split3(qq)
        d = lambda a, b: lax.dot_general(a, b, _NT, preferred_element_type=F32)
        return d(kh, qh) + (d(kh, qm) + d(km, qh))

    s1 = scores(0)
    s2 = scores(1)
    out_refs = (rank2_ref, e2_ref, nrow_ref, e1_ref)
    clean = _peer_select_pass(s1, s2, p1_ref, p2_ref, p1t_ref, out_refs, break_ties=False)

    @pl.when(jnp.logical_not(clean))
    def _():
        _peer_select_pass(s1, s2, p1_ref, p2_ref, p1t_ref, out_refs, break_ties=True)


def _peer_select_pass(s1, s2, p1_ref, p2_ref, p1t_ref, out_refs, break_ties):
    rank2_ref, e2_ref, nrow_ref, e1_ref = out_refs
    rank1, v1 = _top16_ranks(s1, break_ties)
    rank2, v2 = _top16_ranks(s2, break_ties)

    tokens = s1.shape[1]
    pad = jnp.zeros((LANE - PEER_TOPK, tokens), F32)
    cand = (_dot_exact_lhs01(p1_ref[...], jnp.concatenate([v1, pad], axis=0))
            + _dot_exact_lhs01(p2_ref[...], jnp.concatenate([v2, pad], axis=0)))
    cidx = lax.broadcasted_iota(jnp.int32, cand.shape, 0)
    cand = jnp.where(cidx < N_CAND, cand, -jnp.inf)
    popped, _ = _pop_top16(_ordered_int(cand), break_ties)
    sel = jnp.where(popped < _INT_MIN + PEER_TOPK, 1.0, 0.0)
    top = v1[0:1, :] + v2[0:1, :]
    z = jnp.sum(sel * jnp.exp(jnp.where(sel > 0.0, cand - top, 0.0)), axis=0, keepdims=True)
    sel_pad = jnp.concatenate([sel, jnp.zeros((LANE - sel.shape[0], tokens), F32)], axis=0).astype(BF16)
    n_by_rank = jnp.dot(p1t_ref[...], sel_pad, preferred_element_type=F32)
    nrow = jnp.zeros(s1.shape, F32)
    for a in range(PEER_TOPK):
        nrow = jnp.where(rank1 == a, n_by_rank[a:a + 1, :], nrow)

    rank2_ref[0] = rank2.astype(F32).astype(BF16)
    e2_ref[0] = jnp.exp(s2 - v2[0:1, :]).astype(BF16)
    e1 = jnp.exp(s1 - v1[0:1, :]) * (0.5 / z)
    for g in range(PEER_KEYS // SUBLANE):
        for st in range(tokens // LANE):
            tile = (slice(g * SUBLANE, (g + 1) * SUBLANE), slice(st * LANE, (st + 1) * LANE))
            nrow_ref[0, g, st] = nrow[tile]
            e1_ref[0, g, st] = e1[tile]

    if break_ties:
        return None
    ranked = (jnp.sum(jnp.where(rank1 < PEER_TOPK, 1.0, 0.0), axis=0, keepdims=True)
              + jnp.sum(jnp.where(rank2 < PEER_TOPK, 1.0, 0.0), axis=0, keepdims=True)
              + jnp.sum(sel, axis=0, keepdims=True))
    return jnp.max(ranked) == 3.0 * PEER_TOPK


def peer_select(q, keys):
    T = q.shape[0]
    tm = PEER_SELECT_TOKENS
    assert T % tm == 0
    hk = pl.BlockSpec((1, PEER_KEYS, tm), lambda i, h: (h, 0, i))
    groups = PEER_KEYS // SUBLANE
    hk1 = pl.BlockSpec((1, groups, tm // LANE, SUBLANE, LANE), lambda i, h: (h, 0, i, 0, 0))
    shp = lambda dt: jax.ShapeDtypeStruct((PEER_HEADS, PEER_KEYS, T), dt)
    shp1 = jax.ShapeDtypeStruct((PEER_HEADS, groups, T // LANE, SUBLANE, LANE), F32)
    cst = lambda a: pl.BlockSpec(a.shape, lambda i, h: (0, 0))
    p1, p2 = jnp.asarray(_P1, BF16), jnp.asarray(_P2, BF16)
    p1t = jnp.asarray(np.pad(_P1.T, ((0, 0), (0, LANE - _P1.shape[0]))), BF16)
    return pl.pallas_call(
        _peer_select_kernel,
        grid=(T // tm, PEER_HEADS),
        in_specs=[pl.BlockSpec((tm, PEER_QUERY_DIM), lambda i, h: (i, h)),
                  pl.BlockSpec((1, 2, PEER_KEYS, PEER_HALF), lambda i, h: (h, 0, 0, 0)),
                  cst(p1), cst(p2), cst(p1t)],
        out_specs=[hk, hk, hk1, hk1],
        out_shape=[shp(BF16), shp(BF16), shp1, shp1],
        compiler_params=_params("parallel", "parallel"),
        name="peer_select",
    )(q, keys, p1, p2, p1t)


def _peer_dense_kernel(xt_ref, u_ref, u_next_ref, vt_ref, rank2_ref, e2_ref, nrow_ref, e1_ref,
                       nrow_next_ref, e1_next_ref, o_ref, *scratch):
    n = xt_ref.shape[0]
    acc, gbuf, hbuf, wbuf = scratch[:n], scratch[n:2 * n], scratch[2 * n:2 * n + 2], scratch[2 * n + 2:]
    j = pl.program_id(1)
    rows = PEER_EXPERT_TILE // PEER_KEYS
    tc = PEER_TOKEN_CHUNK
    n_chunks = xt_ref.shape[0]
    pack = 16

    def first_matmul(c, u):
        hbuf[c % 2][...] = jnp.dot(u[...], xt_ref[c], preferred_element_type=F32).astype(BF16)

    def gate_weights(c, nrow, e1):
        cols = slice(c * tc, (c + 1) * tc)
        for ii in range(rows):
            w = jnp.zeros((PEER_KEYS, tc), BF16)
            for h in range(PEER_HEADS):
                def spread(ref):
                    tiles = [jnp.broadcast_to(ref[h, 0, c * (tc // LANE) + lt, ii:ii + 1, :],
                                              (pack, LANE)).astype(BF16) for lt in range(tc // LANE)]
                    return jnp.concatenate([jnp.concatenate(tiles, axis=1)] * (PEER_KEYS // pack), axis=0)
                w = w + jnp.where(rank2_ref[h, :, cols] < spread(nrow), e2_ref[h, :, cols] * spread(e1),
                                  jnp.zeros_like(w))
            wbuf[c % 2][ii * PEER_KEYS:(ii + 1) * PEER_KEYS, :] = w

    def finish(c):
        for ii in range(rows):
            r = slice(ii * PEER_KEYS, (ii + 1) * PEER_KEYS)
            hr = hbuf[c % 2][r, :]
            act = hr * (1.0 + lax.erf(hr * jnp.asarray(2.0 ** -0.5, BF16)))
            gbuf[c][r, :] = wbuf[c % 2][r, :] * act
        acc[c][...] += jnp.dot(vt_ref[...], gbuf[c][...], preferred_element_type=F32)

    @pl.when(j == 0)
    def _():
        for a in acc:
            a[...] = jnp.zeros(a.shape, F32)
        first_matmul(0, u_ref)
        gate_weights(0, nrow_ref, e1_ref)

    for c in range(n_chunks):
        if c + 1 < n_chunks:
            first_matmul(c + 1, u_ref)
            gate_weights(c + 1, nrow_ref, e1_ref)
        else:
            first_matmul(0, u_next_ref)
            gate_weights(0, nrow_next_ref, e1_next_ref)
        finish(c)

    @pl.when(j == pl.num_programs(1) - 1)
    def _():
        for c in range(n_chunks):
            o_ref[c * tc:(c + 1) * tc, :] = acc[c][...].T


def peer_dense(xt, u, vt, rank2, e2, nrow, e1):
    n_slabs, D, tc = xt.shape
    T = n_slabs * tc
    tm, te = PEER_TOKENS, PEER_EXPERT_TILE
    rows = te // PEER_KEYS
    n_tiles = PEER_EXPERTS // te
    assert rows == SUBLANE and tc == PEER_TOKEN_CHUNK and (tm // tc) % 2 == 0
    assert T % tm == 0
    nxt = lambda j: jnp.minimum(j + 1, n_tiles - 1)
    by_key2 = pl.BlockSpec((PEER_HEADS, PEER_KEYS, tm), lambda i, j: (0, 0, i))
    by_key1 = pl.BlockSpec((PEER_HEADS, 1, tm // LANE, rows, LANE), lambda i, j: (0, j, i, 0, 0))
    by_key1_next = pl.BlockSpec((PEER_HEADS, 1, tm // LANE, rows, LANE), lambda i, j: (0, nxt(j), i, 0, 0))
    return pl.pallas_call(
        _peer_dense_kernel,
        grid=(T // tm, n_tiles),
        in_specs=[pl.BlockSpec((tm // tc, D, tc), lambda i, j: (i, 0, 0)),
                  pl.BlockSpec((te, D), lambda i, j: (j, 0)),
                  pl.BlockSpec((te, D), lambda i, j: (nxt(j), 0)),
                  pl.BlockSpec((D, te), lambda i, j: (0, j)),
                  by_key2, by_key2, by_key1, by_key1, by_key1_next, by_key1_next],
        out_specs=pl.BlockSpec((tm, D), lambda i, j: (i, 0)),
        out_shape=jax.ShapeDtypeStruct((T, D), F32),
        scratch_shapes=([pltpu.VMEM((D, tc), F32)] * (tm // tc) + [pltpu.VMEM((te, tc), BF16)] * (tm // tc)
                        + [pltpu.VMEM((te, tc), BF16)] * 4),
        compiler_params=pltpu.CompilerParams(dimension_semantics=("parallel", "arbitrary"),
                                             vmem_limit_bytes=PEER_DENSE_VMEM),
        name="peer_dense",
    )(xt, u, u, vt, rank2, e2, nrow, e1, nrow, e1)


def peer_layer(h, g, w_q, keys, u, v):
    q, xt = norm_matmul(h, g, w_q.astype(BF16), F32, emit_t=True)
    rank2, e2, nrow, e1 = peer_select(q, keys)
    return peer_dense(xt, u.astype(BF16), v.T.astype(BF16), rank2, e2, nrow, e1)


def _pad_cols(w, n):
    return jnp.pad(w, ((0, 0), (0, n - w.shape[1])))


def kernel(x, p, mix_norm, ffn_norm, ple_norm, final_norm, ssd_w_in, ssd_conv_w, ssd_conv_b, ssd_dt_bias, ssd_a_log, ssd_d, ssd_norm, ssd_w_out, ret_w_in, ret_norm, ret_w_out, diff_w_in, diff_lambda, diff_norm, diff_w_out, fox_w_in, fox_b_f, fox_w_out, rel_bias, peer_w_q, peer_keys, peer_u, peer_v, ple_proj, ple_gate):
    B, S, D = x.shape
    T = B * S
    depth = mix_norm.shape[0]
    n_mixers = 4
    h = x.reshape(T, D)
    p_all = p.reshape(depth, T, PLE_DIM)
    for i in range(depth):
        m, j = i % n_mixers, i // n_mixers
        g = mix_norm[i]
        if m == 0:
            w = ssd_w_in[j]
            proj = norm_matmul(h, g, w[:, :SSD_MAIN].astype(BF16), BF16)
            dt_raw = norm_matmul(h, g, _pad_cols(w[:, SSD_MAIN:], LANE).astype(BF16), F32)
            y = ssd_core(proj.reshape(B, S, SSD_MAIN), dt_raw.reshape(B, S, LANE), ssd_conv_w[j],
                         ssd_conv_b[j], ssd_dt_bias[j], ssd_a_log[j], ssd_d[j], ssd_norm[j])
            w_out = ssd_w_out[j]
        elif m == 1:
            proj = norm_matmul(h, g, ret_w_in[j].astype(BF16), BF16)
            y = retention_core(proj.reshape(B, S, -1), ret_norm[j])
            w_out = ret_w_out[j]
        elif m == 2:
            lam_init = 0.8 - 0.6 * math.exp(-0.3 * i)
            proj = norm_matmul(h, g, diff_w_in[j].astype(BF16), BF16)
            y = diff_core(proj.reshape(B, S, -1), diff_lambda[j], diff_norm[j], rel_bias, lam_init)
            w_out = diff_w_out[j]
        else:
            w = fox_w_in[j]
            proj = norm_matmul(h, g, w[:, :3 * D].astype(BF16), BF16)
            c_t, c_rows = fox_gate(h.reshape(B, S, D), g, w[:, 3 * D:], fox_b_f[j])
            y = fox_core(proj.reshape(B, S, -1), c_t, c_rows)
            w_out = fox_w_out[j]
        h = matmul_residual(y.reshape(T, -1), w_out.astype(BF16), h)
        y = peer_layer(h, ffn_norm[i], peer_w_q[i], peer_keys[i], peer_u[i], peer_v[i])
        h = ple_layer(h, y, ple_norm[i], ple_gate[i].astype(BF16), p_all, i, ple_proj[i].astype(BF16),
                      final_norm, final=(i == depth - 1))
    return h.reshape(B, S, D)
```

```python
import functools
import math

import jax
import jax.numpy as jnp
import numpy as np
from jax import lax
from jax.experimental import pallas as pl
from jax.experimental.pallas import tpu as pltpu

F32 = jnp.float32
BF16 = jnp.bfloat16

D_MODEL = 1024
CHUNK = 64
NORM_EPS = 1e-6
LOG2E = math.log2(math.e)
ROPE_BASE = 10000.0
PLE_DIM = 256

SSD_D_INNER = 2 * D_MODEL
SSD_HEAD_DIM = 64
SSD_HEADS = SSD_D_INNER // SSD_HEAD_DIM
SSD_GROUPS = 4
SSD_HEADS_PER_GROUP = SSD_HEADS // SSD_GROUPS
SSD_STATE = 128
SSD_CONV = 4
SSD_CONV_DIM = SSD_D_INNER + 2 * SSD_GROUPS * SSD_STATE
SSD_MAIN = SSD_D_INNER + SSD_CONV_DIM
SSD_BLOCK = 128

RET_HEADS = 4
RET_QK_DIM = D_MODEL // RET_HEADS
RET_V_DIM = 2 * RET_QK_DIM
RET_V_WIDTH = RET_HEADS * RET_V_DIM
RET_BLOCK = 256

DIFF_HEADS = 8
DIFF_HEAD_DIM = D_MODEL // DIFF_HEADS // 2
FOX_HEADS = 16
FOX_HEAD_DIM = D_MODEL // FOX_HEADS
FOX_GATE_BLOCK = 512
ATTN_TILE = 512
ATTN_QUERY_TILES = 2
HEAD_LANES = 128

REL_BUCKETS = 32
REL_MAX_DIST = 128

PEER_KEYS = 128
PEER_EXPERTS = PEER_KEYS * PEER_KEYS
PEER_HEADS = 8
PEER_TOPK = 16
PEER_QUERY_DIM = 256
PEER_HALF = PEER_QUERY_DIM // 2
PEER_TOKENS = 1024
PEER_TOKEN_CHUNK = 512
PEER_EXPERT_TILE = 1024
PEER_SELECT_TOKENS = 1024
PEER_DENSE_VMEM = 56 * 1024 * 1024

LANE = 128
SUBLANE = 8
BF16_ROWS = 2 * SUBLANE
VMEM_LIMIT = 48 * 1024 * 1024

_NT = (((1,), (1,)), ((), ()))


def _params(*sem):
    return pltpu.CompilerParams(dimension_semantics=sem, vmem_limit_bytes=VMEM_LIMIT)


def _rms(x, g):
    return x * lax.rsqrt(jnp.mean(x * x, axis=-1, keepdims=True) + NORM_EPS) * g


def _split3(x):
    hi = x.astype(BF16)
    r1 = x - hi.astype(F32)
    mid = r1.astype(BF16)
    lo = (r1 - mid.astype(F32)).astype(BF16)
    return hi, mid, lo


def _dot_exact_lhs01(m01, x):
    hi, mid, lo = _split3(x)
    d = functools.partial(jnp.dot, preferred_element_type=F32)
    return (d(m01, hi) + d(m01, mid)) + d(m01, lo)


def _norm_matmul_kernel(h_ref, g_ref, w_ref, o_ref, *rest, emit_t):
    if emit_t:
        xt_ref, xn_ref = rest
    else:
        (xn_ref,) = rest

    @pl.when(pl.program_id(1) == 0)
    def _():
        y = _rms(h_ref[...], g_ref[...])
        xn_ref[...] = y.astype(BF16)
        if emit_t:
            tc = xt_ref.shape[2]
            for c in range(xt_ref.shape[0]):
                xt_ref[c] = y[c * tc:(c + 1) * tc, :].T.astype(BF16)

    o_ref[...] = jnp.dot(xn_ref[...], w_ref[...], preferred_element_type=F32).astype(o_ref.dtype)


def norm_matmul(h, g, w, out_dtype, emit_t=False, tm=1024, tn=1024):
    T, D = h.shape
    N = w.shape[1]
    tn = min(tn, N)
    assert T % tm == 0 and N % tn == 0
    out_shape = [jax.ShapeDtypeStruct((T, N), out_dtype)]
    out_specs = [pl.BlockSpec((tm, tn), lambda i, j: (i, j))]
    if emit_t:
        tc = PEER_TOKEN_CHUNK
        out_shape.append(jax.ShapeDtypeStruct((T // tc, D, tc), BF16))
        out_specs.append(pl.BlockSpec((tm // tc, D, tc), lambda i, j: (i, 0, 0)))
    res = pl.pallas_call(
        functools.partial(_norm_matmul_kernel, emit_t=emit_t),
        grid=(T // tm, N // tn),
        in_specs=[pl.BlockSpec((tm, D), lambda i, j: (i, 0)),
                  pl.BlockSpec((1, D), lambda i, j: (0, 0)),
                  pl.BlockSpec((D, tn), lambda i, j: (0, j))],
        out_specs=out_specs,
        out_shape=out_shape,
        scratch_shapes=[pltpu.VMEM((tm, D), BF16)],
        compiler_params=_params("parallel", "arbitrary"),
        name="norm_matmul",
    )(h, g.reshape(1, D), w)
    return res if emit_t else res[0]


def _matmul_residual_kernel(a_ref, w_ref, h_ref, o_ref):
    o_ref[...] = h_ref[...] + jnp.dot(a_ref[...], w_ref[...], preferred_element_type=F32)


def matmul_residual(a, w, h, tm=512):
    T, K = a.shape
    N = w.shape[1]
    return pl.pallas_call(
        _matmul_residual_kernel,
        grid=(T // tm,),
        in_specs=[pl.BlockSpec((tm, K), lambda i: (i, 0)),
                  pl.BlockSpec((K, N), lambda i: (0, 0)),
                  pl.BlockSpec((tm, N), lambda i: (i, 0))],
        out_specs=pl.BlockSpec((tm, N), lambda i: (i, 0)),
        out_shape=jax.ShapeDtypeStruct((T, N), F32),
        input_output_aliases={2: 0},
        compiler_params=_params("parallel"),
        name="matmul_residual",
    )(a, w, h)


def _ple_kernel(h_ref, y_ref, g_ref, wg_ref, p_ref, wp_ref, fg_ref, o_ref, *, final):
    x = h_ref[...] + y_ref[...]
    xn = _rms(x, g_ref[...]).astype(BF16)
    gate = jax.nn.sigmoid(jnp.dot(xn, wg_ref[...], preferred_element_type=F32))
    proj = jnp.dot(p_ref[...].astype(BF16), wp_ref[...], preferred_element_type=F32)
    y = x + gate * proj
    if final:
        y = _rms(y, fg_ref[...])
    o_ref[...] = y


def ple_layer(h, y, g, w_gate, p_all, layer, w_proj, final_g, final, tm=512):
    T, D = h.shape
    return pl.pallas_call(
        functools.partial(_ple_kernel, final=final),
        grid=(T // tm,),
        in_specs=[pl.BlockSpec((tm, D), lambda i: (i, 0)),
                  pl.BlockSpec((tm, D), lambda i: (i, 0)),
                  pl.BlockSpec((1, D), lambda i: (0, 0)),
                  pl.BlockSpec((D, D), lambda i: (0, 0)),
                  pl.BlockSpec((None, tm, PLE_DIM), lambda i: (layer, i, 0)),
                  pl.BlockSpec((PLE_DIM, D), lambda i: (0, 0)),
                  pl.BlockSpec((1, D), lambda i: (0, 0))],
        out_specs=pl.BlockSpec((tm, D), lambda i: (i, 0)),
        out_shape=jax.ShapeDtypeStruct((T, D), F32),
        input_output_aliases={0: 0},
        compiler_params=_params("parallel"),
        name="ple_layer",
    )(h, y, g.reshape(1, D), w_gate, p_all, w_proj, final_g.reshape(1, D))


def _ssd_kernel(proj_ref, dt_ref, cw_ref, cb_ref, dtb_ref, alog_ref, dsk_ref, ng_ref, ewide_ref, efeat_ref,
                o_ref, xbuf, state, ybuf, ibuf, cwide, cfeat, xw_s, *, L):
    DI, P, N, R = SSD_D_INNER, SSD_HEAD_DIM, SSD_STATE, SSD_HEADS_PER_GROUP
    GW = DI // SSD_GROUPS
    PAIR = 2 * P
    assert PAIR == LANE and L == LANE

    @pl.when(pl.program_id(1) == 0)
    def _():
        xbuf[0:8, :] = jnp.zeros((8, SSD_CONV_DIM), F32)
        state[...] = jnp.zeros(state.shape, F32)

    xbc = proj_ref[0, :, DI:].astype(F32)
    xbuf[8:8 + L, :] = xbc
    conv = cb_ref[...] + cw_ref[3:4, :] * xbc
    for j in range(1, SSD_CONV):
        conv = conv + cw_ref[SSD_CONV - 1 - j:SSD_CONV - j, :] * xbuf[8 - j:8 - j + L, :]
    xbuf[0:8, :] = xbuf[L:L + 8, :]
    act = conv * jax.nn.sigmoid(conv)
    xs = act[:, :DI]
    bm = act[:, DI:DI + SSD_GROUPS * N]
    cm = act[:, DI + SSD_GROUPS * N:]

    dt = jax.nn.softplus(dt_ref[0] + dtb_ref[...])
    a = dt * (-jnp.exp(alog_ref[...]))
    row = lax.broadcasted_iota(jnp.int32, (L, L), 0)
    col = lax.broadcasted_iota(jnp.int32, (L, L), 1)
    tril = row >= col
    cum = _dot_exact_lhs01(tril.astype(BF16), a)
    cum_t = cum.T
    dt_t = dt.T

    split = lambda x: jnp.concatenate(_split3(x), axis=1)
    cum3 = split(cum)
    cwide[...] = jnp.dot(cum3, ewide_ref[...], preferred_element_type=F32)
    cfeat[...] = jnp.dot(cum3, efeat_ref[...], preferred_element_type=F32)
    dfeat = jnp.dot(split(dt), efeat_ref[...], preferred_element_type=F32)
    last = cfeat[L - 1:L, :]
    xw_s[...] = (xs * (jnp.exp(last - cfeat[...]) * dfeat)).astype(BF16)
    elast = jnp.exp(last)
    xs_b = xs.astype(BF16)
    lane = lax.broadcasted_iota(jnp.int32, (1, PAIR), 1)
    lower = lane < P
    zero = jnp.zeros((L, PAIR), BF16)

    for g in range(SSD_GROUPS):
        bg = bm[:, g * N:(g + 1) * N]
        cg = cm[:, g * N:(g + 1) * N].astype(BF16)
        cb = lax.dot_general(cg, bg.astype(BF16), _NT, preferred_element_type=F32)
        bg_t = bg.T.astype(BF16)
        for pr in range(R // 2):
            blk = slice((g * (R // 2) + pr) * PAIR, (g * (R // 2) + pr + 1) * PAIR)
            xp = xs_b[:, blk]
            halves = (jnp.where(lower, xp, zero), jnp.where(lower, zero, xp))
            y = None
            for k in range(2):
                hd = g * R + 2 * pr + k
                seg = cwide[:, hd * LANE:(hd + 1) * LANE] - cum_t[hd:hd + 1, :]
                w = jnp.exp(jnp.where(tril, seg, -jnp.inf)) * cb * dt_t[hd:hd + 1, :]
                yk = jnp.dot(w.astype(BF16), halves[k], preferred_element_type=F32)
                y = yk if y is None else y + yk
            st = state[g, :, pr * PAIR:(pr + 1) * PAIR]
            ybuf[:, blk] = y
            ibuf[:, blk] = jnp.dot(cg, st.astype(BF16), preferred_element_type=F32)
            state[g, :, pr * PAIR:(pr + 1) * PAIR] = (st * elast[:, blk]
                                                      + jnp.dot(bg_t, xw_s[:, blk], preferred_element_type=F32))

    y = ybuf[...] + ibuf[...] * jnp.exp(cfeat[...]) + dsk_ref[...] * xs
    z = proj_ref[0, :, :DI].astype(F32)
    y = y * (z * jax.nn.sigmoid(z))
    for g in range(SSD_GROUPS):
        yg = y[:, g * GW:(g + 1) * GW]
        o_ref[0, :, g * GW:(g + 1) * GW] = _rms(yg, ng_ref[:, g * GW:(g + 1) * GW]).astype(o_ref.dtype)


def ssd_core(proj, dt_raw, conv_w, conv_b, dt_bias, a_log, d_skip, norm_g):
    B, S, _ = proj.shape
    L = SSD_BLOCK
    assert S % L == 0
    pad = LANE - SSD_HEADS
    vec = lambda n: pl.BlockSpec((1, n), lambda b, c: (0, 0))
    whole = lambda a: pl.BlockSpec(a.shape, lambda b, c: (0, 0))

    def spread(width):
        e = np.zeros((LANE, SSD_HEADS * width), np.float32)
        for h in range(SSD_HEADS):
            e[h, h * width:(h + 1) * width] = 1.0
        return jnp.asarray(np.concatenate([e, e, e], axis=0), BF16)

    e_wide, e_feat = spread(LANE), spread(SSD_HEAD_DIM)
    return pl.pallas_call(
        functools.partial(_ssd_kernel, L=L),
        grid=(B, S // L),
        in_specs=[pl.BlockSpec((1, L, SSD_MAIN), lambda b, c: (b, c, 0)),
                  pl.BlockSpec((1, L, LANE), lambda b, c: (b, c, 0)),
                  pl.BlockSpec((SSD_CONV, SSD_CONV_DIM), lambda b, c: (0, 0)),
                  vec(SSD_CONV_DIM), vec(LANE), vec(LANE), vec(SSD_D_INNER), vec(SSD_D_INNER),
                  whole(e_wide), whole(e_feat)],
        out_specs=pl.BlockSpec((1, L, SSD_D_INNER), lambda b, c: (b, c, 0)),
        out_shape=jax.ShapeDtypeStruct((B, S, SSD_D_INNER), BF16),
        scratch_shapes=[pltpu.VMEM((L + 8, SSD_CONV_DIM), F32),
                        pltpu.VMEM((SSD_GROUPS, SSD_STATE, SSD_D_INNER // SSD_GROUPS), F32),
                        pltpu.VMEM((L, SSD_D_INNER), F32), pltpu.VMEM((L, SSD_D_INNER), F32),
                        pltpu.VMEM((L, SSD_HEADS * LANE), F32), pltpu.VMEM((L, SSD_D_INNER), F32),
                        pltpu.VMEM((L, SSD_D_INNER), BF16)],
        compiler_params=_params("parallel", "arbitrary"),
        name="ssd_core",
    )(proj, dt_raw,
      conv_w.reshape(SSD_CONV, SSD_CONV_DIM), conv_b.reshape(1, SSD_CONV_DIM),
      jnp.pad(dt_bias, (0, pad)).reshape(1, LANE), jnp.pad(a_log, (0, pad)).reshape(1, LANE),
      jnp.repeat(d_skip, SSD_HEAD_DIM).reshape(1, SSD_D_INNER), norm_g.reshape(1, SSD_D_INNER),
      e_wide, e_feat)


def _retention_kernel(q_ref, k_ref, v_ref, gate_ref, cos_ref, sin_ref, dec_ref, qd_ref, kd_ref,
                      cd_ref, ng_ref, o_ref, state):
    half = RET_QK_DIM // 2

    @pl.when(pl.program_id(2) == 0)
    def _():
        state[...] = jnp.zeros(state.shape, F32)

    cos = cos_ref[...]
    sin = sin_ref[...]

    def rot(x):
        x1, x2 = x[:, :half], x[:, half:]
        return jnp.concatenate([x1 * cos - x2 * sin, x1 * sin + x2 * cos], axis=-1)

    q = rot(q_ref[0].astype(F32))
    k = rot(k_ref[0].astype(F32)) * (RET_QK_DIM ** -0.5)
    v = v_ref[0]
    qb = q.astype(BF16)
    s = lax.dot_general(qb, k.astype(BF16), _NT, preferred_element_type=F32) * dec_ref[0]
    o = jnp.dot(s.astype(BF16), v, preferred_element_type=F32)
    st = state[...]
    o = o + jnp.dot(qb, st.astype(BF16), preferred_element_type=F32) * qd_ref[0]
    kt = (k * kd_ref[0]).T.astype(BF16)
    state[...] = st * cd_ref[0] + jnp.dot(kt, v, preferred_element_type=F32)
    gate = gate_ref[0].astype(F32)
    o_ref[0] = (_rms(o, ng_ref[0]) * (gate * jax.nn.sigmoid(gate))).astype(o_ref.dtype)


def retention_core(proj, norm_g):
    B, S, _ = proj.shape
    H, dk, dv, L = RET_HEADS, RET_QK_DIM, RET_V_DIM, RET_BLOCK
    assert S % L == 0 and L % CHUNK == 0
    inv = 1.0 / (ROPE_BASE ** (jnp.arange(0, dk, 2, dtype=F32) / dk))
    ang = jnp.arange(S, dtype=F32)[:, None] * inv[None, :]
    log_gamma = jnp.log1p(-jnp.exp2(-5.0 - jnp.arange(H, dtype=F32)))
    idx = jnp.arange(L, dtype=F32)
    visible = (jnp.arange(L)[None, :] // CHUNK) <= (jnp.arange(L)[:, None] // CHUNK)
    decay = jnp.where(visible[None],
                      jnp.exp(log_gamma[:, None, None] * jnp.abs(idx[:, None] - idx[None, :])), 0.0)
    q_decay = jnp.exp(log_gamma[:, None] * (idx[None, :] + 1.0))[..., None]
    k_decay = jnp.exp(log_gamma[:, None] * (L - 1.0 - idx[None, :]))[..., None]
    block_decay = jnp.exp(log_gamma * L).reshape(H, 1, 1)
    return pl.pallas_call(
        _retention_kernel,
        grid=(B, H, S // L),
        in_specs=[pl.BlockSpec((1, L, dk), lambda b, h, c: (b, c, h)),
                  pl.BlockSpec((1, L, dk), lambda b, h, c: (b, c, H + h)),
                  pl.BlockSpec((1, L, dv), lambda b, h, c: (b, c, H + h)),
                  pl.BlockSpec((1, L, dv), lambda b, h, c: (b, c, 2 * H + h)),
                  pl.BlockSpec((L, dk // 2), lambda b, h, c: (c, 0)),
                  pl.BlockSpec((L, dk // 2), lambda b, h, c: (c, 0)),
                  pl.BlockSpec((1, L, L), lambda b, h, c: (h, 0, 0)),
                  pl.BlockSpec((1, L, 1), lambda b, h, c: (h, 0, 0)),
                  pl.BlockSpec((1, L, 1), lambda b, h, c: (h, 0, 0)),
                  pl.BlockSpec((1, 1, 1), lambda b, h, c: (h, 0, 0)),
                  pl.BlockSpec((1, 1, dv), lambda b, h, c: (h, 0, 0))],
        out_specs=pl.BlockSpec((1, L, dv), lambda b, h, c: (b, c, h)),
        out_shape=jax.ShapeDtypeStruct((B, S, RET_V_WIDTH), BF16),
        scratch_shapes=[pltpu.VMEM((dk, dv), F32)],
        compiler_params=_params("parallel", "parallel", "arbitrary"),
        name="retention_core",
    )(proj, proj, proj, proj, jnp.cos(ang), jnp.sin(ang), decay, q_decay, k_decay, block_decay,
      norm_g.reshape(H, 1, dv))


def _attn_kernel(*refs, mode, T):
    if mode == "diff":
        q_ref, k_ref, v_ref, bias_ref, lam_ref, ng_ref, o_ref, m_s, acc_s, l_s = refs
    else:
        q_ref, k_ref, v_ref, ck_ref, cq_ref, o_ref, m_s, acc_s, cq_s = refs
    i = pl.program_id(2)
    NQ = ATTN_QUERY_TILES
    half = HEAD_LANES // 2
    reps = T // HEAD_LANES
    lane = lax.broadcasted_iota(jnp.int32, (1, HEAD_LANES), 1)
    first = lane < half
    second = jnp.logical_not(first)
    qs = []
    for u in range(NQ):
        q = (q_ref[0, u * T:(u + 1) * T, :].astype(F32) * (half ** -0.5 * LOG2E)).astype(BF16)
        zero = jnp.zeros_like(q)
        qs.append((jnp.where(first, q, zero), jnp.where(first, zero, q)))

    m_s[...] = jnp.full(m_s.shape, -jnp.inf, F32)
    acc_s[...] = jnp.zeros(acc_s.shape, F32)
    if mode == "diff":
        l_s[...] = jnp.zeros(l_s.shape, F32)
    else:
        for u in range(NQ):
            for a in range(2):
                cq_s[u, a] = jnp.broadcast_to(cq_ref[0, 0, u * T:(u + 1) * T, a:a + 1], (T, T))

    def step(j, subtiles, diagonal):
        start = pl.multiple_of(j * T, T)
        k = k_ref[0, pl.ds(start, T), :]
        v = v_ref[0, pl.ds(start, T), :]
        if mode == "diff":
            vs = (v, v)
        else:
            ones = jnp.ones_like(v)
            vs = (jnp.where(first, v, ones), jnp.where(second, v, ones))
        if diagonal is not None:
            row = lax.broadcasted_iota(jnp.int32, (T, T), 0)
            col = lax.broadcasted_iota(jnp.int32, (T, T), 1)
            if mode == "diff":
                visible = (col // CHUNK) <= (row // CHUNK)
            else:
                visible = col <= row
        for u in subtiles:
            for a in range(2):
                s = lax.dot_general(qs[u][a], k, _NT, preferred_element_type=F32)
                if mode == "diff":
                    s = s + bias_ref[0, jnp.minimum(NQ * i + u - j, 2)]
                else:
                    s = (s - ck_ref[0, 0, j, a:a + 1, :]) + cq_s[u, a]
                if u == diagonal:
                    s = jnp.where(visible, s, -jnp.inf)
                m_prev = m_s[u, a]
                m_new = jnp.maximum(m_prev, jnp.max(s, axis=-1, keepdims=True))
                alpha = jnp.exp2(m_prev - m_new)
                p = jnp.exp2(s - jnp.concatenate([m_new] * reps, axis=1))
                if mode == "diff":
                    l_s[u, a] = alpha * l_s[u, a] + jnp.sum(p, axis=-1, keepdims=True)
                acc_s[u, a] = alpha * acc_s[u, a] + jnp.dot(p.astype(BF16), vs[a], preferred_element_type=F32)
                m_s[u, a] = m_new

    everyone = tuple(range(NQ))
    lax.fori_loop(0, NQ * i, lambda j, c: (step(j, everyone, None), c)[1], 0)
    for d in range(NQ):
        step(NQ * i + d, everyone[d:], d)

    for u in range(NQ):
        if mode == "diff":
            o = acc_s[u, 0] / l_s[u, 0] - lam_ref[0] * (acc_s[u, 1] / l_s[u, 1])
            o = _rms(o, ng_ref[...])
        else:
            o0, o1 = acc_s[u, 0], acc_s[u, 1]
            o = jnp.where(first, o0 / pltpu.roll(o0, half, 1), o1 / pltpu.roll(o1, half, 1))
        o_ref[0, u * T:(u + 1) * T, :] = o.astype(o_ref.dtype)


def _attn_call(mode, proj, n_blocks, extra_inputs, extra_specs):
    B, S, _ = proj.shape
    T, NQ = ATTN_TILE, ATTN_QUERY_TILES
    assert S % (NQ * T) == 0
    stats = pltpu.VMEM((NQ, 2, T, HEAD_LANES), F32)
    return pl.pallas_call(
        functools.partial(_attn_kernel, mode=mode, T=T),
        grid=(B, n_blocks, S // (NQ * T)),
        in_specs=[pl.BlockSpec((1, NQ * T, HEAD_LANES), lambda b, h, i: (b, i, h)),
                  pl.BlockSpec((1, S, HEAD_LANES), lambda b, h, i: (b, 0, n_blocks + h)),
                  pl.BlockSpec((1, S, HEAD_LANES), lambda b, h, i: (b, 0, 2 * n_blocks + h))] + extra_specs,
        out_specs=pl.BlockSpec((1, NQ * T, HEAD_LANES), lambda b, h, i: (b, i, h)),
        out_shape=jax.ShapeDtypeStruct((B, S, D_MODEL), BF16),
        scratch_shapes=[stats, stats, stats if mode == "diff" else pltpu.VMEM((NQ, 2, T, T), F32)],
        compiler_params=_params("parallel", "parallel", "arbitrary"),
        name=mode + "_attention",
    )(proj, proj, proj, *extra_inputs)


def _t5_bucket(rel):
    nb = REL_BUCKETS // 2
    max_exact = nb // 2
    ret = (rel > 0).astype(jnp.int32) * nb
    n = jnp.abs(rel)
    nf = jnp.maximum(n, 1).astype(F32)
    large = max_exact + (jnp.log(nf / max_exact) / math.log(REL_MAX_DIST / max_exact)
                         * (nb - max_exact)).astype(jnp.int32)
    large = jnp.minimum(large, nb - 1)
    return ret + jnp.where(n < max_exact, n, large)


def _bias_tiles_kernel(bucket_ref, table_ref, o_ref):
    h = pl.program_id(0)
    for d in range(bucket_ref.shape[0]):
        bucket = bucket_ref[d]
        tile = jnp.zeros(bucket.shape, F32)
        for b in range(REL_BUCKETS):
            tile = jnp.where(bucket == b, table_ref[b * DIFF_HEADS + h] * LOG2E, tile)
        o_ref[0, d] = tile


def diff_bias_tiles(rel_bias):
    T = ATTN_TILE
    assert T >= REL_MAX_DIST
    off = jnp.arange(T)
    rel = (off[None, None, :] - off[None, :, None]) - (jnp.arange(3) * T)[:, None, None]
    return pl.pallas_call(
        _bias_tiles_kernel,
        grid=(DIFF_HEADS,),
        in_specs=[pl.BlockSpec((3, T, T), lambda h: (0, 0, 0)),
                  pl.BlockSpec(memory_space=pltpu.SMEM)],
        out_specs=pl.BlockSpec((1, 3, T, T), lambda h: (h, 0, 0, 0)),
        out_shape=jax.ShapeDtypeStruct((DIFF_HEADS, 3, T, T), F32),
        compiler_params=_params("parallel"),
        name="diff_bias_tiles",
    )(_t5_bucket(rel), rel_bias.astype(F32).reshape(REL_BUCKETS * DIFF_HEADS))


def diff_core(proj, lam_vecs, norm_g, rel_bias, lam_init):
    T = ATTN_TILE
    lv = lam_vecs.astype(F32)
    lam = jnp.exp(jnp.sum(lv[0] * lv[1])) - jnp.exp(jnp.sum(lv[2] * lv[3])) + lam_init
    bias = diff_bias_tiles(rel_bias)
    g = (norm_g * (1.0 - lam_init)).reshape(1, HEAD_LANES)
    extra_specs = [pl.BlockSpec((1, 3, T, T), lambda b, h, i: (h, 0, 0, 0)),
                   pl.BlockSpec(memory_space=pltpu.SMEM),
                   pl.BlockSpec((1, HEAD_LANES), lambda b, h, i: (0, 0))]
    return _attn_call("diff", proj, DIFF_HEADS, [bias, lam.reshape(1), g], extra_specs)


def _fox_gate_kernel(h_ref, g_ref, w_ref, b_ref, ct_ref, c_ref, carry, *, L):
    @pl.when(pl.program_id(1) == 0)
    def _():
        carry[...] = jnp.zeros(carry.shape, F32)

    xn = _rms(h_ref[0], g_ref[...])
    logits = lax.dot_general(w_ref[...], xn.astype(BF16), _NT, preferred_element_type=F32)
    log_f = jax.nn.log_sigmoid(logits + b_ref[...])
    row = lax.broadcasted_iota(jnp.int32, (L, L), 0)
    col = lax.broadcasted_iota(jnp.int32, (L, L), 1)
    triu = (row <= col).astype(BF16)
    hi, mid, lo = _split3(log_f)
    d = functools.partial(jnp.dot, preferred_element_type=F32)
    c = carry[...] + ((d(hi, triu) + d(mid, triu)) + d(lo, triu))
    carry[...] = c[:, L - 1:L]
    c2 = c * LOG2E
    ct_ref[0] = c2
    c_ref[0] = c2.T


def fox_gate(h3, g, w_f, b_f):
    B, S, D = h3.shape
    L = FOX_GATE_BLOCK
    assert S % L == 0
    pad = LANE - FOX_HEADS
    return pl.pallas_call(
        functools.partial(_fox_gate_kernel, L=L),
        grid=(B, S // L),
        in_specs=[pl.BlockSpec((1, L, D), lambda b, c: (b, c, 0)),
                  pl.BlockSpec((1, D), lambda b, c: (0, 0)),
                  pl.BlockSpec((LANE, D), lambda b, c: (0, 0)),
                  pl.BlockSpec((LANE, 1), lambda b, c: (0, 0))],
        out_specs=[pl.BlockSpec((1, LANE, L), lambda b, c: (b, 0, c)),
                   pl.BlockSpec((1, L, LANE), lambda b, c: (b, c, 0))],
        out_shape=[jax.ShapeDtypeStruct((B, LANE, S), F32), jax.ShapeDtypeStruct((B, S, LANE), F32)],
        scratch_shapes=[pltpu.VMEM((LANE, 1), F32)],
        compiler_params=_params("parallel", "arbitrary"),
        name="fox_gate",
    )(h3, g.reshape(1, D), jnp.pad(w_f.T, ((0, pad), (0, 0))).astype(BF16),
      jnp.pad(b_f, (0, pad)).reshape(LANE, 1))


def fox_core(proj, c_t, c_rows):
    B, S, _ = proj.shape
    T = ATTN_TILE
    nb = FOX_HEADS // 2
    ck = jnp.transpose(c_t[:, :FOX_HEADS].reshape(B, nb, 2, S // T, T), (0, 1, 3, 2, 4))
    cq = jnp.transpose(c_rows[:, :, :FOX_HEADS].reshape(B, S, nb, 2), (0, 2, 1, 3))
    extra_specs = [pl.BlockSpec((1, 1, S // T, 2, T), lambda b, h, i: (b, h, 0, 0, 0)),
                   pl.BlockSpec((1, 1, ATTN_QUERY_TILES * T, 2), lambda b, h, i: (b, h, i, 0))]
    return _attn_call("fox", proj, nb, [ck, cq], extra_specs)


def _candidate_tables():
    pairs = [(a, b) for a in range(PEER_TOPK) for b in range(PEER_TOPK) if (a + 1) * (b + 1) <= PEER_TOPK]
    rows = -(-len(pairs) // BF16_ROWS) * BF16_ROWS
    p1 = np.zeros((rows, LANE), np.float32)
    p2 = np.zeros((rows, LANE), np.float32)
    for r, (a, b) in enumerate(pairs):
        p1[r, a] = 1.0
        p2[r, b] = 1.0
    return len(pairs), p1, p2


N_CAND, _P1, _P2 = _candidate_tables()


_INT_MIN = -2 ** 31
_FLIP = 0x7FFFFFFF


def _ordered_int(x):
    b = lax.bitcast_convert_type(x + 0.0, jnp.int32)
    return jnp.where(b < 0, b ^ _FLIP, b)


def _ordered_float(k):
    return lax.bitcast_convert_type(jnp.where(k < 0, k ^ _FLIP, k), F32)


_FLOOR = -3.0e38
_MARK, _MARK_STEP = -3.2e38, 1.0e36


def _top16_ranks(x, break_ties):
    n, t = x.shape
    idx = lax.broadcasted_iota(jnp.int32, (n, t), 0)
    ridx = lax.broadcasted_iota(jnp.int32, (PEER_TOPK, t), 0)
    work = _ordered_int(x) if break_ties else jnp.maximum(x, _FLOOR)
    vals = jnp.zeros((PEER_TOPK, t), work.dtype)
    for r in range(PEER_TOPK):
        mx = jnp.max(work, axis=0, keepdims=True)
        hit = work == mx
        if break_ties:
            pos = jnp.min(jnp.where(hit, idx, n), axis=0, keepdims=True)
            hit = idx == pos
        work = jnp.where(hit, _INT_MIN + r if break_ties else _MARK - r * _MARK_STEP, work)
        vals = jnp.where(ridx == r, mx, vals)
    if break_ties:
        rank = jnp.where(work < _INT_MIN + PEER_TOPK, work & (2 * PEER_TOPK - 1), PEER_TOPK)
        return rank, _ordered_float(vals)
    rank = jnp.where(work < 0.5 * (_FLOOR + _MARK),
                     jnp.round((_MARK - work) * (1.0 / _MARK_STEP)).astype(jnp.int32), PEER_TOPK)
    return rank, vals


def _peer_select_kernel(q_ref, keys_ref, p1_ref, p2_ref, p1t_ref, rank2_ref, e2_ref, nrow_ref, e1_ref):
    def scores(c):
        kk = keys_ref[0, c]
        qq = q_ref[:, c * PEER_HALF:(c + 1) * PEER_HALF]
        kh, km, _ = _split3(kk)
        qh, qm, _ = _split3(qq)
        d = lambda a, b: lax.dot_general(a, b, _NT, preferred_element_type=F32)
        return d(kh, qh) + (d(kh, qm) + d(km, qh))

    s1 = scores(0)
    s2 = scores(1)
    out_refs = (rank2_ref, e2_ref, nrow_ref, e1_ref)
    clean = _peer_select_pass(s1, s2, p1_ref, p2_ref, p1t_ref, out_refs, break_ties=False)

    @pl.when(jnp.logical_not(clean))
    def _():
        _peer_select_pass(s1, s2, p1_ref, p2_ref, p1t_ref, out_refs, break_ties=True)


def _peer_select_pass(s1, s2, p1_ref, p2_ref, p1t_ref, out_refs, break_ties):
    rank2_ref, e2_ref, nrow_ref, e1_ref = out_refs
    rank1, v1 = _top16_ranks(s1, break_ties)
    rank2, v2 = _top16_ranks(s2, break_ties)

    tokens = s1.shape[1]
    pad = jnp.zeros((LANE - PEER_TOPK, tokens), F32)
    cand = (_dot_exact_lhs01(p1_ref[...], jnp.concatenate([v1, pad], axis=0))
            + _dot_exact_lhs01(p2_ref[...], jnp.concatenate([v2, pad], axis=0)))
    cidx = lax.broadcasted_iota(jnp.int32, cand.shape, 0)
    cand = jnp.where(cidx < N_CAND, cand, -jnp.inf)
    cand_rank, _ = _top16_ranks(cand, break_ties)
    sel = jnp.where(cand_rank < PEER_TOPK, 1.0, 0.0)
    top = v1[0:1, :] + v2[0:1, :]
    z = jnp.sum(sel * jnp.exp(jnp.where(sel > 0.0, cand - top, 0.0)), axis=0, keepdims=True)
    sel_pad = jnp.concatenate([sel, jnp.zeros((LANE - sel.shape[0], tokens), F32)], axis=0).astype(BF16)
    n_by_rank = jnp.dot(p1t_ref[...], sel_pad, preferred_element_type=F32)
    nrow = jnp.zeros(s1.shape, F32)
    for a in range(PEER_TOPK):
        nrow = jnp.where(rank1 == a, n_by_rank[a:a + 1, :], nrow)

    rank2_ref[0] = rank2.astype(F32).astype(BF16)
    e2_ref[0] = jnp.exp(s2 - v2[0:1, :]).astype(BF16)
    e1 = jnp.exp(s1 - v1[0:1, :]) * (0.5 / z)
    for g in range(PEER_KEYS // SUBLANE):
        for st in range(tokens // LANE):
            tile = (slice(g * SUBLANE, (g + 1) * SUBLANE), slice(st * LANE, (st + 1) * LANE))
            nrow_ref[0, g, st] = nrow[tile]
            e1_ref[0, g, st] = e1[tile]

    if break_ties:
        return None
    ranked = (jnp.sum(jnp.where(rank1 < PEER_TOPK, 1.0, 0.0), axis=0, keepdims=True)
              + jnp.sum(jnp.where(rank2 < PEER_TOPK, 1.0, 0.0), axis=0, keepdims=True)
              + jnp.sum(sel, axis=0, keepdims=True))
    return jnp.max(ranked) == 3.0 * PEER_TOPK


def peer_select(q, keys):
    T = q.shape[0]
    tm = PEER_SELECT_TOKENS
    assert T % tm == 0
    hk = pl.BlockSpec((1, PEER_KEYS, tm), lambda i, h: (h, 0, i))
    groups = PEER_KEYS // SUBLANE
    hk1 = pl.BlockSpec((1, groups, tm // LANE, SUBLANE, LANE), lambda i, h: (h, 0, i, 0, 0))
    shp = lambda dt: jax.ShapeDtypeStruct((PEER_HEADS, PEER_KEYS, T), dt)
    shp1 = jax.ShapeDtypeStruct((PEER_HEADS, groups, T // LANE, SUBLANE, LANE), F32)
    cst = lambda a: pl.BlockSpec(a.shape, lambda i, h: (0, 0))
    p1, p2 = jnp.asarray(_P1, BF16), jnp.asarray(_P2, BF16)
    p1t = jnp.asarray(np.pad(_P1.T, ((0, 0), (0, LANE - _P1.shape[0]))), BF16)
    return pl.pallas_call(
        _peer_select_kernel,
        grid=(T // tm, PEER_HEADS),
        in_specs=[pl.BlockSpec((tm, PEER_QUERY_DIM), lambda i, h: (i, h)),
                  pl.BlockSpec((1, 2, PEER_KEYS, PEER_HALF), lambda i, h: (h, 0, 0, 0)),
                  cst(p1), cst(p2), cst(p1t)],
        out_specs=[hk, hk, hk1, hk1],
        out_shape=[shp(BF16), shp(BF16), shp1, shp1],
        compiler_params=_params("parallel", "parallel"),
        name="peer_select",
    )(q, keys, p1, p2, p1t)


def _peer_dense_kernel(xt_ref, u_ref, u_next_ref, vt_ref, rank2_ref, e2_ref, nrow_ref, e1_ref,
                       nrow_next_ref, e1_next_ref, o_ref, *scratch):
    n = xt_ref.shape[0]
    acc, gbuf, hbuf, wbuf = scratch[:n], scratch[n:2 * n], scratch[2 * n:2 * n + 2], scratch[2 * n + 2:]
    j = pl.program_id(1)
    rows = PEER_EXPERT_TILE // PEER_KEYS
    tc = PEER_TOKEN_CHUNK
    n_chunks = xt_ref.shape[0]
    pack = BF16_ROWS

    def first_matmul(c, u):
        hbuf[c % 2][...] = jnp.dot(u[...], xt_ref[c], preferred_element_type=F32).astype(BF16)

    def gate_weights(c, nrow, e1):
        cols = slice(c * tc, (c + 1) * tc)
        for ii in range(rows):
            w = jnp.zeros((PEER_KEYS, tc), BF16)
            for h in range(PEER_HEADS):
                def spread(ref):
                    tiles = [jnp.broadcast_to(ref[h, 0, c * (tc // LANE) + lt, ii:ii + 1, :],
                                              (pack, LANE)).astype(BF16) for lt in range(tc // LANE)]
                    return jnp.concatenate([jnp.concatenate(tiles, axis=1)] * (PEER_KEYS // pack), axis=0)
                w = w + jnp.where(rank2_ref[h, :, cols] < spread(nrow), e2_ref[h, :, cols] * spread(e1),
                                  jnp.zeros_like(w))
            wbuf[c % 2][ii * PEER_KEYS:(ii + 1) * PEER_KEYS, :] = w

    def finish(c):
        for ii in range(rows):
            r = slice(ii * PEER_KEYS, (ii + 1) * PEER_KEYS)
            hr = hbuf[c % 2][r, :]
            act = hr * (1.0 + lax.erf(hr * jnp.asarray(2.0 ** -0.5, BF16)))
            gbuf[c][r, :] = wbuf[c % 2][r, :] * act
        acc[c][...] += jnp.dot(vt_ref[...], gbuf[c][...], preferred_element_type=F32)

    @pl.when(j == 0)
    def _():
        for a in acc:
            a[...] = jnp.zeros(a.shape, F32)
        first_matmul(0, u_ref)
        gate_weights(0, nrow_ref, e1_ref)

    for c in range(n_chunks):
        if c + 1 < n_chunks:
            first_matmul(c + 1, u_ref)
            gate_weights(c + 1, nrow_ref, e1_ref)
        else:
            first_matmul(0, u_next_ref)
            gate_weights(0, nrow_next_ref, e1_next_ref)
        finish(c)

    @pl.when(j == pl.num_programs(1) - 1)
    def _():
        for c in range(n_chunks):
            o_ref[c * tc:(c + 1) * tc, :] = acc[c][...].T


def peer_dense(xt, u, vt, rank2, e2, nrow, e1):
    n_slabs, D, tc = xt.shape
    T = n_slabs * tc
    tm, te = PEER_TOKENS, PEER_EXPERT_TILE
    rows = te // PEER_KEYS
    n_tiles = PEER_EXPERTS // te
    assert rows == SUBLANE and tc == PEER_TOKEN_CHUNK and (tm // tc) % 2 == 0
    assert T % tm == 0
    nxt = lambda j: jnp.minimum(j + 1, n_tiles - 1)
    by_key2 = pl.BlockSpec((PEER_HEADS, PEER_KEYS, tm), lambda i, j: (0, 0, i))
    by_key1 = pl.BlockSpec((PEER_HEADS, 1, tm // LANE, rows, LANE), lambda i, j: (0, j, i, 0, 0))
    by_key1_next = pl.BlockSpec((PEER_HEADS, 1, tm // LANE, rows, LANE), lambda i, j: (0, nxt(j), i, 0, 0))
    return pl.pallas_call(
        _peer_dense_kernel,
        grid=(T // tm, n_tiles),
        in_specs=[pl.BlockSpec((tm // tc, D, tc), lambda i, j: (i, 0, 0)),
                  pl.BlockSpec((te, D), lambda i, j: (j, 0)),
                  pl.BlockSpec((te, D), lambda i, j: (nxt(j), 0)),
                  pl.BlockSpec((D, te), lambda i, j: (0, j)),
                  by_key2, by_key2, by_key1, by_key1, by_key1_next, by_key1_next],
        out_specs=pl.BlockSpec((tm, D), lambda i, j: (i, 0)),
        out_shape=jax.ShapeDtypeStruct((T, D), F32),
        scratch_shapes=([pltpu.VMEM((D, tc), F32)] * (tm // tc) + [pltpu.VMEM((te, tc), BF16)] * (tm // tc)
                        + [pltpu.VMEM((te, tc), BF16)] * 4),
        compiler_params=pltpu.CompilerParams(dimension_semantics=("parallel", "arbitrary"),
                                             vmem_limit_bytes=PEER_DENSE_VMEM),
        name="peer_dense",
    )(xt, u, u, vt, rank2, e2, nrow, e1, nrow, e1)


def peer_layer(h, g, w_q, keys, u, v):
    q, xt = norm_matmul(h, g, w_q.astype(BF16), F32, emit_t=True)
    rank2, e2, nrow, e1 = peer_select(q, keys)
    return peer_dense(xt, u.astype(BF16), v.T.astype(BF16), rank2, e2, nrow, e1)


def _pad_cols(w, n):
    return jnp.pad(w, ((0, 0), (0, n - w.shape[1])))


def kernel(x, p, mix_norm, ffn_norm, ple_norm, final_norm, ssd_w_in, ssd_conv_w, ssd_conv_b, ssd_dt_bias, ssd_a_log, ssd_d, ssd_norm, ssd_w_out, ret_w_in, ret_norm, ret_w_out, diff_w_in, diff_lambda, diff_norm, diff_w_out, fox_w_in, fox_b_f, fox_w_out, rel_bias, peer_w_q, peer_keys, peer_u, peer_v, ple_proj, ple_gate):
    B, S, D = x.shape
    T = B * S
    depth = mix_norm.shape[0]
    n_mixers = 4
    h = x.reshape(T, D)
    p_all = p.reshape(depth, T, PLE_DIM)
    for i in range(depth):
        m, j = i % n_mixers, i // n_mixers
        g = mix_norm[i]
        if m == 0:
            w = ssd_w_in[j]
            proj = norm_matmul(h, g, w[:, :SSD_MAIN].astype(BF16), BF16)
            dt_raw = norm_matmul(h, g, _pad_cols(w[:, SSD_MAIN:], LANE).astype(BF16), F32)
            y = ssd_core(proj.reshape(B, S, SSD_MAIN), dt_raw.reshape(B, S, LANE), ssd_conv_w[j],
                         ssd_conv_b[j], ssd_dt_bias[j], ssd_a_log[j], ssd_d[j], ssd_norm[j])
            w_out = ssd_w_out[j]
        elif m == 1:
            proj = norm_matmul(h, g, ret_w_in[j].astype(BF16), BF16)
            y = retention_core(proj.reshape(B, S, -1), ret_norm[j])
            w_out = ret_w_out[j]
        elif m == 2:
            lam_init = 0.8 - 0.6 * math.exp(-0.3 * i)
            proj = norm_matmul(h, g, diff_w_in[j].astype(BF16), BF16)
            y = diff_core(proj.reshape(B, S, -1), diff_lambda[j], diff_norm[j], rel_bias, lam_init)
            w_out = diff_w_out[j]
        else:
            w = fox_w_in[j]
            proj = norm_matmul(h, g, w[:, :3 * D].astype(BF16), BF16)
            c_t, c_rows = fox_gate(h.reshape(B, S, D), g, w[:, 3 * D:], fox_b_f[j])
            y = fox_core(proj.reshape(B, S, -1), c_t, c_rows)
            w_out = fox_w_out[j]
        h = matmul_residual(y.reshape(T, -1), w_out.astype(BF16), h)
        y = peer_layer(h, ffn_norm[i], peer_w_q[i], peer_keys[i], peer_u[i], peer_v[i])
        h = ple_layer(h, y, ple_norm[i], ple_gate[i].astype(BF16), p_all, i, ple_proj[i].astype(BF16),
                      final_norm, final=(i == depth - 1))
    return h.reshape(B, S, D)
```

```python
import functools
import math

import jax
import jax.numpy as jnp
import numpy as np
from jax import lax
from jax.experimental import pallas as pl
from jax.experimental.pallas import tpu as pltpu

F32 = jnp.float32
BF16 = jnp.bfloat16

D_MODEL = 1024
CHUNK = 64
NORM_EPS = 1e-6
LOG2E = math.log2(math.e)
ROPE_BASE = 10000.0
PLE_DIM = 256

SSD_D_INNER = 2 * D_MODEL
SSD_HEAD_DIM = 64
SSD_HEADS = SSD_D_INNER // SSD_HEAD_DIM
SSD_GROUPS = 4
SSD_HEADS_PER_GROUP = SSD_HEADS // SSD_GROUPS
SSD_STATE = 128
SSD_CONV = 4
SSD_CONV_DIM = SSD_D_INNER + 2 * SSD_GROUPS * SSD_STATE
SSD_MAIN = SSD_D_INNER + SSD_CONV_DIM
SSD_BLOCK = 128

RET_HEADS = 4
RET_QK_DIM = D_MODEL // RET_HEADS
RET_V_DIM = 2 * RET_QK_DIM
RET_V_WIDTH = RET_HEADS * RET_V_DIM
RET_BLOCK = 256

DIFF_HEADS = 8
DIFF_HEAD_DIM = D_MODEL // DIFF_HEADS // 2
FOX_HEADS = 16
FOX_HEAD_DIM = D_MODEL // FOX_HEADS
FOX_GATE_BLOCK = 512
ATTN_TILE = 512
ATTN_QUERY_TILES = 4
HEAD_LANES = 128

REL_BUCKETS = 32
REL_MAX_DIST = 128

PEER_KEYS = 128
PEER_EXPERTS = PEER_KEYS * PEER_KEYS
PEER_HEADS = 8
PEER_TOPK = 16
PEER_QUERY_DIM = 256
PEER_HALF = PEER_QUERY_DIM // 2
PEER_TOKENS = 1024
PEER_TOKEN_CHUNK = 512
PEER_EXPERT_TILE = 1024
PEER_SELECT_TOKENS = 1024
PEER_DENSE_VMEM = 56 * 1024 * 1024

LANE = 128
SUBLANE = 8
BF16_ROWS = 2 * SUBLANE
VMEM_LIMIT = 48 * 1024 * 1024

_NT = (((1,), (1,)), ((), ()))


def _params(*sem):
    return pltpu.CompilerParams(dimension_semantics=sem, vmem_limit_bytes=VMEM_LIMIT)


def _rms(x, g):
    return x * lax.rsqrt(jnp.mean(x * x, axis=-1, keepdims=True) + NORM_EPS) * g


def _split3(x):
    hi = x.astype(BF16)
    r1 = x - hi.astype(F32)
    mid = r1.astype(BF16)
    lo = (r1 - mid.astype(F32)).astype(BF16)
    return hi, mid, lo


def _dot_exact_lhs01(m01, x):
    hi, mid, lo = _split3(x)
    d = functools.partial(jnp.dot, preferred_element_type=F32)
    return (d(m01, hi) + d(m01, mid)) + d(m01, lo)


def _norm_matmul_kernel(h_ref, g_ref, w_ref, o_ref, *rest, emit_t):
    if emit_t:
        xt_ref, xn_ref = rest
    else:
        (xn_ref,) = rest

    @pl.when(pl.program_id(1) == 0)
    def _():
        y = _rms(h_ref[...], g_ref[...])
        xn_ref[...] = y.astype(BF16)
        if emit_t:
            tc = xt_ref.shape[2]
            for c in range(xt_ref.shape[0]):
                xt_ref[c] = y[c * tc:(c + 1) * tc, :].T.astype(BF16)

    o_ref[...] = jnp.dot(xn_ref[...], w_ref[...], preferred_element_type=F32).astype(o_ref.dtype)


def norm_matmul(h, g, w, out_dtype, emit_t=False, tm=1024, tn=1024):
    T, D = h.shape
    N = w.shape[1]
    tn = min(tn, N)
    assert T % tm == 0 and N % tn == 0
    out_shape = [jax.ShapeDtypeStruct((T, N), out_dtype)]
    out_specs = [pl.BlockSpec((tm, tn), lambda i, j: (i, j))]
    if emit_t:
        tc = PEER_TOKEN_CHUNK
        out_shape.append(jax.ShapeDtypeStruct((T // tc, D, tc), BF16))
        out_specs.append(pl.BlockSpec((tm // tc, D, tc), lambda i, j: (i, 0, 0)))
    res = pl.pallas_call(
        functools.partial(_norm_matmul_kernel, emit_t=emit_t),
        grid=(T // tm, N // tn),
        in_specs=[pl.BlockSpec((tm, D), lambda i, j: (i, 0)),
                  pl.BlockSpec((1, D), lambda i, j: (0, 0)),
                  pl.BlockSpec((D, tn), lambda i, j: (0, j))],
        out_specs=out_specs,
        out_shape=out_shape,
        scratch_shapes=[pltpu.VMEM((tm, D), BF16)],
        compiler_params=_params("parallel", "arbitrary"),
        name="norm_matmul",
    )(h, g.reshape(1, D), w)
    return res if emit_t else res[0]


def _matmul_residual_kernel(a_ref, w_ref, h_ref, o_ref):
    o_ref[...] = h_ref[...] + jnp.dot(a_ref[...], w_ref[...], preferred_element_type=F32)


def matmul_residual(a, w, h, tm=1024):
    T, K = a.shape
    N = w.shape[1]
    return pl.pallas_call(
        _matmul_residual_kernel,
        grid=(T // tm,),
        in_specs=[pl.BlockSpec((tm, K), lambda i: (i, 0)),
                  pl.BlockSpec((K, N), lambda i: (0, 0)),
                  pl.BlockSpec((tm, N), lambda i: (i, 0))],
        out_specs=pl.BlockSpec((tm, N), lambda i: (i, 0)),
        out_shape=jax.ShapeDtypeStruct((T, N), F32),
        input_output_aliases={2: 0},
        compiler_params=_params("parallel"),
        name="matmul_residual",
    )(a, w, h)


def _ple_kernel(h_ref, y_ref, g_ref, wg_ref, p_ref, wp_ref, fg_ref, o_ref, *, final):
    x = h_ref[...] + y_ref[...]
    xn = _rms(x, g_ref[...]).astype(BF16)
    gate = jax.nn.sigmoid(jnp.dot(xn, wg_ref[...], preferred_element_type=F32))
    proj = jnp.dot(p_ref[...].astype(BF16), wp_ref[...], preferred_element_type=F32)
    y = x + gate * proj
    if final:
        y = _rms(y, fg_ref[...])
    o_ref[...] = y


def ple_layer(h, y, g, w_gate, p_all, layer, w_proj, final_g, final, tm=1024):
    T, D = h.shape
    return pl.pallas_call(
        functools.partial(_ple_kernel, final=final),
        grid=(T // tm,),
        in_specs=[pl.BlockSpec((tm, D), lambda i: (i, 0)),
                  pl.BlockSpec((tm, D), lambda i: (i, 0)),
                  pl.BlockSpec((1, D), lambda i: (0, 0)),
                  pl.BlockSpec((D, D), lambda i: (0, 0)),
                  pl.BlockSpec((None, tm, PLE_DIM), lambda i: (layer, i, 0)),
                  pl.BlockSpec((PLE_DIM, D), lambda i: (0, 0)),
                  pl.BlockSpec((1, D), lambda i: (0, 0))],
        out_specs=pl.BlockSpec((tm, D), lambda i: (i, 0)),
        out_shape=jax.ShapeDtypeStruct((T, D), F32),
        input_output_aliases={0: 0},
        compiler_params=_params("parallel"),
        name="ple_layer",
    )(h, y, g.reshape(1, D), w_gate, p_all, w_proj, final_g.reshape(1, D))


def _ssd_kernel(proj_ref, dt_ref, cw_ref, cb_ref, dtb_ref, alog_ref, dsk_ref, ng_ref, ewide_ref, efeat_ref,
                o_ref, xbuf, state, ybuf, ibuf, cwide, cfeat, xw_s, *, L):
    DI, P, N, R = SSD_D_INNER, SSD_HEAD_DIM, SSD_STATE, SSD_HEADS_PER_GROUP
    GW = DI // SSD_GROUPS
    PAIR = 2 * P
    assert PAIR == LANE and L == LANE

    @pl.when(pl.program_id(1) == 0)
    def _():
        xbuf[0:8, :] = jnp.zeros((8, SSD_CONV_DIM), F32)
        state[...] = jnp.zeros(state.shape, F32)

    xbc = proj_ref[0, :, DI:].astype(F32)
    xbuf[8:8 + L, :] = xbc
    conv = cb_ref[...] + cw_ref[3:4, :] * xbc
    for j in range(1, SSD_CONV):
        conv = conv + cw_ref[SSD_CONV - 1 - j:SSD_CONV - j, :] * xbuf[8 - j:8 - j + L, :]
    xbuf[0:8, :] = xbuf[L:L + 8, :]
    act = conv * jax.nn.sigmoid(conv)
    xs = act[:, :DI]
    bm = act[:, DI:DI + SSD_GROUPS * N]
    cm = act[:, DI + SSD_GROUPS * N:]

    dt = jax.nn.softplus(dt_ref[0] + dtb_ref[...])
    a = dt * (-jnp.exp(alog_ref[...]))
    row = lax.broadcasted_iota(jnp.int32, (L, L), 0)
    col = lax.broadcasted_iota(jnp.int32, (L, L), 1)
    tril = row >= col
    cum = _dot_exact_lhs01(tril.astype(BF16), a)
    cum_t = cum.T
    dt_t = dt.T

    split = lambda x: jnp.concatenate(_split3(x), axis=1)
    cum3 = split(cum)
    cwide[...] = jnp.dot(cum3, ewide_ref[...], preferred_element_type=F32)
    cfeat[...] = jnp.dot(cum3, efeat_ref[...], preferred_element_type=F32)
    dfeat = jnp.dot(split(dt), efeat_ref[...], preferred_element_type=F32)
    last = cfeat[L - 1:L, :]
    xw_s[...] = (xs * (jnp.exp(last - cfeat[...]) * dfeat)).astype(BF16)
    elast = jnp.exp(last)
    xs_b = xs.astype(BF16)
    lane = lax.broadcasted_iota(jnp.int32, (1, PAIR), 1)
    lower = lane < P
    zero = jnp.zeros((L, PAIR), BF16)

    for g in range(SSD_GROUPS):
        bg = bm[:, g * N:(g + 1) * N]
        cg = cm[:, g * N:(g + 1) * N].astype(BF16)
        cb = lax.dot_general(cg, bg.astype(BF16), _NT, preferred_element_type=F32)
        bg_t = bg.T.astype(BF16)
        for pr in range(R // 2):
            blk = slice((g * (R // 2) + pr) * PAIR, (g * (R // 2) + pr + 1) * PAIR)
            xp = xs_b[:, blk]
            halves = (jnp.where(lower, xp, zero), jnp.where(lower, zero, xp))
            y = None
            for k in range(2):
                hd = g * R + 2 * pr + k
                seg = cwide[:, hd * LANE:(hd + 1) * LANE] - cum_t[hd:hd + 1, :]
                w = jnp.exp(jnp.where(tril, seg, -jnp.inf)) * cb * dt_t[hd:hd + 1, :]
                yk = jnp.dot(w.astype(BF16), halves[k], preferred_element_type=F32)
                y = yk if y is None else y + yk
            st = state[g, :, pr * PAIR:(pr + 1) * PAIR]
            ybuf[:, blk] = y
            ibuf[:, blk] = jnp.dot(cg, st.astype(BF16), preferred_element_type=F32)
            state[g, :, pr * PAIR:(pr + 1) * PAIR] = (st * elast[:, blk]
                                                      + jnp.dot(bg_t, xw_s[:, blk], preferred_element_type=F32))

    y = ybuf[...] + ibuf[...] * jnp.exp(cfeat[...]) + dsk_ref[...] * xs
    z = proj_ref[0, :, :DI].astype(F32)
    y = y * (z * jax.nn.sigmoid(z))
    for g in range(SSD_GROUPS):
        yg = y[:, g * GW:(g + 1) * GW]
        o_ref[0, :, g * GW:(g + 1) * GW] = _rms(yg, ng_ref[:, g * GW:(g + 1) * GW]).astype(o_ref.dtype)


def ssd_core(proj, dt_raw, conv_w, conv_b, dt_bias, a_log, d_skip, norm_g):
    B, S, _ = proj.shape
    L = SSD_BLOCK
    assert S % L == 0
    pad = LANE - SSD_HEADS
    vec = lambda n: pl.BlockSpec((1, n), lambda b, c: (0, 0))
    whole = lambda a: pl.BlockSpec(a.shape, lambda b, c: (0, 0))

    def spread(width):
        e = np.zeros((LANE, SSD_HEADS * width), np.float32)
        for h in range(SSD_HEADS):
            e[h, h * width:(h + 1) * width] = 1.0
        return jnp.asarray(np.concatenate([e, e, e], axis=0), BF16)

    e_wide, e_feat = spread(LANE), spread(SSD_HEAD_DIM)
    return pl.pallas_call(
        functools.partial(_ssd_kernel, L=L),
        grid=(B, S // L),
        in_specs=[pl.BlockSpec((1, L, SSD_MAIN), lambda b, c: (b, c, 0)),
                  pl.BlockSpec((1, L, LANE), lambda b, c: (b, c, 0)),
                  pl.BlockSpec((SSD_CONV, SSD_CONV_DIM), lambda b, c: (0, 0)),
                  vec(SSD_CONV_DIM), vec(LANE), vec(LANE), vec(SSD_D_INNER), vec(SSD_D_INNER),
                  whole(e_wide), whole(e_feat)],
        out_specs=pl.BlockSpec((1, L, SSD_D_INNER), lambda b, c: (b, c, 0)),
        out_shape=jax.ShapeDtypeStruct((B, S, SSD_D_INNER), BF16),
        scratch_shapes=[pltpu.VMEM((L + 8, SSD_CONV_DIM), F32),
                        pltpu.VMEM((SSD_GROUPS, SSD_STATE, SSD_D_INNER // SSD_GROUPS), F32),
                        pltpu.VMEM((L, SSD_D_INNER), F32), pltpu.VMEM((L, SSD_D_INNER), F32),
                        pltpu.VMEM((L, SSD_HEADS * LANE), F32), pltpu.VMEM((L, SSD_D_INNER), F32),
                        pltpu.VMEM((L, SSD_D_INNER), BF16)],
        compiler_params=_params("parallel", "arbitrary"),
        name="ssd_core",
    )(proj, dt_raw,
      conv_w.reshape(SSD_CONV, SSD_CONV_DIM), conv_b.reshape(1, SSD_CONV_DIM),
      jnp.pad(dt_bias, (0, pad)).reshape(1, LANE), jnp.pad(a_log, (0, pad)).reshape(1, LANE),
      jnp.repeat(d_skip, SSD_HEAD_DIM).reshape(1, SSD_D_INNER), norm_g.reshape(1, SSD_D_INNER),
      e_wide, e_feat)


def _retention_kernel(q_ref, k_ref, v_ref, gate_ref, cos_ref, sin_ref, dec_ref, qd_ref, kd_ref,
                      cd_ref, ng_ref, o_ref, state):
    half = RET_QK_DIM // 2

    @pl.when(pl.program_id(2) == 0)
    def _():
        state[...] = jnp.zeros(state.shape, F32)

    cos = cos_ref[...]
    sin = sin_ref[...]

    def rot(x):
        x1, x2 = x[:, :half], x[:, half:]
        return jnp.concatenate([x1 * cos - x2 * sin, x1 * sin + x2 * cos], axis=-1)

    q = rot(q_ref[0].astype(F32))
    k = rot(k_ref[0].astype(F32)) * (RET_QK_DIM ** -0.5)
    v = v_ref[0]
    qb = q.astype(BF16)
    s = lax.dot_general(qb, k.astype(BF16), _NT, preferred_element_type=F32) * dec_ref[0]
    o = jnp.dot(s.astype(BF16), v, preferred_element_type=F32)
    st = state[...]
    o = o + jnp.dot(qb, st.astype(BF16), preferred_element_type=F32) * qd_ref[0]
    kt = (k * kd_ref[0]).T.astype(BF16)
    state[...] = st * cd_ref[0] + jnp.dot(kt, v, preferred_element_type=F32)
    gate = gate_ref[0].astype(F32)
    o_ref[0] = (_rms(o, ng_ref[0]) * (gate * jax.nn.sigmoid(gate))).astype(o_ref.dtype)


def retention_core(proj, norm_g):
    B, S, _ = proj.shape
    H, dk, dv, L = RET_HEADS, RET_QK_DIM, RET_V_DIM, RET_BLOCK
    assert S % L == 0 and L % CHUNK == 0
    inv = 1.0 / (ROPE_BASE ** (jnp.arange(0, dk, 2, dtype=F32) / dk))
    ang = jnp.arange(S, dtype=F32)[:, None] * inv[None, :]
    log_gamma = jnp.log1p(-jnp.exp2(-5.0 - jnp.arange(H, dtype=F32)))
    idx = jnp.arange(L, dtype=F32)
    visible = (jnp.arange(L)[None, :] // CHUNK) <= (jnp.arange(L)[:, None] // CHUNK)
    decay = jnp.where(visible[None],
                      jnp.exp(log_gamma[:, None, None] * jnp.abs(idx[:, None] - idx[None, :])), 0.0)
    q_decay = jnp.exp(log_gamma[:, None] * (idx[None, :] + 1.0))[..., None]
    k_decay = jnp.exp(log_gamma[:, None] * (L - 1.0 - idx[None, :]))[..., None]
    block_decay = jnp.exp(log_gamma * L).reshape(H, 1, 1)
    return pl.pallas_call(
        _retention_kernel,
        grid=(B, H, S // L),
        in_specs=[pl.BlockSpec((1, L, dk), lambda b, h, c: (b, c, h)),
                  pl.BlockSpec((1, L, dk), lambda b, h, c: (b, c, H + h)),
                  pl.BlockSpec((1, L, dv), lambda b, h, c: (b, c, H + h)),
                  pl.BlockSpec((1, L, dv), lambda b, h, c: (b, c, 2 * H + h)),
                  pl.BlockSpec((L, dk // 2), lambda b, h, c: (c, 0)),
                  pl.BlockSpec((L, dk // 2), lambda b, h, c: (c, 0)),
                  pl.BlockSpec((1, L, L), lambda b, h, c: (h, 0, 0)),
                  pl.BlockSpec((1, L, 1), lambda b, h, c: (h, 0, 0)),
                  pl.BlockSpec((1, L, 1), lambda b, h, c: (h, 0, 0)),
                  pl.BlockSpec((1, 1, 1), lambda b, h, c: (h, 0, 0)),
                  pl.BlockSpec((1, 1, dv), lambda b, h, c: (h, 0, 0))],
        out_specs=pl.BlockSpec((1, L, dv), lambda b, h, c: (b, c, h)),
        out_shape=jax.ShapeDtypeStruct((B, S, RET_V_WIDTH), BF16),
        scratch_shapes=[pltpu.VMEM((dk, dv), F32)],
        compiler_params=_params("parallel", "parallel", "arbitrary"),
        name="retention_core",
    )(proj, proj, proj, proj, jnp.cos(ang), jnp.sin(ang), decay, q_decay, k_decay, block_decay,
      norm_g.reshape(H, 1, dv))


def _attn_kernel(*refs, mode, T):
    if mode == "diff":
        q_ref, k_ref, v_ref, bias_ref, lam_ref, ng_ref, o_ref, m_s, acc_s, l_s = refs
    else:
        q_ref, k_ref, v_ref, ck_ref, cq_ref, o_ref, m_s, acc_s, cq_s = refs
    i = pl.program_id(2)
    NQ = ATTN_QUERY_TILES
    half = HEAD_LANES // 2
    reps = T // HEAD_LANES
    lane = lax.broadcasted_iota(jnp.int32, (1, HEAD_LANES), 1)
    first = lane < half
    second = jnp.logical_not(first)
    qs = []
    for u in range(NQ):
        q = (q_ref[0, u * T:(u + 1) * T, :].astype(F32) * (half ** -0.5 * LOG2E)).astype(BF16)
        zero = jnp.zeros_like(q)
        qs.append((jnp.where(first, q, zero), jnp.where(first, zero, q)))

    m_s[...] = jnp.full(m_s.shape, -jnp.inf, F32)
    acc_s[...] = jnp.zeros(acc_s.shape, F32)
    if mode == "diff":
        l_s[...] = jnp.zeros(l_s.shape, F32)
    else:
        for u in range(NQ):
            for a in range(2):
                cq_s[u, a] = jnp.broadcast_to(cq_ref[0, 0, u * T:(u + 1) * T, a:a + 1], (T, T))

    def step(j, subtiles, diagonal):
        start = pl.multiple_of(j * T, T)
        k = k_ref[0, pl.ds(start, T), :]
        v = v_ref[0, pl.ds(start, T), :]
        if mode == "diff":
            vs = (v, v)
        else:
            ones = jnp.ones_like(v)
            vs = (jnp.where(first, v, ones), jnp.where(second, v, ones))
        if diagonal is not None:
            row = lax.broadcasted_iota(jnp.int32, (T, T), 0)
            col = lax.broadcasted_iota(jnp.int32, (T, T), 1)
            if mode == "diff":
                visible = (col // CHUNK) <= (row // CHUNK)
            else:
                visible = col <= row
        for u in subtiles:
            for a in range(2):
                s = lax.dot_general(qs[u][a], k, _NT, preferred_element_type=F32)
                if mode == "diff":
                    s = s + bias_ref[0, jnp.minimum(NQ * i + u - j, 2)]
                else:
                    s = (s - ck_ref[0, 0, j, a:a + 1, :]) + cq_s[u, a]
                if u == diagonal:
                    s = jnp.where(visible, s, -jnp.inf)
                m_prev = m_s[u, a]
                m_new = jnp.maximum(m_prev, jnp.max(s, axis=-1, keepdims=True))
                alpha = jnp.exp2(m_prev - m_new)
                p = jnp.exp2(s - jnp.concatenate([m_new] * reps, axis=1))
                if mode == "diff":
                    l_s[u, a] = alpha * l_s[u, a] + jnp.sum(p, axis=-1, keepdims=True)
                acc_s[u, a] = alpha * acc_s[u, a] + jnp.dot(p.astype(BF16), vs[a], preferred_element_type=F32)
                m_s[u, a] = m_new

    everyone = tuple(range(NQ))
    lax.fori_loop(0, NQ * i, lambda j, c: (step(j, everyone, None), c)[1], 0)
    for d in range(NQ):
        step(NQ * i + d, everyone[d:], d)

    for u in range(NQ):
        if mode == "diff":
            o = acc_s[u, 0] / l_s[u, 0] - lam_ref[0] * (acc_s[u, 1] / l_s[u, 1])
            o = _rms(o, ng_ref[...])
        else:
            o0, o1 = acc_s[u, 0], acc_s[u, 1]
            o = jnp.where(first, o0 / pltpu.roll(o0, half, 1), o1 / pltpu.roll(o1, half, 1))
        o_ref[0, u * T:(u + 1) * T, :] = o.astype(o_ref.dtype)


def _attn_call(mode, proj, n_blocks, extra_inputs, extra_specs):
    B, S, _ = proj.shape
    T, NQ = ATTN_TILE, ATTN_QUERY_TILES
    assert S % (NQ * T) == 0
    stats = pltpu.VMEM((NQ, 2, T, HEAD_LANES), F32)
    return pl.pallas_call(
        functools.partial(_attn_kernel, mode=mode, T=T),
        grid=(B, n_blocks, S // (NQ * T)),
        in_specs=[pl.BlockSpec((1, NQ * T, HEAD_LANES), lambda b, h, i: (b, i, h)),
                  pl.BlockSpec((1, S, HEAD_LANES), lambda b, h, i: (b, 0, n_blocks + h)),
                  pl.BlockSpec((1, S, HEAD_LANES), lambda b, h, i: (b, 0, 2 * n_blocks + h))] + extra_specs,
        out_specs=pl.BlockSpec((1, NQ * T, HEAD_LANES), lambda b, h, i: (b, i, h)),
        out_shape=jax.ShapeDtypeStruct((B, S, D_MODEL), BF16),
        scratch_shapes=[stats, stats, stats if mode == "diff" else pltpu.VMEM((NQ, 2, T, T), F32)],
        compiler_params=_params("parallel", "parallel", "arbitrary"),
        name=mode + "_attention",
    )(proj, proj, proj, *extra_inputs)


def _t5_bucket(rel):
    nb = REL_BUCKETS // 2
    max_exact = nb // 2
    ret = (rel > 0).astype(jnp.int32) * nb
    n = jnp.abs(rel)
    nf = jnp.maximum(n, 1).astype(F32)
    large = max_exact + (jnp.log(nf / max_exact) / math.log(REL_MAX_DIST / max_exact)
                         * (nb - max_exact)).astype(jnp.int32)
    large = jnp.minimum(large, nb - 1)
    return ret + jnp.where(n < max_exact, n, large)


def _bias_tiles_kernel(bucket_ref, table_ref, o_ref):
    h = pl.program_id(0)
    for d in range(bucket_ref.shape[0]):
        bucket = bucket_ref[d]
        tile = jnp.zeros(bucket.shape, F32)
        for b in range(REL_BUCKETS):
            tile = jnp.where(bucket == b, table_ref[b * DIFF_HEADS + h] * LOG2E, tile)
        o_ref[0, d] = tile


def diff_bias_tiles(rel_bias):
    T = ATTN_TILE
    assert T >= REL_MAX_DIST
    off = jnp.arange(T)
    rel = (off[None, None, :] - off[None, :, None]) - (jnp.arange(3) * T)[:, None, None]
    return pl.pallas_call(
        _bias_tiles_kernel,
        grid=(DIFF_HEADS,),
        in_specs=[pl.BlockSpec((3, T, T), lambda h: (0, 0, 0)),
                  pl.BlockSpec(memory_space=pltpu.SMEM)],
        out_specs=pl.BlockSpec((1, 3, T, T), lambda h: (h, 0, 0, 0)),
        out_shape=jax.ShapeDtypeStruct((DIFF_HEADS, 3, T, T), F32),
        compiler_params=_params("parallel"),
        name="diff_bias_tiles",
    )(_t5_bucket(rel), rel_bias.astype(F32).reshape(REL_BUCKETS * DIFF_HEADS))


def diff_core(proj, lam_vecs, norm_g, rel_bias, lam_init):
    T = ATTN_TILE
    lv = lam_vecs.astype(F32)
    lam = jnp.exp(jnp.sum(lv[0] * lv[1])) - jnp.exp(jnp.sum(lv[2] * lv[3])) + lam_init
    bias = diff_bias_tiles(rel_bias)
    g = (norm_g * (1.0 - lam_init)).reshape(1, HEAD_LANES)
    extra_specs = [pl.BlockSpec((1, 3, T, T), lambda b, h, i: (h, 0, 0, 0)),
                   pl.BlockSpec(memory_space=pltpu.SMEM),
                   pl.BlockSpec((1, HEAD_LANES), lambda b, h, i: (0, 0))]
    return _attn_call("diff", proj, DIFF_HEADS, [bias, lam.reshape(1), g], extra_specs)


def _fox_gate_kernel(h_ref, g_ref, w_ref, b_ref, ct_ref, c_ref, carry, *, L):
    @pl.when(pl.program_id(1) == 0)
    def _():
        carry[...] = jnp.zeros(carry.shape, F32)

    xn = _rms(h_ref[0], g_ref[...])
    logits = lax.dot_general(w_ref[...], xn.astype(BF16), _NT, preferred_element_type=F32)
    log_f = jax.nn.log_sigmoid(logits + b_ref[...])
    row = lax.broadcasted_iota(jnp.int32, (L, L), 0)
    col = lax.broadcasted_iota(jnp.int32, (L, L), 1)
    triu = (row <= col).astype(BF16)
    hi, mid, lo = _split3(log_f)
    d = functools.partial(jnp.dot, preferred_element_type=F32)
    c = carry[...] + ((d(hi, triu) + d(mid, triu)) + d(lo, triu))
    carry[...] = c[:, L - 1:L]
    c2 = c * LOG2E
    ct_ref[0] = c2
    c_ref[0] = c2.T


def fox_gate(h3, g, w_f, b_f):
    B, S, D = h3.shape
    L = FOX_GATE_BLOCK
    assert S % L == 0
    pad = LANE - FOX_HEADS
    return pl.pallas_call(
        functools.partial(_fox_gate_kernel, L=L),
        grid=(B, S // L),
        in_specs=[pl.BlockSpec((1, L, D), lambda b, c: (b, c, 0)),
                  pl.BlockSpec((1, D), lambda b, c: (0, 0)),
                  pl.BlockSpec((LANE, D), lambda b, c: (0, 0)),
                  pl.BlockSpec((LANE, 1), lambda b, c: (0, 0))],
        out_specs=[pl.BlockSpec((1, LANE, L), lambda b, c: (b, 0, c)),
                   pl.BlockSpec((1, L, LANE), lambda b, c: (b, c, 0))],
        out_shape=[jax.ShapeDtypeStruct((B, LANE, S), F32), jax.ShapeDtypeStruct((B, S, LANE), F32)],
        scratch_shapes=[pltpu.VMEM((LANE, 1), F32)],
        compiler_params=_params("parallel", "arbitrary"),
        name="fox_gate",
    )(h3, g.reshape(1, D), jnp.pad(w_f.T, ((0, pad), (0, 0))).astype(BF16),
      jnp.pad(b_f, (0, pad)).reshape(LANE, 1))


def fox_core(proj, c_t, c_rows):
    B, S, _ = proj.shape
    T = ATTN_TILE
    nb = FOX_HEADS // 2
    ck = jnp.transpose(c_t[:, :FOX_HEADS].reshape(B, nb, 2, S // T, T), (0, 1, 3, 2, 4))
    cq = jnp.transpose(c_rows[:, :, :FOX_HEADS].reshape(B, S, nb, 2), (0, 2, 1, 3))
    extra_specs = [pl.BlockSpec((1, 1, S // T, 2, T), lambda b, h, i: (b, h, 0, 0, 0)),
                   pl.BlockSpec((1, 1, ATTN_QUERY_TILES * T, 2), lambda b, h, i: (b, h, i, 0))]
    return _attn_call("fox", proj, nb, [ck, cq], extra_specs)


def _candidate_tables():
    pairs = [(a, b) for a in range(PEER_TOPK) for b in range(PEER_TOPK) if (a + 1) * (b + 1) <= PEER_TOPK]
    rows = -(-len(pairs) // BF16_ROWS) * BF16_ROWS
    p1 = np.zeros((rows, LANE), np.float32)
    p2 = np.zeros((rows, LANE), np.float32)
    for r, (a, b) in enumerate(pairs):
        p1[r, a] = 1.0
        p2[r, b] = 1.0
    return len(pairs), p1, p2


N_CAND, _P1, _P2 = _candidate_tables()


_INT_MIN = -2 ** 31
_FLIP = 0x7FFFFFFF


def _ordered_int(x):
    b = lax.bitcast_convert_type(x + 0.0, jnp.int32)
    return jnp.where(b < 0, b ^ _FLIP, b)


def _ordered_float(k):
    return lax.bitcast_convert_type(jnp.where(k < 0, k ^ _FLIP, k), F32)


_FLOOR = -3.0e38
_MARK, _MARK_STEP = -3.2e38, 1.0e36


def _top16_ranks(x, break_ties):
    n, t = x.shape
    idx = lax.broadcasted_iota(jnp.int32, (n, t), 0)
    ridx = lax.broadcasted_iota(jnp.int32, (PEER_TOPK, t), 0)
    work = _ordered_int(x) if break_ties else jnp.maximum(x, _FLOOR)
    vals = jnp.zeros((PEER_TOPK, t), work.dtype)
    for r in range(PEER_TOPK):
        mx = jnp.max(work, axis=0, keepdims=True)
        hit = work == mx
        if break_ties:
            pos = jnp.min(jnp.where(hit, idx, n), axis=0, keepdims=True)
            hit = idx == pos
        work = jnp.where(hit, _INT_MIN + r if break_ties else _MARK - r * _MARK_STEP, work)
        vals = jnp.where(ridx == r, mx, vals)
    if break_ties:
        rank = jnp.where(work < _INT_MIN + PEER_TOPK, work & (2 * PEER_TOPK - 1), PEER_TOPK)
        return rank, _ordered_float(vals)
    rank = jnp.where(work < 0.5 * (_FLOOR + _MARK),
                     jnp.round((_MARK - work) * (1.0 / _MARK_STEP)).astype(jnp.int32), PEER_TOPK)
    return rank, vals


def _peer_select_kernel(q_ref, keys_ref, p1_ref, p2_ref, p1t_ref, rank2_ref, e2_ref, nrow_ref, e1_ref):
    def scores(c):
        kk = keys_ref[0, c]
        qq = q_ref[:, c * PEER_HALF:(c + 1) * PEER_HALF]
        kh, km, _ = _split3(kk)
        qh, qm, _ = _split3(qq)
        d = lambda a, b: lax.dot_general(a, b, _NT, preferred_element_type=F32)
        return d(kh, qh) + (d(kh, qm) + d(km, qh))

    s1 = scores(0)
    s2 = scores(1)
    out_refs = (rank2_ref, e2_ref, nrow_ref, e1_ref)
    clean = _peer_select_pass(s1, s2, p1_ref, p2_ref, p1t_ref, out_refs, break_ties=False)

    @pl.when(jnp.logical_not(clean))
    def _():
        _peer_select_pass(s1, s2, p1_ref, p2_ref, p1t_ref, out_refs, break_ties=True)


def _peer_select_pass(s1, s2, p1_ref, p2_ref, p1t_ref, out_refs, break_ties):
    rank2_ref, e2_ref, nrow_ref, e1_ref = out_refs
    rank1, v1 = _top16_ranks(s1, break_ties)
    rank2, v2 = _top16_ranks(s2, break_ties)

    tokens = s1.shape[1]
    pad = jnp.zeros((LANE - PEER_TOPK, tokens), F32)
    cand = (_dot_exact_lhs01(p1_ref[...], jnp.concatenate([v1, pad], axis=0))
            + _dot_exact_lhs01(p2_ref[...], jnp.concatenate([v2, pad], axis=0)))
    cidx = lax.broadcasted_iota(jnp.int32, cand.shape, 0)
    cand = jnp.where(cidx < N_CAND, cand, -jnp.inf)
    cand_rank, _ = _top16_ranks(cand, break_ties)
    sel = jnp.where(cand_rank < PEER_TOPK, 1.0, 0.0)
    top = v1[0:1, :] + v2[0:1, :]
    z = jnp.sum(sel * jnp.exp(jnp.where(sel > 0.0, cand - top, 0.0)), axis=0, keepdims=True)
    sel_pad = jnp.concatenate([sel, jnp.zeros((LANE - sel.shape[0], tokens), F32)], axis=0).astype(BF16)
    n_by_rank = jnp.dot(p1t_ref[...], sel_pad, preferred_element_type=F32)
    nrow = jnp.zeros(s1.shape, F32)
    for a in range(PEER_TOPK):
        nrow = jnp.where(rank1 == a, n_by_rank[a:a + 1, :], nrow)

    rank2_ref[0] = rank2.astype(F32).astype(BF16)
    e2_ref[0] = jnp.exp(s2 - v2[0:1, :]).astype(BF16)
    e1 = jnp.exp(s1 - v1[0:1, :]) * (0.5 / z)
    for g in range(PEER_KEYS // SUBLANE):
        for st in range(tokens // LANE):
            tile = (slice(g * SUBLANE, (g + 1) * SUBLANE), slice(st * LANE, (st + 1) * LANE))
            nrow_ref[0, g, st] = nrow[tile]
            e1_ref[0, g, st] = e1[tile]

    if break_ties:
        return None
    ranked = (jnp.sum(jnp.where(rank1 < PEER_TOPK, 1.0, 0.0), axis=0, keepdims=True)
              + jnp.sum(jnp.where(rank2 < PEER_TOPK, 1.0, 0.0), axis=0, keepdims=True)
              + jnp.sum(sel, axis=0, keepdims=True))
    return jnp.max(ranked) == 3.0 * PEER_TOPK


def peer_select(q, keys):
    T = q.shape[0]
    tm = PEER_SELECT_TOKENS
    assert T % tm == 0
    hk = pl.BlockSpec((1, PEER_KEYS, tm), lambda i, h: (h, 0, i))
    groups = PEER_KEYS // SUBLANE
    hk1 = pl.BlockSpec((1, groups, tm // LANE, SUBLANE, LANE), lambda i, h: (h, 0, i, 0, 0))
    shp = lambda dt: jax.ShapeDtypeStruct((PEER_HEADS, PEER_KEYS, T), dt)
    shp1 = jax.ShapeDtypeStruct((PEER_HEADS, groups, T // LANE, SUBLANE, LANE), F32)
    cst = lambda a: pl.BlockSpec(a.shape, lambda i, h: (0, 0))
    p1, p2 = jnp.asarray(_P1, BF16), jnp.asarray(_P2, BF16)
    p1t = jnp.asarray(np.pad(_P1.T, ((0, 0), (0, LANE - _P1.shape[0]))), BF16)
    return pl.pallas_call(
        _peer_select_kernel,
        grid=(T // tm, PEER_HEADS),
        in_specs=[pl.BlockSpec((tm, PEER_QUERY_DIM), lambda i, h: (i, h)),
                  pl.BlockSpec((1, 2, PEER_KEYS, PEER_HALF), lambda i, h: (h, 0, 0, 0)),
                  cst(p1), cst(p2), cst(p1t)],
        out_specs=[hk, hk, hk1, hk1],
        out_shape=[shp(BF16), shp(BF16), shp1, shp1],
        compiler_params=_params("parallel", "parallel"),
        name="peer_select",
    )(q, keys, p1, p2, p1t)


def _peer_dense_kernel(xt_ref, u_ref, u_next_ref, vt_ref, rank2_ref, e2_ref, nrow_ref, e1_ref,
                       nrow_next_ref, e1_next_ref, o_ref, *scratch):
    n = xt_ref.shape[0]
    acc, gbuf, hbuf, wbuf = scratch[:n], scratch[n:2 * n], scratch[2 * n:2 * n + 2], scratch[2 * n + 2:]
    j = pl.program_id(1)
    rows = PEER_EXPERT_TILE // PEER_KEYS
    tc = PEER_TOKEN_CHUNK
    n_chunks = xt_ref.shape[0]
    pack = BF16_ROWS

    def first_matmul(c, u):
        hbuf[c % 2][...] = jnp.dot(u[...], xt_ref[c], preferred_element_type=F32).astype(BF16)

    def gate_weights(c, nrow, e1):
        cols = slice(c * tc, (c + 1) * tc)
        for ii in range(rows):
            w = jnp.zeros((PEER_KEYS, tc), BF16)
            for h in range(PEER_HEADS):
                def spread(ref):
                    tiles = [jnp.broadcast_to(ref[h, 0, c * (tc // LANE) + lt, ii:ii + 1, :],
                                              (pack, LANE)).astype(BF16) for lt in range(tc // LANE)]
                    return jnp.concatenate([jnp.concatenate(tiles, axis=1)] * (PEER_KEYS // pack), axis=0)
                w = w + jnp.where(rank2_ref[h, :, cols] < spread(nrow), e2_ref[h, :, cols] * spread(e1),
                                  jnp.zeros_like(w))
            wbuf[c % 2][ii * PEER_KEYS:(ii + 1) * PEER_KEYS, :] = w

    def finish(c):
        for ii in range(rows):
            r = slice(ii * PEER_KEYS, (ii + 1) * PEER_KEYS)
            hr = hbuf[c % 2][r, :]
            act = hr * (1.0 + lax.erf(hr * jnp.asarray(2.0 ** -0.5, BF16)))
            gbuf[c][r, :] = wbuf[c % 2][r, :] * act
        acc[c][...] += jnp.dot(vt_ref[...], gbuf[c][...], preferred_element_type=F32)

    @pl.when(j == 0)
    def _():
        for a in acc:
            a[...] = jnp.zeros(a.shape, F32)
        first_matmul(0, u_ref)
        gate_weights(0, nrow_ref, e1_ref)

    for c in range(n_chunks):
        if c + 1 < n_chunks:
            first_matmul(c + 1, u_ref)
            gate_weights(c + 1, nrow_ref, e1_ref)
        else:
            first_matmul(0, u_next_ref)
            gate_weights(0, nrow_next_ref, e1_next_ref)
        finish(c)

    @pl.when(j == pl.num_programs(1) - 1)
    def _():
        for c in range(n_chunks):
            o_ref[c * tc:(c + 1) * tc, :] = acc[c][...].T


def peer_dense(xt, u, vt, rank2, e2, nrow, e1):
    n_slabs, D, tc = xt.shape
    T = n_slabs * tc
    tm, te = PEER_TOKENS, PEER_EXPERT_TILE
    rows = te // PEER_KEYS
    n_tiles = PEER_EXPERTS // te
    assert rows == SUBLANE and tc == PEER_TOKEN_CHUNK and (tm // tc) % 2 == 0
    assert T % tm == 0
    nxt = lambda j: jnp.minimum(j + 1, n_tiles - 1)
    by_key2 = pl.BlockSpec((PEER_HEADS, PEER_KEYS, tm), lambda i, j: (0, 0, i))
    by_key1 = pl.BlockSpec((PEER_HEADS, 1, tm // LANE, rows, LANE), lambda i, j: (0, j, i, 0, 0))
    by_key1_next = pl.BlockSpec((PEER_HEADS, 1, tm // LANE, rows, LANE), lambda i, j: (0, nxt(j), i, 0, 0))
    return pl.pallas_call(
        _peer_dense_kernel,
        grid=(T // tm, n_tiles),
        in_specs=[pl.BlockSpec((tm // tc, D, tc), lambda i, j: (i, 0, 0)),
                  pl.BlockSpec((te, D), lambda i, j: (j, 0)),
                  pl.BlockSpec((te, D), lambda i, j: (nxt(j), 0)),
                  pl.BlockSpec((D, te), lambda i, j: (0, j)),
                  by_key2, by_key2, by_key1, by_key1, by_key1_next, by_key1_next],
        out_specs=pl.BlockSpec((tm, D), lambda i, j: (i, 0)),
        out_shape=jax.ShapeDtypeStruct((T, D), F32),
        scratch_shapes=([pltpu.VMEM((D, tc), F32)] * (tm // tc) + [pltpu.VMEM((te, tc), BF16)] * (tm // tc)
                        + [pltpu.VMEM((te, tc), BF16)] * 4),
        compiler_params=pltpu.CompilerParams(dimension_semantics=("parallel", "arbitrary"),
                                             vmem_limit_bytes=PEER_DENSE_VMEM),
        name="peer_dense",
    )(xt, u, u, vt, rank2, e2, nrow, e1, nrow, e1)


def peer_layer(h, g, w_q, keys, u, v):
    q, xt = norm_matmul(h, g, w_q.astype(BF16), F32, emit_t=True)
    rank2, e2, nrow, e1 = peer_select(q, keys)
    return peer_dense(xt, u.astype(BF16), v.T.astype(BF16), rank2, e2, nrow, e1)


def _pad_cols(w, n):
    return jnp.pad(w, ((0, 0), (0, n - w.shape[1])))


def kernel(x, p, mix_norm, ffn_norm, ple_norm, final_norm, ssd_w_in, ssd_conv_w, ssd_conv_b, ssd_dt_bias, ssd_a_log, ssd_d, ssd_norm, ssd_w_out, ret_w_in, ret_norm, ret_w_out, diff_w_in, diff_lambda, diff_norm, diff_w_out, fox_w_in, fox_b_f, fox_w_out, rel_bias, peer_w_q, peer_keys, peer_u, peer_v, ple_proj, ple_gate):
    B, S, D = x.shape
    T = B * S
    depth = mix_norm.shape[0]
    n_mixers = 4
    h = x.reshape(T, D)
    p_all = p.reshape(depth, T, PLE_DIM)
    for i in range(depth):
        m, j = i % n_mixers, i // n_mixers
        g = mix_norm[i]
        if m == 0:
            w = ssd_w_in[j]
            proj = norm_matmul(h, g, w[:, :SSD_MAIN].astype(BF16), BF16)
            dt_raw = norm_matmul(h, g, _pad_cols(w[:, SSD_MAIN:], LANE).astype(BF16), F32)
            y = ssd_core(proj.reshape(B, S, SSD_MAIN), dt_raw.reshape(B, S, LANE), ssd_conv_w[j],
                         ssd_conv_b[j], ssd_dt_bias[j], ssd_a_log[j], ssd_d[j], ssd_norm[j])
            w_out = ssd_w_out[j]
        elif m == 1:
            proj = norm_matmul(h, g, ret_w_in[j].astype(BF16), BF16)
            y = retention_core(proj.reshape(B, S, -1), ret_norm[j])
            w_out = ret_w_out[j]
        elif m == 2:
            lam_init = 0.8 - 0.6 * math.exp(-0.3 * i)
            proj = norm_matmul(h, g, diff_w_in[j].astype(BF16), BF16)
            y = diff_core(proj.reshape(B, S, -1), diff_lambda[j], diff_norm[j], rel_bias, lam_init)
            w_out = diff_w_out[j]
        else:
            w = fox_w_in[j]
            proj = norm_matmul(h, g, w[:, :3 * D].astype(BF16), BF16)
            c_t, c_rows = fox_gate(h.reshape(B, S, D), g, w[:, 3 * D:], fox_b_f[j])
            y = fox_core(proj.reshape(B, S, -1), c_t, c_rows)
            w_out = fox_w_out[j]
        h = matmul_residual(y.reshape(T, -1), w_out.astype(BF16), h)
        y = peer_layer(h, ffn_norm[i], peer_w_q[i], peer_keys[i], peer_u[i], peer_v[i])
        h = ple_layer(h, y, ple_norm[i], ple_gate[i].astype(BF16), p_all, i, ple_proj[i].astype(BF16),
                      final_norm, final=(i == depth - 1))
    return h.reshape(B, S, D)
```

```python
import functools
import math

import jax
import jax.numpy as jnp
import numpy as np
from jax import lax
from jax.experimental import pallas as pl
from jax.experimental.pallas import tpu as pltpu

F32 = jnp.float32
BF16 = jnp.bfloat16

D_MODEL = 1024
CHUNK = 64
NORM_EPS = 1e-6
LOG2E = math.log2(math.e)
ROPE_BASE = 10000.0
PLE_DIM = 256

SSD_D_INNER = 2 * D_MODEL
SSD_HEAD_DIM = 64
SSD_HEADS = SSD_D_INNER // SSD_HEAD_DIM
SSD_GROUPS = 4
SSD_HEADS_PER_GROUP = SSD_HEADS // SSD_GROUPS
SSD_STATE = 128
SSD_CONV = 4
SSD_CONV_DIM = SSD_D_INNER + 2 * SSD_GROUPS * SSD_STATE
SSD_MAIN = SSD_D_INNER + SSD_CONV_DIM
SSD_BLOCK = 128

RET_HEADS = 4
RET_QK_DIM = D_MODEL // RET_HEADS
RET_V_DIM = 2 * RET_QK_DIM
RET_V_WIDTH = RET_HEADS * RET_V_DIM
RET_BLOCK = 256

DIFF_HEADS = 8
DIFF_HEAD_DIM = D_MODEL // DIFF_HEADS // 2
FOX_HEADS = 16
FOX_HEAD_DIM = D_MODEL // FOX_HEADS
FOX_GATE_BLOCK = 512
ATTN_TILE = 512
ATTN_QUERY_TILES = 4
HEAD_LANES = 128

REL_BUCKETS = 32
REL_MAX_DIST = 128

PEER_KEYS = 128
PEER_EXPERTS = PEER_KEYS * PEER_KEYS
PEER_HEADS = 8
PEER_TOPK = 16
PEER_QUERY_DIM = 256
PEER_HALF = PEER_QUERY_DIM // 2
PEER_TOKENS = 1024
PEER_TOKEN_CHUNK = 512
PEER_EXPERT_TILE = 1024
PEER_SELECT_TOKENS = 2048
PEER_DENSE_VMEM = 56 * 1024 * 1024

LANE = 128
SUBLANE = 8
BF16_ROWS = 2 * SUBLANE
VMEM_LIMIT = 48 * 1024 * 1024

_NT = (((1,), (1,)), ((), ()))


def _params(*sem):
    return pltpu.CompilerParams(dimension_semantics=sem, vmem_limit_bytes=VMEM_LIMIT)


def _rms(x, g):
    return x * lax.rsqrt(jnp.mean(x * x, axis=-1, keepdims=True) + NORM_EPS) * g


def _split3(x):
    hi = x.astype(BF16)
    r1 = x - hi.astype(F32)
    mid = r1.astype(BF16)
    lo = (r1 - mid.astype(F32)).astype(BF16)
    return hi, mid, lo


def _dot_exact_lhs01(m01, x):
    hi, mid, lo = _split3(x)
    d = functools.partial(jnp.dot, preferred_element_type=F32)
    return (d(m01, hi) + d(m01, mid)) + d(m01, lo)


def _norm_matmul_kernel(h_ref, g_ref, w_ref, o_ref, *rest, emit_t):
    if emit_t:
        xt_ref, xn_ref = rest
    else:
        (xn_ref,) = rest

    @pl.when(pl.program_id(1) == 0)
    def _():
        y = _rms(h_ref[...], g_ref[...])
        xn_ref[...] = y.astype(BF16)
        if emit_t:
            tc = xt_ref.shape[2]
            for c in range(xt_ref.shape[0]):
                xt_ref[c] = y[c * tc:(c + 1) * tc, :].T.astype(BF16)

    o_ref[...] = jnp.dot(xn_ref[...], w_ref[...], preferred_element_type=F32).astype(o_ref.dtype)


def norm_matmul(h, g, w, out_dtype, emit_t=False, tm=1024, tn=1024):
    T, D = h.shape
    N = w.shape[1]
    tn = min(tn, N)
    assert T % tm == 0 and N % tn == 0
    out_shape = [jax.ShapeDtypeStruct((T, N), out_dtype)]
    out_specs = [pl.BlockSpec((tm, tn), lambda i, j: (i, j))]
    if emit_t:
        tc = PEER_TOKEN_CHUNK
        out_shape.append(jax.ShapeDtypeStruct((T // tc, D, tc), BF16))
        out_specs.append(pl.BlockSpec((tm // tc, D, tc), lambda i, j: (i, 0, 0)))
    res = pl.pallas_call(
        functools.partial(_norm_matmul_kernel, emit_t=emit_t),
        grid=(T // tm, N // tn),
        in_specs=[pl.BlockSpec((tm, D), lambda i, j: (i, 0)),
                  pl.BlockSpec((1, D), lambda i, j: (0, 0)),
                  pl.BlockSpec((D, tn), lambda i, j: (0, j))],
        out_specs=out_specs,
        out_shape=out_shape,
        scratch_shapes=[pltpu.VMEM((tm, D), BF16)],
        compiler_params=_params("parallel", "arbitrary"),
        name="norm_matmul",
    )(h, g.reshape(1, D), w)
    return res if emit_t else res[0]


def _matmul_residual_kernel(a_ref, w_ref, h_ref, o_ref):
    o_ref[...] = h_ref[...] + jnp.dot(a_ref[...], w_ref[...], preferred_element_type=F32)


def matmul_residual(a, w, h, tm=1024):
    T, K = a.shape
    N = w.shape[1]
    return pl.pallas_call(
        _matmul_residual_kernel,
        grid=(T // tm,),
        in_specs=[pl.BlockSpec((tm, K), lambda i: (i, 0)),
                  pl.BlockSpec((K, N), lambda i: (0, 0)),
                  pl.BlockSpec((tm, N), lambda i: (i, 0))],
        out_specs=pl.BlockSpec((tm, N), lambda i: (i, 0)),
        out_shape=jax.ShapeDtypeStruct((T, N), F32),
        input_output_aliases={2: 0},
        compiler_params=_params("parallel"),
        name="matmul_residual",
    )(a, w, h)


def _ple_kernel(h_ref, y_ref, g_ref, wg_ref, p_ref, wp_ref, fg_ref, o_ref, *, final):
    x = h_ref[...] + y_ref[...]
    xn = _rms(x, g_ref[...]).astype(BF16)
    gate = jax.nn.sigmoid(jnp.dot(xn, wg_ref[...], preferred_element_type=F32))
    proj = jnp.dot(p_ref[...].astype(BF16), wp_ref[...], preferred_element_type=F32)
    y = x + gate * proj
    if final:
        y = _rms(y, fg_ref[...])
    o_ref[...] = y


def ple_layer(h, y, g, w_gate, p_all, layer, w_proj, final_g, final, tm=1024):
    T, D = h.shape
    return pl.pallas_call(
        functools.partial(_ple_kernel, final=final),
        grid=(T // tm,),
        in_specs=[pl.BlockSpec((tm, D), lambda i: (i, 0)),
                  pl.BlockSpec((tm, D), lambda i: (i, 0)),
                  pl.BlockSpec((1, D), lambda i: (0, 0)),
                  pl.BlockSpec((D, D), lambda i: (0, 0)),
                  pl.BlockSpec((None, tm, PLE_DIM), lambda i: (layer, i, 0)),
                  pl.BlockSpec((PLE_DIM, D), lambda i: (0, 0)),
                  pl.BlockSpec((1, D), lambda i: (0, 0))],
        out_specs=pl.BlockSpec((tm, D), lambda i: (i, 0)),
        out_shape=jax.ShapeDtypeStruct((T, D), F32),
        input_output_aliases={0: 0},
        compiler_params=_params("parallel"),
        name="ple_layer",
    )(h, y, g.reshape(1, D), w_gate, p_all, w_proj, final_g.reshape(1, D))


def _ssd_kernel(proj_ref, dt_ref, cw_ref, cb_ref, dtb_ref, alog_ref, dsk_ref, ng_ref, ewide_ref, efeat_ref,
                o_ref, xbuf, state, ybuf, ibuf, cwide, cfeat, xw_s, *, L):
    DI, P, N, R = SSD_D_INNER, SSD_HEAD_DIM, SSD_STATE, SSD_HEADS_PER_GROUP
    GW = DI // SSD_GROUPS
    PAIR = 2 * P
    assert PAIR == LANE and L == LANE

    @pl.when(pl.program_id(1) == 0)
    def _():
        xbuf[0:8, :] = jnp.zeros((8, SSD_CONV_DIM), F32)
        state[...] = jnp.zeros(state.shape, F32)

    xbc = proj_ref[0, :, DI:].astype(F32)
    xbuf[8:8 + L, :] = xbc
    conv = cb_ref[...] + cw_ref[3:4, :] * xbc
    for j in range(1, SSD_CONV):
        conv = conv + cw_ref[SSD_CONV - 1 - j:SSD_CONV - j, :] * xbuf[8 - j:8 - j + L, :]
    xbuf[0:8, :] = xbuf[L:L + 8, :]
    act = conv * jax.nn.sigmoid(conv)
    xs = act[:, :DI]
    bm = act[:, DI:DI + SSD_GROUPS * N]
    cm = act[:, DI + SSD_GROUPS * N:]

    dt = jax.nn.softplus(dt_ref[0] + dtb_ref[...])
    a = dt * (-jnp.exp(alog_ref[...]))
    row = lax.broadcasted_iota(jnp.int32, (L, L), 0)
    col = lax.broadcasted_iota(jnp.int32, (L, L), 1)
    tril = row >= col
    cum = _dot_exact_lhs01(tril.astype(BF16), a)
    cum_t = cum.T
    dt_t = dt.T

    split = lambda x: jnp.concatenate(_split3(x), axis=1)
    cum3 = split(cum)
    cwide[...] = jnp.dot(cum3, ewide_ref[...], preferred_element_type=F32)
    cfeat[...] = jnp.dot(cum3, efeat_ref[...], preferred_element_type=F32)
    dfeat = jnp.dot(split(dt), efeat_ref[...], preferred_element_type=F32)
    last = cfeat[L - 1:L, :]
    xw_s[...] = (xs * (jnp.exp(last - cfeat[...]) * dfeat)).astype(BF16)
    elast = jnp.exp(last)
    xs_b = xs.astype(BF16)
    lane = lax.broadcasted_iota(jnp.int32, (1, PAIR), 1)
    lower = lane < P
    zero = jnp.zeros((L, PAIR), BF16)

    for g in range(SSD_GROUPS):
        bg = bm[:, g * N:(g + 1) * N]
        cg = cm[:, g * N:(g + 1) * N].astype(BF16)
        cb = lax.dot_general(cg, bg.astype(BF16), _NT, preferred_element_type=F32)
        bg_t = bg.T.astype(BF16)
        for pr in range(R // 2):
            blk = slice((g * (R // 2) + pr) * PAIR, (g * (R // 2) + pr + 1) * PAIR)
            xp = xs_b[:, blk]
            halves = (jnp.where(lower, xp, zero), jnp.where(lower, zero, xp))
            y = None
            for k in range(2):
                hd = g * R + 2 * pr + k
                seg = cwide[:, hd * LANE:(hd + 1) * LANE] - cum_t[hd:hd + 1, :]
                w = jnp.exp(jnp.where(tril, seg, -jnp.inf)) * cb * dt_t[hd:hd + 1, :]
                yk = jnp.dot(w.astype(BF16), halves[k], preferred_element_type=F32)
                y = yk if y is None else y + yk
            st = state[g, :, pr * PAIR:(pr + 1) * PAIR]
            ybuf[:, blk] = y
            ibuf[:, blk] = jnp.dot(cg, st.astype(BF16), preferred_element_type=F32)
            state[g, :, pr * PAIR:(pr + 1) * PAIR] = (st * elast[:, blk]
                                                      + jnp.dot(bg_t, xw_s[:, blk], preferred_element_type=F32))

    y = ybuf[...] + ibuf[...] * jnp.exp(cfeat[...]) + dsk_ref[...] * xs
    z = proj_ref[0, :, :DI].astype(F32)
    y = y * (z * jax.nn.sigmoid(z))
    for g in range(SSD_GROUPS):
        yg = y[:, g * GW:(g + 1) * GW]
        o_ref[0, :, g * GW:(g + 1) * GW] = _rms(yg, ng_ref[:, g * GW:(g + 1) * GW]).astype(o_ref.dtype)


def ssd_core(proj, dt_raw, conv_w, conv_b, dt_bias, a_log, d_skip, norm_g):
    B, S, _ = proj.shape
    L = SSD_BLOCK
    assert S % L == 0
    pad = LANE - SSD_HEADS
    vec = lambda n: pl.BlockSpec((1, n), lambda b, c: (0, 0))
    whole = lambda a: pl.BlockSpec(a.shape, lambda b, c: (0, 0))

    def spread(width):
        e = np.zeros((LANE, SSD_HEADS * width), np.float32)
        for h in range(SSD_HEADS):
            e[h, h * width:(h + 1) * width] = 1.0
        return jnp.asarray(np.concatenate([e, e, e], axis=0), BF16)

    e_wide, e_feat = spread(LANE), spread(SSD_HEAD_DIM)
    return pl.pallas_call(
        functools.partial(_ssd_kernel, L=L),
        grid=(B, S // L),
        in_specs=[pl.BlockSpec((1, L, SSD_MAIN), lambda b, c: (b, c, 0)),
                  pl.BlockSpec((1, L, LANE), lambda b, c: (b, c, 0)),
                  pl.BlockSpec((SSD_CONV, SSD_CONV_DIM), lambda b, c: (0, 0)),
                  vec(SSD_CONV_DIM), vec(LANE), vec(LANE), vec(SSD_D_INNER), vec(SSD_D_INNER),
                  whole(e_wide), whole(e_feat)],
        out_specs=pl.BlockSpec((1, L, SSD_D_INNER), lambda b, c: (b, c, 0)),
        out_shape=jax.ShapeDtypeStruct((B, S, SSD_D_INNER), BF16),
        scratch_shapes=[pltpu.VMEM((L + 8, SSD_CONV_DIM), F32),
                        pltpu.VMEM((SSD_GROUPS, SSD_STATE, SSD_D_INNER // SSD_GROUPS), F32),
                        pltpu.VMEM((L, SSD_D_INNER), F32), pltpu.VMEM((L, SSD_D_INNER), F32),
                        pltpu.VMEM((L, SSD_HEADS * LANE), F32), pltpu.VMEM((L, SSD_D_INNER), F32),
                        pltpu.VMEM((L, SSD_D_INNER), BF16)],
        compiler_params=_params("parallel", "arbitrary"),
        name="ssd_core",
    )(proj, dt_raw,
      conv_w.reshape(SSD_CONV, SSD_CONV_DIM), conv_b.reshape(1, SSD_CONV_DIM),
      jnp.pad(dt_bias, (0, pad)).reshape(1, LANE), jnp.pad(a_log, (0, pad)).reshape(1, LANE),
      jnp.repeat(d_skip, SSD_HEAD_DIM).reshape(1, SSD_D_INNER), norm_g.reshape(1, SSD_D_INNER),
      e_wide, e_feat)


def _retention_kernel(q_ref, k_ref, v_ref, gate_ref, cos_ref, sin_ref, dec_ref, qd_ref, kd_ref,
                      cd_ref, ng_ref, o_ref, state):
    half = RET_QK_DIM // 2

    @pl.when(pl.program_id(2) == 0)
    def _():
        state[...] = jnp.zeros(state.shape, F32)

    cos = cos_ref[...]
    sin = sin_ref[...]

    def rot(x):
        x1, x2 = x[:, :half], x[:, half:]
        return jnp.concatenate([x1 * cos - x2 * sin, x1 * sin + x2 * cos], axis=-1)

    q = rot(q_ref[0].astype(F32))
    k = rot(k_ref[0].astype(F32)) * (RET_QK_DIM ** -0.5)
    v = v_ref[0]
    qb = q.astype(BF16)
    s = lax.dot_general(qb, k.astype(BF16), _NT, preferred_element_type=F32) * dec_ref[0]
    o = jnp.dot(s.astype(BF16), v, preferred_element_type=F32)
    st = state[...]
    o = o + jnp.dot(qb, st.astype(BF16), preferred_element_type=F32) * qd_ref[0]
    kt = (k * kd_ref[0]).T.astype(BF16)
    state[...] = st * cd_ref[0] + jnp.dot(kt, v, preferred_element_type=F32)
    gate = gate_ref[0].astype(F32)
    o_ref[0] = (_rms(o, ng_ref[0]) * (gate * jax.nn.sigmoid(gate))).astype(o_ref.dtype)


def retention_core(proj, norm_g):
    B, S, _ = proj.shape
    H, dk, dv, L = RET_HEADS, RET_QK_DIM, RET_V_DIM, RET_BLOCK
    assert S % L == 0 and L % CHUNK == 0
    inv = 1.0 / (ROPE_BASE ** (jnp.arange(0, dk, 2, dtype=F32) / dk))
    ang = jnp.arange(S, dtype=F32)[:, None] * inv[None, :]
    log_gamma = jnp.log1p(-jnp.exp2(-5.0 - jnp.arange(H, dtype=F32)))
    idx = jnp.arange(L, dtype=F32)
    visible = (jnp.arange(L)[None, :] // CHUNK) <= (jnp.arange(L)[:, None] // CHUNK)
    decay = jnp.where(visible[None],
                      jnp.exp(log_gamma[:, None, None] * jnp.abs(idx[:, None] - idx[None, :])), 0.0)
    q_decay = jnp.exp(log_gamma[:, None] * (idx[None, :] + 1.0))[..., None]
    k_decay = jnp.exp(log_gamma[:, None] * (L - 1.0 - idx[None, :]))[..., None]
    block_decay = jnp.exp(log_gamma * L).reshape(H, 1, 1)
    return pl.pallas_call(
        _retention_kernel,
        grid=(B, H, S // L),
        in_specs=[pl.BlockSpec((1, L, dk), lambda b, h, c: (b, c, h)),
                  pl.BlockSpec((1, L, dk), lambda b, h, c: (b, c, H + h)),
                  pl.BlockSpec((1, L, dv), lambda b, h, c: (b, c, H + h)),
                  pl.BlockSpec((1, L, dv), lambda b, h, c: (b, c, 2 * H + h)),
                  pl.BlockSpec((L, dk // 2), lambda b, h, c: (c, 0)),
                  pl.BlockSpec((L, dk // 2), lambda b, h, c: (c, 0)),
                  pl.BlockSpec((1, L, L), lambda b, h, c: (h, 0, 0)),
                  pl.BlockSpec((1, L, 1), lambda b, h, c: (h, 0, 0)),
                  pl.BlockSpec((1, L, 1), lambda b, h, c: (h, 0, 0)),
                  pl.BlockSpec((1, 1, 1), lambda b, h, c: (h, 0, 0)),
                  pl.BlockSpec((1, 1, dv), lambda b, h, c: (h, 0, 0))],
        out_specs=pl.BlockSpec((1, L, dv), lambda b, h, c: (b, c, h)),
        out_shape=jax.ShapeDtypeStruct((B, S, RET_V_WIDTH), BF16),
        scratch_shapes=[pltpu.VMEM((dk, dv), F32)],
        compiler_params=_params("parallel", "parallel", "arbitrary"),
        name="retention_core",
    )(proj, proj, proj, proj, jnp.cos(ang), jnp.sin(ang), decay, q_decay, k_decay, block_decay,
      norm_g.reshape(H, 1, dv))


def _attn_kernel(*refs, mode, T):
    if mode == "diff":
        q_ref, k_ref, v_ref, bias_ref, lam_ref, ng_ref, o_ref, m_s, acc_s, l_s = refs
    else:
        q_ref, k_ref, v_ref, ck_ref, cq_ref, o_ref, m_s, acc_s, cq_s = refs
    i = pl.program_id(2)
    NQ = ATTN_QUERY_TILES
    half = HEAD_LANES // 2
    reps = T // HEAD_LANES
    lane = lax.broadcasted_iota(jnp.int32, (1, HEAD_LANES), 1)
    first = lane < half
    second = jnp.logical_not(first)
    qs = []
    for u in range(NQ):
        q = (q_ref[0, u * T:(u + 1) * T, :].astype(F32) * (half ** -0.5 * LOG2E)).astype(BF16)
        zero = jnp.zeros_like(q)
        qs.append((jnp.where(first, q, zero), jnp.where(first, zero, q)))

    m_s[...] = jnp.full(m_s.shape, -jnp.inf, F32)
    acc_s[...] = jnp.zeros(acc_s.shape, F32)
    if mode == "diff":
        l_s[...] = jnp.zeros(l_s.shape, F32)
    else:
        for u in range(NQ):
            for a in range(2):
                cq_s[u, a] = jnp.broadcast_to(cq_ref[0, 0, u * T:(u + 1) * T, a:a + 1], (T, T))

    def step(j, subtiles, diagonal):
        start = pl.multiple_of(j * T, T)
        k = k_ref[0, pl.ds(start, T), :]
        v = v_ref[0, pl.ds(start, T), :]
        if mode == "diff":
            vs = (v, v)
        else:
            ones = jnp.ones_like(v)
            vs = (jnp.where(first, v, ones), jnp.where(second, v, ones))
        if diagonal is not None:
            row = lax.broadcasted_iota(jnp.int32, (T, T), 0)
            col = lax.broadcasted_iota(jnp.int32, (T, T), 1)
            if mode == "diff":
                visible = (col // CHUNK) <= (row // CHUNK)
            else:
                visible = col <= row
        for u in subtiles:
            for a in range(2):
                s = lax.dot_general(qs[u][a], k, _NT, preferred_element_type=F32)
                if mode == "diff":
                    s = s + bias_ref[0, jnp.minimum(NQ * i + u - j, 2)]
                else:
                    s = (s - ck_ref[0, 0, j, a:a + 1, :]) + cq_s[u, a]
                if u == diagonal:
                    s = jnp.where(visible, s, -jnp.inf)
                m_prev = m_s[u, a]
                m_new = jnp.maximum(m_prev, jnp.max(s, axis=-1, keepdims=True))
                alpha = jnp.exp2(m_prev - m_new)
                p = jnp.exp2(s - jnp.concatenate([m_new] * reps, axis=1))
                if mode == "diff":
                    l_s[u, a] = alpha * l_s[u, a] + jnp.sum(p, axis=-1, keepdims=True)
                acc_s[u, a] = alpha * acc_s[u, a] + jnp.dot(p.astype(BF16), vs[a], preferred_element_type=F32)
                m_s[u, a] = m_new

    everyone = tuple(range(NQ))
    lax.fori_loop(0, NQ * i, lambda j, c: (step(j, everyone, None), c)[1], 0)
    for d in range(NQ):
        step(NQ * i + d, everyone[d:], d)

    for u in range(NQ):
        if mode == "diff":
            o = acc_s[u, 0] / l_s[u, 0] - lam_ref[0] * (acc_s[u, 1] / l_s[u, 1])
            o = _rms(o, ng_ref[...])
        else:
            o0, o1 = acc_s[u, 0], acc_s[u, 1]
            o = jnp.where(first, o0 / pltpu.roll(o0, half, 1), o1 / pltpu.roll(o1, half, 1))
        o_ref[0, u * T:(u + 1) * T, :] = o.astype(o_ref.dtype)


def _attn_call(mode, proj, n_blocks, extra_inputs, extra_specs):
    B, S, _ = proj.shape
    T, NQ = ATTN_TILE, ATTN_QUERY_TILES
    assert S % (NQ * T) == 0
    stats = pltpu.VMEM((NQ, 2, T, HEAD_LANES), F32)
    return pl.pallas_call(
        functools.partial(_attn_kernel, mode=mode, T=T),
        grid=(B, n_blocks, S // (NQ * T)),
        in_specs=[pl.BlockSpec((1, NQ * T, HEAD_LANES), lambda b, h, i: (b, i, h)),
                  pl.BlockSpec((1, S, HEAD_LANES), lambda b, h, i: (b, 0, n_blocks + h)),
                  pl.BlockSpec((1, S, HEAD_LANES), lambda b, h, i: (b, 0, 2 * n_blocks + h))] + extra_specs,
        out_specs=pl.BlockSpec((1, NQ * T, HEAD_LANES), lambda b, h, i: (b, i, h)),
        out_shape=jax.ShapeDtypeStruct((B, S, D_MODEL), BF16),
        scratch_shapes=[stats, stats, stats if mode == "diff" else pltpu.VMEM((NQ, 2, T, T), F32)],
        compiler_params=_params("parallel", "parallel", "arbitrary"),
        name=mode + "_attention",
    )(proj, proj, proj, *extra_inputs)


def _t5_bucket(rel):
    nb = REL_BUCKETS // 2
    max_exact = nb // 2
    ret = (rel > 0).astype(jnp.int32) * nb
    n = jnp.abs(rel)
    nf = jnp.maximum(n, 1).astype(F32)
    large = max_exact + (jnp.log(nf / max_exact) / math.log(REL_MAX_DIST / max_exact)
                         * (nb - max_exact)).astype(jnp.int32)
    large = jnp.minimum(large, nb - 1)
    return ret + jnp.where(n < max_exact, n, large)


def _bias_tiles_kernel(bucket_ref, table_ref, o_ref):
    h = pl.program_id(0)
    for d in range(bucket_ref.shape[0]):
        bucket = bucket_ref[d]
        tile = jnp.zeros(bucket.shape, F32)
        for b in range(REL_BUCKETS):
            tile = jnp.where(bucket == b, table_ref[b * DIFF_HEADS + h] * LOG2E, tile)
        o_ref[0, d] = tile


def diff_bias_tiles(rel_bias):
    T = ATTN_TILE
    assert T >= REL_MAX_DIST
    off = jnp.arange(T)
    rel = (off[None, None, :] - off[None, :, None]) - (jnp.arange(3) * T)[:, None, None]
    return pl.pallas_call(
        _bias_tiles_kernel,
        grid=(DIFF_HEADS,),
        in_specs=[pl.BlockSpec((3, T, T), lambda h: (0, 0, 0)),
                  pl.BlockSpec(memory_space=pltpu.SMEM)],
        out_specs=pl.BlockSpec((1, 3, T, T), lambda h: (h, 0, 0, 0)),
        out_shape=jax.ShapeDtypeStruct((DIFF_HEADS, 3, T, T), F32),
        compiler_params=_params("parallel"),
        name="diff_bias_tiles",
    )(_t5_bucket(rel), rel_bias.astype(F32).reshape(REL_BUCKETS * DIFF_HEADS))


def diff_core(proj, lam_vecs, norm_g, rel_bias, lam_init):
    T = ATTN_TILE
    lv = lam_vecs.astype(F32)
    lam = jnp.exp(jnp.sum(lv[0] * lv[1])) - jnp.exp(jnp.sum(lv[2] * lv[3])) + lam_init
    bias = diff_bias_tiles(rel_bias)
    g = (norm_g * (1.0 - lam_init)).reshape(1, HEAD_LANES)
    extra_specs = [pl.BlockSpec((1, 3, T, T), lambda b, h, i: (h, 0, 0, 0)),
                   pl.BlockSpec(memory_space=pltpu.SMEM),
                   pl.BlockSpec((1, HEAD_LANES), lambda b, h, i: (0, 0))]
    return _attn_call("diff", proj, DIFF_HEADS, [bias, lam.reshape(1), g], extra_specs)


def _fox_gate_kernel(h_ref, g_ref, w_ref, b_ref, ct_ref, c_ref, carry, *, L):
    @pl.when(pl.program_id(1) == 0)
    def _():
        carry[...] = jnp.zeros(carry.shape, F32)

    xn = _rms(h_ref[0], g_ref[...])
    logits = lax.dot_general(w_ref[...], xn.astype(BF16), _NT, preferred_element_type=F32)
    log_f = jax.nn.log_sigmoid(logits + b_ref[...])
    row = lax.broadcasted_iota(jnp.int32, (L, L), 0)
    col = lax.broadcasted_iota(jnp.int32, (L, L), 1)
    triu = (row <= col).astype(BF16)
    hi, mid, lo = _split3(log_f)
    d = functools.partial(jnp.dot, preferred_element_type=F32)
    c = carry[...] + ((d(hi, triu) + d(mid, triu)) + d(lo, triu))
    carry[...] = c[:, L - 1:L]
    c2 = c * LOG2E
    ct_ref[0] = c2
    c_ref[0] = c2.T


def fox_gate(h3, g, w_f, b_f):
    B, S, D = h3.shape
    L = FOX_GATE_BLOCK
    assert S % L == 0
    pad = LANE - FOX_HEADS
    return pl.pallas_call(
        functools.partial(_fox_gate_kernel, L=L),
        grid=(B, S // L),
        in_specs=[pl.BlockSpec((1, L, D), lambda b, c: (b, c, 0)),
                  pl.BlockSpec((1, D), lambda b, c: (0, 0)),
                  pl.BlockSpec((LANE, D), lambda b, c: (0, 0)),
                  pl.BlockSpec((LANE, 1), lambda b, c: (0, 0))],
        out_specs=[pl.BlockSpec((1, LANE, L), lambda b, c: (b, 0, c)),
                   pl.BlockSpec((1, L, LANE), lambda b, c: (b, c, 0))],
        out_shape=[jax.ShapeDtypeStruct((B, LANE, S), F32), jax.ShapeDtypeStruct((B, S, LANE), F32)],
        scratch_shapes=[pltpu.VMEM((LANE, 1), F32)],
        compiler_params=_params("parallel", "arbitrary"),
        name="fox_gate",
    )(h3, g.reshape(1, D), jnp.pad(w_f.T, ((0, pad), (0, 0))).astype(BF16),
      jnp.pad(b_f, (0, pad)).reshape(LANE, 1))


def fox_core(proj, c_t, c_rows):
    B, S, _ = proj.shape
    T = ATTN_TILE
    nb = FOX_HEADS // 2
    ck = jnp.transpose(c_t[:, :FOX_HEADS].reshape(B, nb, 2, S // T, T), (0, 1, 3, 2, 4))
    cq = jnp.transpose(c_rows[:, :, :FOX_HEADS].reshape(B, S, nb, 2), (0, 2, 1, 3))
    extra_specs = [pl.BlockSpec((1, 1, S // T, 2, T), lambda b, h, i: (b, h, 0, 0, 0)),
                   pl.BlockSpec((1, 1, ATTN_QUERY_TILES * T, 2), lambda b, h, i: (b, h, i, 0))]
    return _attn_call("fox", proj, nb, [ck, cq], extra_specs)


def _candidate_tables():
    pairs = [(a, b) for a in range(PEER_TOPK) for b in range(PEER_TOPK) if (a + 1) * (b + 1) <= PEER_TOPK]
    rows = -(-len(pairs) // BF16_ROWS) * BF16_ROWS
    p1 = np.zeros((rows, LANE), np.float32)
    p2 = np.zeros((rows, LANE), np.float32)
    for r, (a, b) in enumerate(pairs):
        p1[r, a] = 1.0
        p2[r, b] = 1.0
    return len(pairs), p1, p2


N_CAND, _P1, _P2 = _candidate_tables()


_INT_MIN = -2 ** 31
_FLIP = 0x7FFFFFFF


def _ordered_int(x):
    b = lax.bitcast_convert_type(x + 0.0, jnp.int32)
    return jnp.where(b < 0, b ^ _FLIP, b)


def _ordered_float(k):
    return lax.bitcast_convert_type(jnp.where(k < 0, k ^ _FLIP, k), F32)


_FLOOR = -3.0e38
_MARK, _MARK_STEP = -3.2e38, 1.0e36


def _top16_ranks(x, break_ties):
    n, t = x.shape
    idx = lax.broadcasted_iota(jnp.int32, (n, t), 0)
    ridx = lax.broadcasted_iota(jnp.int32, (PEER_TOPK, t), 0)
    work = _ordered_int(x) if break_ties else jnp.maximum(x, _FLOOR)
    vals = jnp.zeros((PEER_TOPK, t), work.dtype)
    for r in range(PEER_TOPK):
        mx = jnp.max(work, axis=0, keepdims=True)
        hit = work == mx
        if break_ties:
            pos = jnp.min(jnp.where(hit, idx, n), axis=0, keepdims=True)
            hit = idx == pos
        work = jnp.where(hit, _INT_MIN + r if break_ties else _MARK - r * _MARK_STEP, work)
        vals = jnp.where(ridx == r, mx, vals)
    if break_ties:
        rank = jnp.where(work < _INT_MIN + PEER_TOPK, work & (2 * PEER_TOPK - 1), PEER_TOPK)
        return rank, _ordered_float(vals)
    rank = jnp.where(work < 0.5 * (_FLOOR + _MARK),
                     jnp.round((_MARK - work) * (1.0 / _MARK_STEP)).astype(jnp.int32), PEER_TOPK)
    return rank, vals


def _peer_select_kernel(q_ref, keys_ref, p1_ref, p2_ref, p1t_ref, rank2_ref, e2_ref, nrow_ref, e1_ref):
    def scores(c):
        kk = keys_ref[0, c]
        qq = q_ref[:, c * PEER_HALF:(c + 1) * PEER_HALF]
        kh, km, _ = _split3(kk)
        qh, qm, _ = _split3(qq)
        d = lambda a, b: lax.dot_general(a, b, _NT, preferred_element_type=F32)
        return d(kh, qh) + (d(kh, qm) + d(km, qh))

    s1 = scores(0)
    s2 = scores(1)
    out_refs = (rank2_ref, e2_ref, nrow_ref, e1_ref)
    clean = _peer_select_pass(s1, s2, p1_ref, p2_ref, p1t_ref, out_refs, break_ties=False)

    @pl.when(jnp.logical_not(clean))
    def _():
        _peer_select_pass(s1, s2, p1_ref, p2_ref, p1t_ref, out_refs, break_ties=True)


def _peer_select_pass(s1, s2, p1_ref, p2_ref, p1t_ref, out_refs, break_ties):
    rank2_ref, e2_ref, nrow_ref, e1_ref = out_refs
    rank1, v1 = _top16_ranks(s1, break_ties)
    rank2, v2 = _top16_ranks(s2, break_ties)

    tokens = s1.shape[1]
    pad = jnp.zeros((LANE - PEER_TOPK, tokens), F32)
    cand = (_dot_exact_lhs01(p1_ref[...], jnp.concatenate([v1, pad], axis=0))
            + _dot_exact_lhs01(p2_ref[...], jnp.concatenate([v2, pad], axis=0)))
    cidx = lax.broadcasted_iota(jnp.int32, cand.shape, 0)
    cand = jnp.where(cidx < N_CAND, cand, -jnp.inf)
    cand_rank, _ = _top16_ranks(cand, break_ties)
    sel = jnp.where(cand_rank < PEER_TOPK, 1.0, 0.0)
    top = v1[0:1, :] + v2[0:1, :]
    z = jnp.sum(sel * jnp.exp(jnp.where(sel > 0.0, cand - top, 0.0)), axis=0, keepdims=True)
    sel_pad = jnp.concatenate([sel, jnp.zeros((LANE - sel.shape[0], tokens), F32)], axis=0).astype(BF16)
    n_by_rank = jnp.dot(p1t_ref[...], sel_pad, preferred_element_type=F32)
    nrow = jnp.zeros(s1.shape, F32)
    for a in range(PEER_TOPK):
        nrow = jnp.where(rank1 == a, n_by_rank[a:a + 1, :], nrow)

    rank2_ref[0] = rank2.astype(F32).astype(BF16)
    e2_ref[0] = jnp.exp(s2 - v2[0:1, :]).astype(BF16)
    e1 = jnp.exp(s1 - v1[0:1, :]) * (0.5 / z)
    for g in range(PEER_KEYS // SUBLANE):
        for st in range(tokens // LANE):
            tile = (slice(g * SUBLANE, (g + 1) * SUBLANE), slice(st * LANE, (st + 1) * LANE))
            nrow_ref[0, g, st] = nrow[tile]
            e1_ref[0, g, st] = e1[tile]

    if break_ties:
        return None
    ranked = (jnp.sum(jnp.where(rank1 < PEER_TOPK, 1.0, 0.0), axis=0, keepdims=True)
              + jnp.sum(jnp.where(rank2 < PEER_TOPK, 1.0, 0.0), axis=0, keepdims=True)
              + jnp.sum(sel, axis=0, keepdims=True))
    return jnp.max(ranked) == 3.0 * PEER_TOPK


def peer_select(q, keys):
    T = q.shape[0]
    tm = PEER_SELECT_TOKENS
    assert T % tm == 0
    hk = pl.BlockSpec((1, PEER_KEYS, tm), lambda i, h: (h, 0, i))
    groups = PEER_KEYS // SUBLANE
    hk1 = pl.BlockSpec((1, groups, tm // LANE, SUBLANE, LANE), lambda i, h: (h, 0, i, 0, 0))
    shp = lambda dt: jax.ShapeDtypeStruct((PEER_HEADS, PEER_KEYS, T), dt)
    shp1 = jax.ShapeDtypeStruct((PEER_HEADS, groups, T // LANE, SUBLANE, LANE), F32)
    cst = lambda a: pl.BlockSpec(a.shape, lambda i, h: (0, 0))
    p1, p2 = jnp.asarray(_P1, BF16), jnp.asarray(_P2, BF16)
    p1t = jnp.asarray(np.pad(_P1.T, ((0, 0), (0, LANE - _P1.shape[0]))), BF16)
    return pl.pallas_call(
        _peer_select_kernel,
        grid=(T // tm, PEER_HEADS),
        in_specs=[pl.BlockSpec((tm, PEER_QUERY_DIM), lambda i, h: (i, h)),
                  pl.BlockSpec((1, 2, PEER_KEYS, PEER_HALF), lambda i, h: (h, 0, 0, 0)),
                  cst(p1), cst(p2), cst(p1t)],
        out_specs=[hk, hk, hk1, hk1],
        out_shape=[shp(BF16), shp(BF16), shp1, shp1],
        compiler_params=_params("parallel", "parallel"),
        name="peer_select",
    )(q, keys, p1, p2, p1t)


def _peer_dense_kernel(xt_ref, u_ref, u_next_ref, vt_ref, rank2_ref, e2_ref, nrow_ref, e1_ref,
                       nrow_next_ref, e1_next_ref, o_ref, *scratch):
    n = xt_ref.shape[0]
    acc, gbuf, hbuf, wbuf = scratch[:n], scratch[n:2 * n], scratch[2 * n:2 * n + 2], scratch[2 * n + 2:]
    j = pl.program_id(1)
    rows = PEER_EXPERT_TILE // PEER_KEYS
    tc = PEER_TOKEN_CHUNK
    n_chunks = xt_ref.shape[0]
    pack = BF16_ROWS

    def first_matmul(c, u):
        hbuf[c % 2][...] = jnp.dot(u[...], xt_ref[c], preferred_element_type=F32).astype(BF16)

    def gate_weights(c, nrow, e1):
        cols = slice(c * tc, (c + 1) * tc)
        for ii in range(rows):
            w = jnp.zeros((PEER_KEYS, tc), BF16)
            for h in range(PEER_HEADS):
                def spread(ref):
                    tiles = [jnp.broadcast_to(ref[h, 0, c * (tc // LANE) + lt, ii:ii + 1, :],
                                              (pack, LANE)).astype(BF16) for lt in range(tc // LANE)]
                    return jnp.concatenate([jnp.concatenate(tiles, axis=1)] * (PEER_KEYS // pack), axis=0)
                w = w + jnp.where(rank2_ref[h, :, cols] < spread(nrow), e2_ref[h, :, cols] * spread(e1),
                                  jnp.zeros_like(w))
            wbuf[c % 2][ii * PEER_KEYS:(ii + 1) * PEER_KEYS, :] = w

    def finish(c):
        for ii in range(rows):
            r = slice(ii * PEER_KEYS, (ii + 1) * PEER_KEYS)
            hr = hbuf[c % 2][r, :]
            act = hr * (1.0 + lax.erf(hr * jnp.asarray(2.0 ** -0.5, BF16)))
            gbuf[c][r, :] = wbuf[c % 2][r, :] * act
        acc[c][...] += jnp.dot(vt_ref[...], gbuf[c][...], preferred_element_type=F32)

    @pl.when(j == 0)
    def _():
        for a in acc:
            a[...] = jnp.zeros(a.shape, F32)
        first_matmul(0, u_ref)
        gate_weights(0, nrow_ref, e1_ref)

    for c in range(n_chunks):
        if c + 1 < n_chunks:
            first_matmul(c + 1, u_ref)
            gate_weights(c + 1, nrow_ref, e1_ref)
        else:
            first_matmul(0, u_next_ref)
            gate_weights(0, nrow_next_ref, e1_next_ref)
        finish(c)

    @pl.when(j == pl.num_programs(1) - 1)
    def _():
        for c in range(n_chunks):
            o_ref[c * tc:(c + 1) * tc, :] = acc[c][...].T


def peer_dense(xt, u, vt, rank2, e2, nrow, e1):
    n_slabs, D, tc = xt.shape
    T = n_slabs * tc
    tm, te = PEER_TOKENS, PEER_EXPERT_TILE
    rows = te // PEER_KEYS
    n_tiles = PEER_EXPERTS // te
    assert rows == SUBLANE and tc == PEER_TOKEN_CHUNK and (tm // tc) % 2 == 0
    assert T % tm == 0
    nxt = lambda j: jnp.minimum(j + 1, n_tiles - 1)
    by_key2 = pl.BlockSpec((PEER_HEADS, PEER_KEYS, tm), lambda i, j: (0, 0, i))
    by_key1 = pl.BlockSpec((PEER_HEADS, 1, tm // LANE, rows, LANE), lambda i, j: (0, j, i, 0, 0))
    by_key1_next = pl.BlockSpec((PEER_HEADS, 1, tm // LANE, rows, LANE), lambda i, j: (0, nxt(j), i, 0, 0))
    return pl.pallas_call(
        _peer_dense_kernel,
        grid=(T // tm, n_tiles),
        in_specs=[pl.BlockSpec((tm // tc, D, tc), lambda i, j: (i, 0, 0)),
                  pl.BlockSpec((te, D), lambda i, j: (j, 0)),
                  pl.BlockSpec((te, D), lambda i, j: (nxt(j), 0)),
                  pl.BlockSpec((D, te), lambda i, j: (0, j)),
                  by_key2, by_key2, by_key1, by_key1, by_key1_next, by_key1_next],
        out_specs=pl.BlockSpec((tm, D), lambda i, j: (i, 0)),
        out_shape=jax.ShapeDtypeStruct((T, D), F32),
        scratch_shapes=([pltpu.VMEM((D, tc), F32)] * (tm // tc) + [pltpu.VMEM((te, tc), BF16)] * (tm // tc)
                        + [pltpu.VMEM((te, tc), BF16)] * 4),
        compiler_params=pltpu.CompilerParams(dimension_semantics=("parallel", "arbitrary"),
                                             vmem_limit_bytes=PEER_DENSE_VMEM),
        name="peer_dense",
    )(xt, u, u, vt, rank2, e2, nrow, e1, nrow, e1)


def peer_layer(h, g, w_q, keys, u, v):
    q, xt = norm_matmul(h, g, w_q.astype(BF16), F32, emit_t=True)
    rank2, e2, nrow, e1 = peer_select(q, keys)
    return peer_dense(xt, u.astype(BF16), v.T.astype(BF16), rank2, e2, nrow, e1)


def _pad_cols(w, n):
    return jnp.pad(w, ((0, 0), (0, n - w.shape[1])))


def kernel(x, p, mix_norm, ffn_norm, ple_norm, final_norm, ssd_w_in, ssd_conv_w, ssd_conv_b, ssd_dt_bias, ssd_a_log, ssd_d, ssd_norm, ssd_w_out, ret_w_in, ret_norm, ret_w_out, diff_w_in, diff_lambda, diff_norm, diff_w_out, fox_w_in, fox_b_f, fox_w_out, rel_bias, peer_w_q, peer_keys, peer_u, peer_v, ple_proj, ple_gate):
    B, S, D = x.shape
    T = B * S
    depth = mix_norm.shape[0]
    n_mixers = 4
    h = x.reshape(T, D)
    p_all = p.reshape(depth, T, PLE_DIM)
    for i in range(depth):
        m, j = i % n_mixers, i // n_mixers
        g = mix_norm[i]
        if m == 0:
            w = ssd_w_in[j]
            proj = norm_matmul(h, g, w[:, :SSD_MAIN].astype(BF16), BF16)
            dt_raw = norm_matmul(h, g, _pad_cols(w[:, SSD_MAIN:], LANE).astype(BF16), F32)
            y = ssd_core(proj.reshape(B, S, SSD_MAIN), dt_raw.reshape(B, S, LANE), ssd_conv_w[j],
                         ssd_conv_b[j], ssd_dt_bias[j], ssd_a_log[j], ssd_d[j], ssd_norm[j])
            w_out = ssd_w_out[j]
        elif m == 1:
            proj = norm_matmul(h, g, ret_w_in[j].astype(BF16), BF16)
            y = retention_core(proj.reshape(B, S, -1), ret_norm[j])
            w_out = ret_w_out[j]
        elif m == 2:
            lam_init = 0.8 - 0.6 * math.exp(-0.3 * i)
            proj = norm_matmul(h, g, diff_w_in[j].astype(BF16), BF16)
            y = diff_core(proj.reshape(B, S, -1), diff_lambda[j], diff_norm[j], rel_bias, lam_init)
            w_out = diff_w_out[j]
        else:
            w = fox_w_in[j]
            proj = norm_matmul(h, g, w[:, :3 * D].astype(BF16), BF16)
            c_t, c_rows = fox_gate(h.reshape(B, S, D), g, w[:, 3 * D:], fox_b_f[j])
            y = fox_core(proj.reshape(B, S, -1), c_t, c_rows)
            w_out = fox_w_out[j]
        h = matmul_residual(y.reshape(T, -1), w_out.astype(BF16), h)
        y = peer_layer(h, ffn_norm[i], peer_w_q[i], peer_keys[i], peer_u[i], peer_v[i])
        h = ple_layer(h, y, ple_norm[i], ple_gate[i].astype(BF16), p_all, i, ple_proj[i].astype(BF16),
                      final_norm, final=(i == depth - 1))
    return h.reshape(B, S, D)
```

```python
import functools
import math

import jax
import jax.numpy as jnp
import numpy as np
from jax import lax
from jax.experimental import pallas as pl
from jax.experimental.pallas import tpu as pltpu

F32 = jnp.float32
BF16 = jnp.bfloat16

D_MODEL = 1024
CHUNK = 64
NORM_EPS = 1e-6
LOG2E = math.log2(math.e)
ROPE_BASE = 10000.0
PLE_DIM = 256

SSD_D_INNER = 2 * D_MODEL
SSD_HEAD_DIM = 64
SSD_HEADS = SSD_D_INNER // SSD_HEAD_DIM
SSD_GROUPS = 4
SSD_HEADS_PER_GROUP = SSD_HEADS // SSD_GROUPS
SSD_STATE = 128
SSD_CONV = 4
SSD_CONV_DIM = SSD_D_INNER + 2 * SSD_GROUPS * SSD_STATE
SSD_MAIN = SSD_D_INNER + SSD_CONV_DIM
SSD_BLOCK = 128

RET_HEADS = 4
RET_QK_DIM = D_MODEL // RET_HEADS
RET_V_DIM = 2 * RET_QK_DIM
RET_V_WIDTH = RET_HEADS * RET_V_DIM
RET_BLOCK = 256

DIFF_HEADS = 8
DIFF_HEAD_DIM = D_MODEL // DIFF_HEADS // 2
FOX_HEADS = 16
FOX_HEAD_DIM = D_MODEL // FOX_HEADS
FOX_GATE_BLOCK = 512
ATTN_TILE = 512
ATTN_QUERY_TILES = 4
HEAD_LANES = 128

REL_BUCKETS = 32
REL_MAX_DIST = 128

PEER_KEYS = 128
PEER_EXPERTS = PEER_KEYS * PEER_KEYS
PEER_HEADS = 8
PEER_TOPK = 16
PEER_QUERY_DIM = 256
PEER_HALF = PEER_QUERY_DIM // 2
PEER_TOKENS = 1024
PEER_TOKEN_CHUNK = 512
PEER_EXPERT_TILE = 1024
PEER_SELECT_TOKENS = 1024
PEER_DENSE_VMEM = 56 * 1024 * 1024

LANE = 128
SUBLANE = 8
BF16_ROWS = 2 * SUBLANE
VMEM_LIMIT = 48 * 1024 * 1024

_NT = (((1,), (1,)), ((), ()))


def _params(*sem):
    return pltpu.CompilerParams(dimension_semantics=sem, vmem_limit_bytes=VMEM_LIMIT)


def _rms(x, g):
    return x * lax.rsqrt(jnp.mean(x * x, axis=-1, keepdims=True) + NORM_EPS) * g


def _split3(x):
    hi = x.astype(BF16)
    r1 = x - hi.astype(F32)
    mid = r1.astype(BF16)
    lo = (r1 - mid.astype(F32)).astype(BF16)
    return hi, mid, lo


def _dot_exact_lhs01(m01, x):
    hi, mid, lo = _split3(x)
    d = functools.partial(jnp.dot, preferred_element_type=F32)
    return (d(m01, hi) + d(m01, mid)) + d(m01, lo)


def _norm_matmul_kernel(h_ref, g_ref, w_ref, *rest, emit_t, tail):
    if tail:
        wt_ref, o_ref, tail_ref, xn_ref = rest
    elif emit_t:
        o_ref, xt_ref, xn_ref = rest
    else:
        o_ref, xn_ref = rest

    @pl.when(pl.program_id(1) == 0)
    def _():
        y = _rms(h_ref[...], g_ref[...])
        xn_ref[...] = y.astype(BF16)
        if tail:
            tail_ref[...] = jnp.dot(xn_ref[...], wt_ref[...], preferred_element_type=F32)
        if emit_t:
            tc = xt_ref.shape[2]
            for c in range(xt_ref.shape[0]):
                xt_ref[c] = y[c * tc:(c + 1) * tc, :].T.astype(BF16)

    o_ref[...] = jnp.dot(xn_ref[...], w_ref[...], preferred_element_type=F32).astype(o_ref.dtype)


def norm_matmul(h, g, w, out_dtype, emit_t=False, w_tail=None, tm=1024, tn=1024):
    T, D = h.shape
    N = w.shape[1]
    tn = min(tn, N)
    assert T % tm == 0 and N % tn == 0 and not (emit_t and w_tail is not None)
    out_shape = [jax.ShapeDtypeStruct((T, N), out_dtype)]
    out_specs = [pl.BlockSpec((tm, tn), lambda i, j: (i, j))]
    in_specs = [pl.BlockSpec((tm, D), lambda i, j: (i, 0)),
                pl.BlockSpec((1, D), lambda i, j: (0, 0)),
                pl.BlockSpec((D, tn), lambda i, j: (0, j))]
    operands = [h, g.reshape(1, D), w]
    if w_tail is not None:
        nt = w_tail.shape[1]
        in_specs.append(pl.BlockSpec((D, nt), lambda i, j: (0, 0)))
        operands.append(w_tail)
        out_shape.append(jax.ShapeDtypeStruct((T, nt), F32))
        out_specs.append(pl.BlockSpec((tm, nt), lambda i, j: (i, 0)))
    if emit_t:
        tc = PEER_TOKEN_CHUNK
        out_shape.append(jax.ShapeDtypeStruct((T // tc, D, tc), BF16))
        out_specs.append(pl.BlockSpec((tm // tc, D, tc), lambda i, j: (i, 0, 0)))
    res = pl.pallas_call(
        functools.partial(_norm_matmul_kernel, emit_t=emit_t, tail=w_tail is not None),
        grid=(T // tm, N // tn),
        in_specs=in_specs,
        out_specs=out_specs,
        out_shape=out_shape,
        scratch_shapes=[pltpu.VMEM((tm, D), BF16)],
        compiler_params=_params("parallel", "arbitrary"),
        name="norm_matmul",
    )(*operands)
    return res if (emit_t or w_tail is not None) else res[0]


def _matmul_residual_kernel(a_ref, w_ref, h_ref, o_ref):
    o_ref[...] = h_ref[...] + jnp.dot(a_ref[...], w_ref[...], preferred_element_type=F32)


def matmul_residual(a, w, h, tm=1024):
    T, K = a.shape
    N = w.shape[1]
    return pl.pallas_call(
        _matmul_residual_kernel,
        grid=(T // tm,),
        in_specs=[pl.BlockSpec((tm, K), lambda i: (i, 0)),
                  pl.BlockSpec((K, N), lambda i: (0, 0)),
                  pl.BlockSpec((tm, N), lambda i: (i, 0))],
        out_specs=pl.BlockSpec((tm, N), lambda i: (i, 0)),
        out_shape=jax.ShapeDtypeStruct((T, N), F32),
        input_output_aliases={2: 0},
        compiler_params=_params("parallel"),
        name="matmul_residual",
    )(a, w, h)


def _ple_kernel(h_ref, y_ref, g_ref, wg_ref, p_ref, wp_ref, fg_ref, o_ref, *, final):
    x = h_ref[...] + y_ref[...]
    xn = _rms(x, g_ref[...]).astype(BF16)
    gate = jax.nn.sigmoid(jnp.dot(xn, wg_ref[...], preferred_element_type=F32))
    proj = jnp.dot(p_ref[...].astype(BF16), wp_ref[...], preferred_element_type=F32)
    y = x + gate * proj
    if final:
        y = _rms(y, fg_ref[...])
    o_ref[...] = y


def ple_layer(h, y, g, w_gate, p_all, layer, w_proj, final_g, final, tm=1024):
    T, D = h.shape
    return pl.pallas_call(
        functools.partial(_ple_kernel, final=final),
        grid=(T // tm,),
        in_specs=[pl.BlockSpec((tm, D), lambda i: (i, 0)),
                  pl.BlockSpec((tm, D), lambda i: (i, 0)),
                  pl.BlockSpec((1, D), lambda i: (0, 0)),
                  pl.BlockSpec((D, D), lambda i: (0, 0)),
                  pl.BlockSpec((None, tm, PLE_DIM), lambda i: (layer, i, 0)),
                  pl.BlockSpec((PLE_DIM, D), lambda i: (0, 0)),
                  pl.BlockSpec((1, D), lambda i: (0, 0))],
        out_specs=pl.BlockSpec((tm, D), lambda i: (i, 0)),
        out_shape=jax.ShapeDtypeStruct((T, D), F32),
        input_output_aliases={0: 0},
        compiler_params=_params("parallel"),
        name="ple_layer",
    )(h, y, g.reshape(1, D), w_gate, p_all, w_proj, final_g.reshape(1, D))


def _ssd_kernel(proj_ref, dt_ref, cw_ref, cb_ref, dtb_ref, alog_ref, dsk_ref, ng_ref, ewide_ref, efeat_ref,
                o_ref, xbuf, state, ybuf, ibuf, cwide, cfeat, xw_s, *, L):
    DI, P, N, R = SSD_D_INNER, SSD_HEAD_DIM, SSD_STATE, SSD_HEADS_PER_GROUP
    GW = DI // SSD_GROUPS
    PAIR = 2 * P
    assert PAIR == LANE and L == LANE

    @pl.when(pl.program_id(1) == 0)
    def _():
        xbuf[0:8, :] = jnp.zeros((8, SSD_CONV_DIM), F32)
        state[...] = jnp.zeros(state.shape, F32)

    xbc = proj_ref[0, :, DI:].astype(F32)
    xbuf[8:8 + L, :] = xbc
    conv = cb_ref[...] + cw_ref[3:4, :] * xbc
    for j in range(1, SSD_CONV):
        conv = conv + cw_ref[SSD_CONV - 1 - j:SSD_CONV - j, :] * xbuf[8 - j:8 - j + L, :]
    xbuf[0:8, :] = xbuf[L:L + 8, :]
    act = conv * jax.nn.sigmoid(conv)
    xs = act[:, :DI]
    bm = act[:, DI:DI + SSD_GROUPS * N]
    cm = act[:, DI + SSD_GROUPS * N:]

    dt = jax.nn.softplus(dt_ref[0] + dtb_ref[...])
    a = dt * (-jnp.exp(alog_ref[...]))
    row = lax.broadcasted_iota(jnp.int32, (L, L), 0)
    col = lax.broadcasted_iota(jnp.int32, (L, L), 1)
    tril = row >= col
    cum = _dot_exact_lhs01(tril.astype(BF16), a)
    cum_t = cum.T
    dt_t = dt.T

    split = lambda x: jnp.concatenate(_split3(x), axis=1)
    cum3 = split(cum)
    cwide[...] = jnp.dot(cum3, ewide_ref[...], preferred_element_type=F32)
    cfeat[...] = jnp.dot(cum3, efeat_ref[...], preferred_element_type=F32)
    dfeat = jnp.dot(split(dt), efeat_ref[...], preferred_element_type=F32)
    last = cfeat[L - 1:L, :]
    xw_s[...] = (xs * (jnp.exp(last - cfeat[...]) * dfeat)).astype(BF16)
    elast = jnp.exp(last)
    xs_b = xs.astype(BF16)
    lane = lax.broadcasted_iota(jnp.int32, (1, PAIR), 1)
    lower = lane < P
    zero = jnp.zeros((L, PAIR), BF16)

    for g in range(SSD_GROUPS):
        bg = bm[:, g * N:(g + 1) * N]
        cg = cm[:, g * N:(g + 1) * N].astype(BF16)
        cb = lax.dot_general(cg, bg.astype(BF16), _NT, preferred_element_type=F32)
        bg_t = bg.T.astype(BF16)
        for pr in range(R // 2):
            blk = slice((g * (R // 2) + pr) * PAIR, (g * (R // 2) + pr + 1) * PAIR)
            xp = xs_b[:, blk]
            halves = (jnp.where(lower, xp, zero), jnp.where(lower, zero, xp))
            y = None
            for k in range(2):
                hd = g * R + 2 * pr + k
                seg = cwide[:, hd * LANE:(hd + 1) * LANE] - cum_t[hd:hd + 1, :]
                w = jnp.exp(jnp.where(tril, seg, -jnp.inf)) * cb * dt_t[hd:hd + 1, :]
                yk = jnp.dot(w.astype(BF16), halves[k], preferred_element_type=F32)
                y = yk if y is None else y + yk
            st = state[g, :, pr * PAIR:(pr + 1) * PAIR]
            ybuf[:, blk] = y
            ibuf[:, blk] = jnp.dot(cg, st.astype(BF16), preferred_element_type=F32)
            state[g, :, pr * PAIR:(pr + 1) * PAIR] = (st * elast[:, blk]
                                                      + jnp.dot(bg_t, xw_s[:, blk], preferred_element_type=F32))

    y = ybuf[...] + ibuf[...] * jnp.exp(cfeat[...]) + dsk_ref[...] * xs
    z = proj_ref[0, :, :DI].astype(F32)
    y = y * (z * jax.nn.sigmoid(z))
    for g in range(SSD_GROUPS):
        yg = y[:, g * GW:(g + 1) * GW]
        o_ref[0, :, g * GW:(g + 1) * GW] = _rms(yg, ng_ref[:, g * GW:(g + 1) * GW]).astype(o_ref.dtype)


def ssd_core(proj, dt_raw, conv_w, conv_b, dt_bias, a_log, d_skip, norm_g):
    B, S, _ = proj.shape
    L = SSD_BLOCK
    assert S % L == 0
    pad = LANE - SSD_HEADS
    vec = lambda n: pl.BlockSpec((1, n), lambda b, c: (0, 0))
    whole = lambda a: pl.BlockSpec(a.shape, lambda b, c: (0, 0))

    def spread(width):
        e = np.zeros((LANE, SSD_HEADS * width), np.float32)
        for h in range(SSD_HEADS):
            e[h, h * width:(h + 1) * width] = 1.0
        return jnp.asarray(np.concatenate([e, e, e], axis=0), BF16)

    e_wide, e_feat = spread(LANE), spread(SSD_HEAD_DIM)
    return pl.pallas_call(
        functools.partial(_ssd_kernel, L=L),
        grid=(B, S // L),
        in_specs=[pl.BlockSpec((1, L, SSD_MAIN), lambda b, c: (b, c, 0)),
                  pl.BlockSpec((1, L, LANE), lambda b, c: (b, c, 0)),
                  pl.BlockSpec((SSD_CONV, SSD_CONV_DIM), lambda b, c: (0, 0)),
                  vec(SSD_CONV_DIM), vec(LANE), vec(LANE), vec(SSD_D_INNER), vec(SSD_D_INNER),
                  whole(e_wide), whole(e_feat)],
        out_specs=pl.BlockSpec((1, L, SSD_D_INNER), lambda b, c: (b, c, 0)),
        out_shape=jax.ShapeDtypeStruct((B, S, SSD_D_INNER), BF16),
        scratch_shapes=[pltpu.VMEM((L + 8, SSD_CONV_DIM), F32),
                        pltpu.VMEM((SSD_GROUPS, SSD_STATE, SSD_D_INNER // SSD_GROUPS), F32),
                        pltpu.VMEM((L, SSD_D_INNER), F32), pltpu.VMEM((L, SSD_D_INNER), F32),
                        pltpu.VMEM((L, SSD_HEADS * LANE), F32), pltpu.VMEM((L, SSD_D_INNER), F32),
                        pltpu.VMEM((L, SSD_D_INNER), BF16)],
        compiler_params=_params("parallel", "arbitrary"),
        name="ssd_core",
    )(proj, dt_raw,
      conv_w.reshape(SSD_CONV, SSD_CONV_DIM), conv_b.reshape(1, SSD_CONV_DIM),
      jnp.pad(dt_bias, (0, pad)).reshape(1, LANE), jnp.pad(a_log, (0, pad)).reshape(1, LANE),
      jnp.repeat(d_skip, SSD_HEAD_DIM).reshape(1, SSD_D_INNER), norm_g.reshape(1, SSD_D_INNER),
      e_wide, e_feat)


def _retention_kernel(q_ref, k_ref, v_ref, gate_ref, cos_ref, sin_ref, dec_ref, qd_ref, kd_ref,
                      cd_ref, ng_ref, o_ref, state):
    half = RET_QK_DIM // 2

    @pl.when(pl.program_id(2) == 0)
    def _():
        state[...] = jnp.zeros(state.shape, F32)

    cos = cos_ref[...]
    sin = sin_ref[...]

    def rot(x):
        x1, x2 = x[:, :half], x[:, half:]
        return jnp.concatenate([x1 * cos - x2 * sin, x1 * sin + x2 * cos], axis=-1)

    q = rot(q_ref[0].astype(F32))
    k = rot(k_ref[0].astype(F32)) * (RET_QK_DIM ** -0.5)
    v = v_ref[0]
    qb = q.astype(BF16)
    s = lax.dot_general(qb, k.astype(BF16), _NT, preferred_element_type=F32) * dec_ref[0]
    o = jnp.dot(s.astype(BF16), v, preferred_element_type=F32)
    st = state[...]
    o = o + jnp.dot(qb, st.astype(BF16), preferred_element_type=F32) * qd_ref[0]
    kt = (k * kd_ref[0]).T.astype(BF16)
    state[...] = st * cd_ref[0] + jnp.dot(kt, v, preferred_element_type=F32)
    gate = gate_ref[0].astype(F32)
    o_ref[0] = (_rms(o, ng_ref[0]) * (gate * jax.nn.sigmoid(gate))).astype(o_ref.dtype)


def retention_core(proj, norm_g):
    B, S, _ = proj.shape
    H, dk, dv, L = RET_HEADS, RET_QK_DIM, RET_V_DIM, RET_BLOCK
    assert S % L == 0 and L % CHUNK == 0
    inv = 1.0 / (ROPE_BASE ** (jnp.arange(0, dk, 2, dtype=F32) / dk))
    ang = jnp.arange(S, dtype=F32)[:, None] * inv[None, :]
    log_gamma = jnp.log1p(-jnp.exp2(-5.0 - jnp.arange(H, dtype=F32)))
    idx = jnp.arange(L, dtype=F32)
    visible = (jnp.arange(L)[None, :] // CHUNK) <= (jnp.arange(L)[:, None] // CHUNK)
    decay = jnp.where(visible[None],
                      jnp.exp(log_gamma[:, None, None] * jnp.abs(idx[:, None] - idx[None, :])), 0.0)
    q_decay = jnp.exp(log_gamma[:, None] * (idx[None, :] + 1.0))[..., None]
    k_decay = jnp.exp(log_gamma[:, None] * (L - 1.0 - idx[None, :]))[..., None]
    block_decay = jnp.exp(log_gamma * L).reshape(H, 1, 1)
    return pl.pallas_call(
        _retention_kernel,
        grid=(B, H, S // L),
        in_specs=[pl.BlockSpec((1, L, dk), lambda b, h, c: (b, c, h)),
                  pl.BlockSpec((1, L, dk), lambda b, h, c: (b, c, H + h)),
                  pl.BlockSpec((1, L, dv), lambda b, h, c: (b, c, H + h)),
                  pl.BlockSpec((1, L, dv), lambda b, h, c: (b, c, 2 * H + h)),
                  pl.BlockSpec((L, dk // 2), lambda b, h, c: (c, 0)),
                  pl.BlockSpec((L, dk // 2), lambda b, h, c: (c, 0)),
                  pl.BlockSpec((1, L, L), lambda b, h, c: (h, 0, 0)),
                  pl.BlockSpec((1, L, 1), lambda b, h, c: (h, 0, 0)),
                  pl.BlockSpec((1, L, 1), lambda b, h, c: (h, 0, 0)),
                  pl.BlockSpec((1, 1, 1), lambda b, h, c: (h, 0, 0)),
                  pl.BlockSpec((1, 1, dv), lambda b, h, c: (h, 0, 0))],
        out_specs=pl.BlockSpec((1, L, dv), lambda b, h, c: (b, c, h)),
        out_shape=jax.ShapeDtypeStruct((B, S, RET_V_WIDTH), BF16),
        scratch_shapes=[pltpu.VMEM((dk, dv), F32)],
        compiler_params=_params("parallel", "parallel", "arbitrary"),
        name="retention_core",
    )(proj, proj, proj, proj, jnp.cos(ang), jnp.sin(ang), decay, q_decay, k_decay, block_decay,
      norm_g.reshape(H, 1, dv))


def _attn_kernel(*refs, mode, T):
    if mode == "diff":
        q_ref, k_ref, v_ref, bias_ref, lam_ref, ng_ref, o_ref, m_s, acc_s, l_s = refs
    else:
        q_ref, k_ref, v_ref, ck_ref, cq_ref, o_ref, m_s, acc_s, cq_s = refs
    i = pl.program_id(2)
    NQ = ATTN_QUERY_TILES
    half = HEAD_LANES // 2
    reps = T // HEAD_LANES
    lane = lax.broadcasted_iota(jnp.int32, (1, HEAD_LANES), 1)
    first = lane < half
    second = jnp.logical_not(first)
    qs = []
    for u in range(NQ):
        q = (q_ref[0, u * T:(u + 1) * T, :].astype(F32) * (half ** -0.5 * LOG2E)).astype(BF16)
        zero = jnp.zeros_like(q)
        qs.append((jnp.where(first, q, zero), jnp.where(first, zero, q)))

    m_s[...] = jnp.full(m_s.shape, -jnp.inf, F32)
    acc_s[...] = jnp.zeros(acc_s.shape, F32)
    if mode == "diff":
        l_s[...] = jnp.zeros(l_s.shape, F32)
    else:
        for u in range(NQ):
            for a in range(2):
                cq_s[u, a] = jnp.broadcast_to(cq_ref[0, 0, u * T:(u + 1) * T, a:a + 1], (T, T))

    def step(j, subtiles, diagonal):
        start = pl.multiple_of(j * T, T)
        k = k_ref[0, pl.ds(start, T), :]
        v = v_ref[0, pl.ds(start, T), :]
        if mode == "diff":
            vs = (v, v)
        else:
            ones = jnp.ones_like(v)
            vs = (jnp.where(first, v, ones), jnp.where(second, v, ones))
        if diagonal is not None:
            row = lax.broadcasted_iota(jnp.int32, (T, T), 0)
            col = lax.broadcasted_iota(jnp.int32, (T, T), 1)
            if mode == "diff":
                visible = (col // CHUNK) <= (row // CHUNK)
            else:
                visible = col <= row
        for u in subtiles:
            for a in range(2):
                s = lax.dot_general(qs[u][a], k, _NT, preferred_element_type=F32)
                if mode == "diff":
                    s = s + bias_ref[0, jnp.minimum(NQ * i + u - j, 2)]
                else:
                    s = (s - ck_ref[0, 0, j, a:a + 1, :]) + cq_s[u, a]
                if u == diagonal:
                    s = jnp.where(visible, s, -jnp.inf)
                m_prev = m_s[u, a]
                m_new = jnp.maximum(m_prev, jnp.max(s, axis=-1, keepdims=True))
                alpha = jnp.exp2(m_prev - m_new)
                p = jnp.exp2(s - jnp.concatenate([m_new] * reps, axis=1))
                if mode == "diff":
                    l_s[u, a] = alpha * l_s[u, a] + jnp.sum(p, axis=-1, keepdims=True)
                acc_s[u, a] = alpha * acc_s[u, a] + jnp.dot(p.astype(BF16), vs[a], preferred_element_type=F32)
                m_s[u, a] = m_new

    everyone = tuple(range(NQ))
    lax.fori_loop(0, NQ * i, lambda j, c: (step(j, everyone, None), c)[1], 0)
    for d in range(NQ):
        step(NQ * i + d, everyone[d:], d)

    for u in range(NQ):
        if mode == "diff":
            o = acc_s[u, 0] / l_s[u, 0] - lam_ref[0] * (acc_s[u, 1] / l_s[u, 1])
            o = _rms(o, ng_ref[...])
        else:
            o0, o1 = acc_s[u, 0], acc_s[u, 1]
            o = jnp.where(first, o0 / pltpu.roll(o0, half, 1), o1 / pltpu.roll(o1, half, 1))
        o_ref[0, u * T:(u + 1) * T, :] = o.astype(o_ref.dtype)


def _attn_call(mode, proj, n_blocks, extra_inputs, extra_specs):
    B, S, _ = proj.shape
    T, NQ = ATTN_TILE, ATTN_QUERY_TILES
    assert S % (NQ * T) == 0
    stats = pltpu.VMEM((NQ, 2, T, HEAD_LANES), F32)
    return pl.pallas_call(
        functools.partial(_attn_kernel, mode=mode, T=T),
        grid=(B, n_blocks, S // (NQ * T)),
        in_specs=[pl.BlockSpec((1, NQ * T, HEAD_LANES), lambda b, h, i: (b, i, h)),
                  pl.BlockSpec((1, S, HEAD_LANES), lambda b, h, i: (b, 0, n_blocks + h)),
                  pl.BlockSpec((1, S, HEAD_LANES), lambda b, h, i: (b, 0, 2 * n_blocks + h))] + extra_specs,
        out_specs=pl.BlockSpec((1, NQ * T, HEAD_LANES), lambda b, h, i: (b, i, h)),
        out_shape=jax.ShapeDtypeStruct((B, S, D_MODEL), BF16),
        scratch_shapes=[stats, stats, stats if mode == "diff" else pltpu.VMEM((NQ, 2, T, T), F32)],
        compiler_params=_params("parallel", "parallel", "arbitrary"),
        name=mode + "_attention",
    )(proj, proj, proj, *extra_inputs)


def _t5_bucket(rel):
    nb = REL_BUCKETS // 2
    max_exact = nb // 2
    ret = (rel > 0).astype(jnp.int32) * nb
    n = jnp.abs(rel)
    nf = jnp.maximum(n, 1).astype(F32)
    large = max_exact + (jnp.log(nf / max_exact) / math.log(REL_MAX_DIST / max_exact)
                         * (nb - max_exact)).astype(jnp.int32)
    large = jnp.minimum(large, nb - 1)
    return ret + jnp.where(n < max_exact, n, large)


def _bias_tiles_kernel(bucket_ref, table_ref, o_ref):
    h = pl.program_id(0)
    for d in range(bucket_ref.shape[0]):
        bucket = bucket_ref[d]
        tile = jnp.zeros(bucket.shape, F32)
        for b in range(REL_BUCKETS):
            tile = jnp.where(bucket == b, table_ref[b * DIFF_HEADS + h] * LOG2E, tile)
        o_ref[0, d] = tile


def diff_bias_tiles(rel_bias):
    T = ATTN_TILE
    assert T >= REL_MAX_DIST
    off = jnp.arange(T)
    rel = (off[None, None, :] - off[None, :, None]) - (jnp.arange(3) * T)[:, None, None]
    return pl.pallas_call(
        _bias_tiles_kernel,
        grid=(DIFF_HEADS,),
        in_specs=[pl.BlockSpec((3, T, T), lambda h: (0, 0, 0)),
                  pl.BlockSpec(memory_space=pltpu.SMEM)],
        out_specs=pl.BlockSpec((1, 3, T, T), lambda h: (h, 0, 0, 0)),
        out_shape=jax.ShapeDtypeStruct((DIFF_HEADS, 3, T, T), F32),
        compiler_params=_params("parallel"),
        name="diff_bias_tiles",
    )(_t5_bucket(rel), rel_bias.astype(F32).reshape(REL_BUCKETS * DIFF_HEADS))


def diff_core(proj, lam_vecs, norm_g, rel_bias, lam_init):
    T = ATTN_TILE
    lv = lam_vecs.astype(F32)
    lam = jnp.exp(jnp.sum(lv[0] * lv[1])) - jnp.exp(jnp.sum(lv[2] * lv[3])) + lam_init
    bias = diff_bias_tiles(rel_bias)
    g = (norm_g * (1.0 - lam_init)).reshape(1, HEAD_LANES)
    extra_specs = [pl.BlockSpec((1, 3, T, T), lambda b, h, i: (h, 0, 0, 0)),
                   pl.BlockSpec(memory_space=pltpu.SMEM),
                   pl.BlockSpec((1, HEAD_LANES), lambda b, h, i: (0, 0))]
    return _attn_call("diff", proj, DIFF_HEADS, [bias, lam.reshape(1), g], extra_specs)


def _fox_gate_kernel(h_ref, g_ref, w_ref, b_ref, ct_ref, c_ref, carry, *, L):
    @pl.when(pl.program_id(1) == 0)
    def _():
        carry[...] = jnp.zeros(carry.shape, F32)

    xn = _rms(h_ref[0], g_ref[...])
    logits = lax.dot_general(w_ref[...], xn.astype(BF16), _NT, preferred_element_type=F32)
    log_f = jax.nn.log_sigmoid(logits + b_ref[...])
    row = lax.broadcasted_iota(jnp.int32, (L, L), 0)
    col = lax.broadcasted_iota(jnp.int32, (L, L), 1)
    triu = (row <= col).astype(BF16)
    hi, mid, lo = _split3(log_f)
    d = functools.partial(jnp.dot, preferred_element_type=F32)
    c = carry[...] + ((d(hi, triu) + d(mid, triu)) + d(lo, triu))
    carry[...] = c[:, L - 1:L]
    c2 = c * LOG2E
    ct_ref[0] = c2
    c_ref[0] = c2.T


def fox_gate(h3, g, w_f, b_f):
    B, S, D = h3.shape
    L = FOX_GATE_BLOCK
    assert S % L == 0
    pad = LANE - FOX_HEADS
    return pl.pallas_call(
        functools.partial(_fox_gate_kernel, L=L),
        grid=(B, S // L),
        in_specs=[pl.BlockSpec((1, L, D), lambda b, c: (b, c, 0)),
                  pl.BlockSpec((1, D), lambda b, c: (0, 0)),
                  pl.BlockSpec((LANE, D), lambda b, c: (0, 0)),
                  pl.BlockSpec((LANE, 1), lambda b, c: (0, 0))],
        out_specs=[pl.BlockSpec((1, LANE, L), lambda b, c: (b, 0, c)),
                   pl.BlockSpec((1, L, LANE), lambda b, c: (b, c, 0))],
        out_shape=[jax.ShapeDtypeStruct((B, LANE, S), F32), jax.ShapeDtypeStruct((B, S, LANE), F32)],
        scratch_shapes=[pltpu.VMEM((LANE, 1), F32)],
        compiler_params=_params("parallel", "arbitrary"),
        name="fox_gate",
    )(h3, g.reshape(1, D), jnp.pad(w_f.T, ((0, pad), (0, 0))).astype(BF16),
      jnp.pad(b_f, (0, pad)).reshape(LANE, 1))


def fox_core(proj, c_t, c_rows):
    B, S, _ = proj.shape
    T = ATTN_TILE
    nb = FOX_HEADS // 2
    ck = jnp.transpose(c_t[:, :FOX_HEADS].reshape(B, nb, 2, S // T, T), (0, 1, 3, 2, 4))
    cq = jnp.transpose(c_rows[:, :, :FOX_HEADS].reshape(B, S, nb, 2), (0, 2, 1, 3))
    extra_specs = [pl.BlockSpec((1, 1, S // T, 2, T), lambda b, h, i: (b, h, 0, 0, 0)),
                   pl.BlockSpec((1, 1, ATTN_QUERY_TILES * T, 2), lambda b, h, i: (b, h, i, 0))]
    return _attn_call("fox", proj, nb, [ck, cq], extra_specs)


def _candidate_tables():
    pairs = [(a, b) for a in range(PEER_TOPK) for b in range(PEER_TOPK) if (a + 1) * (b + 1) <= PEER_TOPK]
    rows = -(-len(pairs) // BF16_ROWS) * BF16_ROWS
    p1 = np.zeros((rows, LANE), np.float32)
    p2 = np.zeros((rows, LANE), np.float32)
    for r, (a, b) in enumerate(pairs):
        p1[r, a] = 1.0
        p2[r, b] = 1.0
    return len(pairs), p1, p2


N_CAND, _P1, _P2 = _candidate_tables()


_INT_MIN = -2 ** 31
_FLIP = 0x7FFFFFFF


def _ordered_int(x):
    b = lax.bitcast_convert_type(x + 0.0, jnp.int32)
    return jnp.where(b < 0, b ^ _FLIP, b)


def _ordered_float(k):
    return lax.bitcast_convert_type(jnp.where(k < 0, k ^ _FLIP, k), F32)


_FLOOR = -3.0e38
_MARK, _MARK_STEP = -3.2e38, 1.0e36


def _top16_ranks(x, break_ties):
    n, t = x.shape
    idx = lax.broadcasted_iota(jnp.int32, (n, t), 0)
    ridx = lax.broadcasted_iota(jnp.int32, (PEER_TOPK, t), 0)
    work = _ordered_int(x) if break_ties else jnp.maximum(x, _FLOOR)
    vals = jnp.zeros((PEER_TOPK, t), work.dtype)
    for r in range(PEER_TOPK):
        mx = jnp.max(work, axis=0, keepdims=True)
        hit = work == mx
        if break_ties:
            pos = jnp.min(jnp.where(hit, idx, n), axis=0, keepdims=True)
            hit = idx == pos
        work = jnp.where(hit, _INT_MIN + r if break_ties else _MARK - r * _MARK_STEP, work)
        vals = jnp.where(ridx == r, mx, vals)
    if break_ties:
        rank = jnp.where(work < _INT_MIN + PEER_TOPK, work & (2 * PEER_TOPK - 1), PEER_TOPK)
        return rank, _ordered_float(vals)
    rank = jnp.where(work < 0.5 * (_FLOOR + _MARK),
                     jnp.round((_MARK - work) * (1.0 / _MARK_STEP)).astype(jnp.int32), PEER_TOPK)
    return rank, vals


def _peer_select_kernel(q_ref, keys_ref, p1_ref, p2_ref, p1t_ref, rank2_ref, e2_ref, nrow_ref, e1_ref):
    def scores(c):
        kk = keys_ref[0, c]
        qq = q_ref[:, c * PEER_HALF:(c + 1) * PEER_HALF]
        kh, km, _ = _split3(kk)
        qh, qm, _ = _split3(qq)
        d = lambda a, b: lax.dot_general(a, b, _NT, preferred_element_type=F32)
        return d(kh, qh) + (d(kh, qm) + d(km, qh))

    s1 = scores(0)
    s2 = scores(1)
    out_refs = (rank2_ref, e2_ref, nrow_ref, e1_ref)
    clean = _peer_select_pass(s1, s2, p1_ref, p2_ref, p1t_ref, out_refs, break_ties=False)

    @pl.when(jnp.logical_not(clean))
    def _():
        _peer_select_pass(s1, s2, p1_ref, p2_ref, p1t_ref, out_refs, break_ties=True)


def _peer_select_pass(s1, s2, p1_ref, p2_ref, p1t_ref, out_refs, break_ties):
    rank2_ref, e2_ref, nrow_ref, e1_ref = out_refs
    rank1, v1 = _top16_ranks(s1, break_ties)
    rank2, v2 = _top16_ranks(s2, break_ties)

    tokens = s1.shape[1]
    pad = jnp.zeros((LANE - PEER_TOPK, tokens), F32)
    cand = (_dot_exact_lhs01(p1_ref[...], jnp.concatenate([v1, pad], axis=0))
            + _dot_exact_lhs01(p2_ref[...], jnp.concatenate([v2, pad], axis=0)))
    cidx = lax.broadcasted_iota(jnp.int32, cand.shape, 0)
    cand = jnp.where(cidx < N_CAND, cand, -jnp.inf)
    cand_rank, _ = _top16_ranks(cand, break_ties)
    sel = jnp.where(cand_rank < PEER_TOPK, 1.0, 0.0)
    top = v1[0:1, :] + v2[0:1, :]
    z = jnp.sum(sel * jnp.exp(jnp.where(sel > 0.0, cand - top, 0.0)), axis=0, keepdims=True)
    sel_pad = jnp.concatenate([sel, jnp.zeros((LANE - sel.shape[0], tokens), F32)], axis=0).astype(BF16)
    n_by_rank = jnp.dot(p1t_ref[...], sel_pad, preferred_element_type=F32)
    nrow = jnp.zeros(s1.shape, F32)
    for a in range(PEER_TOPK):
        nrow = jnp.where(rank1 == a, n_by_rank[a:a + 1, :], nrow)

    rank2_ref[0] = rank2.astype(F32).astype(BF16)
    e2_ref[0] = jnp.exp(s2 - v2[0:1, :]).astype(BF16)
    e1 = jnp.exp(s1 - v1[0:1, :]) * (0.5 / z)
    for g in range(PEER_KEYS // SUBLANE):
        for st in range(tokens // LANE):
            tile = (slice(g * SUBLANE, (g + 1) * SUBLANE), slice(st * LANE, (st + 1) * LANE))
            nrow_ref[0, g, st] = nrow[tile]
            e1_ref[0, g, st] = e1[tile]

    if break_ties:
        return None
    ranked = (jnp.sum(jnp.where(rank1 < PEER_TOPK, 1.0, 0.0), axis=0, keepdims=True)
              + jnp.sum(jnp.where(rank2 < PEER_TOPK, 1.0, 0.0), axis=0, keepdims=True)
              + jnp.sum(sel, axis=0, keepdims=True))
    return jnp.max(ranked) == 3.0 * PEER_TOPK


def peer_select(q, keys):
    T = q.shape[0]
    tm = PEER_SELECT_TOKENS
    assert T % tm == 0
    hk = pl.BlockSpec((1, PEER_KEYS, tm), lambda i, h: (h, 0, i))
    groups = PEER_KEYS // SUBLANE
    hk1 = pl.BlockSpec((1, groups, tm // LANE, SUBLANE, LANE), lambda i, h: (h, 0, i, 0, 0))
    shp = lambda dt: jax.ShapeDtypeStruct((PEER_HEADS, PEER_KEYS, T), dt)
    shp1 = jax.ShapeDtypeStruct((PEER_HEADS, groups, T // LANE, SUBLANE, LANE), F32)
    cst = lambda a: pl.BlockSpec(a.shape, lambda i, h: (0, 0))
    p1, p2 = jnp.asarray(_P1, BF16), jnp.asarray(_P2, BF16)
    p1t = jnp.asarray(np.pad(_P1.T, ((0, 0), (0, LANE - _P1.shape[0]))), BF16)
    return pl.pallas_call(
        _peer_select_kernel,
        grid=(T // tm, PEER_HEADS),
        in_specs=[pl.BlockSpec((tm, PEER_QUERY_DIM), lambda i, h: (i, h)),
                  pl.BlockSpec((1, 2, PEER_KEYS, PEER_HALF), lambda i, h: (h, 0, 0, 0)),
                  cst(p1), cst(p2), cst(p1t)],
        out_specs=[hk, hk, hk1, hk1],
        out_shape=[shp(BF16), shp(BF16), shp1, shp1],
        compiler_params=_params("parallel", "parallel"),
        name="peer_select",
    )(q, keys, p1, p2, p1t)


def _peer_dense_kernel(xt_ref, u_ref, u_next_ref, vt_ref, rank2_ref, e2_ref, nrow_ref, e1_ref,
                       nrow_next_ref, e1_next_ref, o_ref, *scratch):
    n = xt_ref.shape[0]
    acc, gbuf, hbuf, wbuf = scratch[:n], scratch[n:2 * n], scratch[2 * n:2 * n + 2], scratch[2 * n + 2:]
    j = pl.program_id(1)
    rows = PEER_EXPERT_TILE // PEER_KEYS
    tc = PEER_TOKEN_CHUNK
    n_chunks = xt_ref.shape[0]
    pack = BF16_ROWS

    def first_matmul(c, u):
        hbuf[c % 2][...] = jnp.dot(u[...], xt_ref[c], preferred_element_type=F32).astype(BF16)

    def gate_weights(c, nrow, e1):
        cols = slice(c * tc, (c + 1) * tc)
        for ii in range(rows):
            w = jnp.zeros((PEER_KEYS, tc), BF16)
            for h in range(PEER_HEADS):
                def spread(ref):
                    tiles = [jnp.broadcast_to(ref[h, 0, c * (tc // LANE) + lt, ii:ii + 1, :],
                                              (pack, LANE)).astype(BF16) for lt in range(tc // LANE)]
                    return jnp.concatenate([jnp.concatenate(tiles, axis=1)] * (PEER_KEYS // pack), axis=0)
                w = w + jnp.where(rank2_ref[h, :, cols] < spread(nrow), e2_ref[h, :, cols] * spread(e1),
                                  jnp.zeros_like(w))
            wbuf[c % 2][ii * PEER_KEYS:(ii + 1) * PEER_KEYS, :] = w

    def finish(c):
        for ii in range(rows):
            r = slice(ii * PEER_KEYS, (ii + 1) * PEER_KEYS)
            hr = hbuf[c % 2][r, :]
            act = hr * (1.0 + lax.erf(hr * jnp.asarray(2.0 ** -0.5, BF16)))
            gbuf[c][r, :] = wbuf[c % 2][r, :] * act
        acc[c][...] += jnp.dot(vt_ref[...], gbuf[c][...], preferred_element_type=F32)

    @pl.when(j == 0)
    def _():
        for a in acc:
            a[...] = jnp.zeros(a.shape, F32)
        first_matmul(0, u_ref)
        gate_weights(0, nrow_ref, e1_ref)

    for c in range(n_chunks):
        if c + 1 < n_chunks:
            first_matmul(c + 1, u_ref)
            gate_weights(c + 1, nrow_ref, e1_ref)
        else:
            first_matmul(0, u_next_ref)
            gate_weights(0, nrow_next_ref, e1_next_ref)
        finish(c)

    @pl.when(j == pl.num_programs(1) - 1)
    def _():
        for c in range(n_chunks):
            o_ref[c * tc:(c + 1) * tc, :] = acc[c][...].T


def peer_dense(xt, u, vt, rank2, e2, nrow, e1):
    n_slabs, D, tc = xt.shape
    T = n_slabs * tc
    tm, te = PEER_TOKENS, PEER_EXPERT_TILE
    rows = te // PEER_KEYS
    n_tiles = PEER_EXPERTS // te
    assert rows == SUBLANE and tc == PEER_TOKEN_CHUNK and (tm // tc) % 2 == 0
    assert T % tm == 0
    nxt = lambda j: jnp.minimum(j + 1, n_tiles - 1)
    by_key2 = pl.BlockSpec((PEER_HEADS, PEER_KEYS, tm), lambda i, j: (0, 0, i))
    by_key1 = pl.BlockSpec((PEER_HEADS, 1, tm // LANE, rows, LANE), lambda i, j: (0, j, i, 0, 0))
    by_key1_next = pl.BlockSpec((PEER_HEADS, 1, tm // LANE, rows, LANE), lambda i, j: (0, nxt(j), i, 0, 0))
    return pl.pallas_call(
        _peer_dense_kernel,
        grid=(T // tm, n_tiles),
        in_specs=[pl.BlockSpec((tm // tc, D, tc), lambda i, j: (i, 0, 0)),
                  pl.BlockSpec((te, D), lambda i, j: (j, 0)),
                  pl.BlockSpec((te, D), lambda i, j: (nxt(j), 0)),
                  pl.BlockSpec((D, te), lambda i, j: (0, j)),
                  by_key2, by_key2, by_key1, by_key1, by_key1_next, by_key1_next],
        out_specs=pl.BlockSpec((tm, D), lambda i, j: (i, 0)),
        out_shape=jax.ShapeDtypeStruct((T, D), F32),
        scratch_shapes=([pltpu.VMEM((D, tc), F32)] * (tm // tc) + [pltpu.VMEM((te, tc), BF16)] * (tm // tc)
                        + [pltpu.VMEM((te, tc), BF16)] * 4),
        compiler_params=pltpu.CompilerParams(dimension_semantics=("parallel", "arbitrary"),
                                             vmem_limit_bytes=PEER_DENSE_VMEM),
        name="peer_dense",
    )(xt, u, u, vt, rank2, e2, nrow, e1, nrow, e1)


def peer_layer(h, g, w_q, keys, u, v):
    q, xt = norm_matmul(h, g, w_q.astype(BF16), F32, emit_t=True)
    rank2, e2, nrow, e1 = peer_select(q, keys)
    return peer_dense(xt, u.astype(BF16), v.T.astype(BF16), rank2, e2, nrow, e1)


def _pad_cols(w, n):
    return jnp.pad(w, ((0, 0), (0, n - w.shape[1])))


def kernel(x, p, mix_norm, ffn_norm, ple_norm, final_norm, ssd_w_in, ssd_conv_w, ssd_conv_b, ssd_dt_bias, ssd_a_log, ssd_d, ssd_norm, ssd_w_out, ret_w_in, ret_norm, ret_w_out, diff_w_in, diff_lambda, diff_norm, diff_w_out, fox_w_in, fox_b_f, fox_w_out, rel_bias, peer_w_q, peer_keys, peer_u, peer_v, ple_proj, ple_gate):
    B, S, D = x.shape
    T = B * S
    depth = mix_norm.shape[0]
    n_mixers = 4
    h = x.reshape(T, D)
    p_all = p.reshape(depth, T, PLE_DIM)
    for i in range(depth):
        m, j = i % n_mixers, i // n_mixers
        g = mix_norm[i]
        if m == 0:
            w = ssd_w_in[j]
            proj, dt_raw = norm_matmul(h, g, w[:, :SSD_MAIN].astype(BF16), BF16,
                                       w_tail=_pad_cols(w[:, SSD_MAIN:], LANE).astype(BF16))
            y = ssd_core(proj.reshape(B, S, SSD_MAIN), dt_raw.reshape(B, S, LANE), ssd_conv_w[j],
                         ssd_conv_b[j], ssd_dt_bias[j], ssd_a_log[j], ssd_d[j], ssd_norm[j])
            w_out = ssd_w_out[j]
        elif m == 1:
            proj = norm_matmul(h, g, ret_w_in[j].astype(BF16), BF16)
            y = retention_core(proj.reshape(B, S, -1), ret_norm[j])
            w_out = ret_w_out[j]
        elif m == 2:
            lam_init = 0.8 - 0.6 * math.exp(-0.3 * i)
            proj = norm_matmul(h, g, diff_w_in[j].astype(BF16), BF16)
            y = diff_core(proj.reshape(B, S, -1), diff_lambda[j], diff_norm[j], rel_bias, lam_init)
            w_out = diff_w_out[j]
        else:
            w = fox_w_in[j]
            proj = norm_matmul(h, g, w[:, :3 * D].astype(BF16), BF16)
            c_t, c_rows = fox_gate(h.reshape(B, S, D), g, w[:, 3 * D:], fox_b_f[j])
            y = fox_core(proj.reshape(B, S, -1), c_t, c_rows)
            w_out = fox_w_out[j]
        h = matmul_residual(y.reshape(T, -1), w_out.astype(BF16), h)
        y = peer_layer(h, ffn_norm[i], peer_w_q[i], peer_keys[i], peer_u[i], peer_v[i])
        h = ple_layer(h, y, ple_norm[i], ple_gate[i].astype(BF16), p_all, i, ple_proj[i].astype(BF16),
                      final_norm, final=(i == depth - 1))
    return h.reshape(B, S, D)
```
